```python
import jax, jax.numpy as jnp
from jax import lax
import numpy as np

D_MODEL = 1024
BATCH = 2
SEQ = 8192
DEPTH = 2

CHUNK = 128
A_GROUPS = 8
A_WIDTH = 512
A_HEAD = A_WIDTH // A_GROUPS
B_WIDTH = 512
CONV_WIDTH = 3
C_WIDTH = 512
POOL_WINDOWS = (2, 4, 8, 16)
C_GROUP = C_WIDTH // len(POOL_WINDOWS)
IN_TOTAL = 3 * A_WIDTH + 4 * B_WIDTH + 2 * C_WIDTH + 3 * D_MODEL
RMS_EPS = 1e-6
LN_EPS = 1e-5

kernel_name = "hybrid_gmlp_shortconv_pool_gated_merge"


def _rmsnorm(x, g):
    xf = x.astype(jnp.float32)
    y = xf * lax.rsqrt(jnp.mean(xf * xf, axis=-1, keepdims=True) + RMS_EPS)
    return (y * g.astype(jnp.float32)).astype(x.dtype)


def _layernorm(x, g, b):
    xf = x.astype(jnp.float32)
    mu = jnp.mean(xf, axis=-1, keepdims=True)
    xc = xf - mu
    var = jnp.mean(xc * xc, axis=-1, keepdims=True)
    y = xc * lax.rsqrt(var + LN_EPS)
    return (y * g.astype(jnp.float32) + b.astype(jnp.float32)).astype(x.dtype)


def _split_points():
    widths = [A_WIDTH] * 3 + [B_WIDTH] * 4 + [C_WIDTH] * 2 + [D_MODEL] * 3
    return [int(s) for s in np.cumsum(widths)[:-1]]


def _gmlp_branch(u, v, ln_g, ln_b, w_s, b_s):
    u = jax.nn.gelu(u)
    v = _layernorm(jax.nn.gelu(v), ln_g, ln_b)
    bsz, s, _ = v.shape
    vc = v.reshape(bsz, s // CHUNK, CHUNK, A_GROUPS, A_HEAD)
    causal = jnp.tril(jnp.ones((CHUNK, CHUNK), dtype=bool))
    w_m = jnp.where(causal, w_s, 0.0)
    sg = jnp.einsum('gts,bnsgc->bntgc', w_m, vc) + b_s.T[:, :, None]
    return u * sg.reshape(bsz, s, A_WIDTH)


def _shortconv_branch(xb, bg, cg, conv_w, conv_b):
    y = cg * xb
    y = lax.conv_general_dilated(
        y, conv_w[:, None, :].astype(y.dtype), window_strides=(1,),
        padding=[(CONV_WIDTH - 1, 0)], dimension_numbers=('NWC', 'WIO', 'NWC'),
        feature_group_count=B_WIDTH) + conv_b
    return bg * y


def _pool_branch(xc, w_pool, pool_scale):
    bsz, s, _ = xc.shape
    xf = xc.astype(jnp.float32).reshape(bsz, s, len(POOL_WINDOWS), C_GROUP)
    cs = jnp.cumsum(xf, axis=1)
    t_count = jnp.arange(1, s + 1, dtype=jnp.float32)
    pooled = []
    for gi, w in enumerate(POOL_WINDOWS):
        c = cs[:, :, gi]
        lag = jnp.pad(c[:, :s - w], ((0, 0), (w, 0), (0, 0)))
        cnt = jnp.minimum(t_count, float(w))[None, :, None]
        pooled.append((c - lag) / cnt)
    pooled = (jnp.stack(pooled, axis=2) - xf).astype(xc.dtype)
    y = jnp.einsum('bsgc,gcd->bsgd', pooled, w_pool).reshape(bsz, s, C_WIDTH)
    return y * pool_scale


def _hybrid_layer(x, norm_g, w_in, ln_g, ln_b, w_s, b_s, conv_w, conv_b,
                  w_pool, pool_scale, w_pa, w_pb, w_pc, w_o):
    h = _rmsnorm(x, norm_g)
    p = h @ w_in
    (u, v, z_a, x_b, b_g, c_g, z_b, x_c, z_c,
     g_a, g_b, g_c) = jnp.split(p, _split_points(), axis=-1)
    y_a = (_gmlp_branch(u, v, ln_g, ln_b, w_s, b_s) * jax.nn.silu(z_a)) @ w_pa
    y_b = (_shortconv_branch(x_b, b_g, c_g, conv_w, conv_b) * jax.nn.silu(z_b)) @ w_pb
    y_c = (_pool_branch(x_c, w_pool, pool_scale) * jax.nn.silu(z_c)) @ w_pc
    merged = (jax.nn.sigmoid(g_a) * y_a + jax.nn.sigmoid(g_b) * y_b
              + jax.nn.sigmoid(g_c) * y_c)
    return x + merged @ w_o


def setup_inputs(seed: int = 0) -> dict:
    key = jax.random.key(seed)
    ks = jax.random.split(key, 20)
    f32 = jnp.float32
    n = lambda k, shape: jax.random.normal(k, shape, dtype=f32)
    return {
        "x": n(ks[0], (BATCH, SEQ, D_MODEL)),
        "norm_g": 1.0 + 0.02 * n(ks[1], (DEPTH, D_MODEL)),
        "w_in": n(ks[2], (DEPTH, D_MODEL, IN_TOTAL)) * D_MODEL ** -0.5,
        "ln_g": 1.0 + 0.02 * n(ks[3], (DEPTH, A_WIDTH)),
        "ln_b": 0.02 * n(ks[4], (DEPTH, A_WIDTH)),
        "w_s": n(ks[5], (DEPTH, A_GROUPS, CHUNK, CHUNK)) * CHUNK ** -0.5,
        "b_s": 1.0 + 0.1 * n(ks[6], (DEPTH, A_GROUPS, CHUNK)),
        "conv_w": n(ks[7], (DEPTH, CONV_WIDTH, B_WIDTH)) * CONV_WIDTH ** -0.5,
        "conv_b": 0.02 * n(ks[8], (DEPTH, B_WIDTH)),
        "w_pool": n(ks[9], (DEPTH, len(POOL_WINDOWS), C_GROUP, C_GROUP)) * C_GROUP ** -0.5,
        "pool_scale": 1.0 + 0.02 * n(ks[10], (DEPTH, C_WIDTH)),
        "w_pa": n(ks[11], (DEPTH, A_WIDTH, D_MODEL)) * A_WIDTH ** -0.5,
        "w_pb": n(ks[12], (DEPTH, B_WIDTH, D_MODEL)) * B_WIDTH ** -0.5,
        "w_pc": n(ks[13], (DEPTH, C_WIDTH, D_MODEL)) * C_WIDTH ** -0.5,
        "w_o": n(ks[14], (DEPTH, D_MODEL, D_MODEL)) * D_MODEL ** -0.5,
        "final_g": 1.0 + 0.02 * n(ks[15], (D_MODEL,)),
    }


def reference(x, norm_g, w_in, ln_g, ln_b, w_s, b_s, conv_w, conv_b, w_pool,
              pool_scale, w_pa, w_pb, w_pc, w_o, final_g):
    for l in range(DEPTH):
        x = _hybrid_layer(x, norm_g[l], w_in[l], ln_g[l], ln_b[l], w_s[l], b_s[l],
                          conv_w[l], conv_b[l], w_pool[l], pool_scale[l],
                          w_pa[l], w_pb[l], w_pc[l], w_o[l])
    return _rmsnorm(x, final_g)
```

```python
import functools

import jax
import jax.numpy as jnp
from jax import lax
from jax.experimental import pallas as pl
from jax.experimental.pallas import tpu as pltpu

D_MODEL = 1024
DEPTH = 2
CHUNK = 128
A_GROUPS = 8
A_WIDTH = 512
A_HEAD = A_WIDTH // A_GROUPS
B_WIDTH = 512
CONV_WIDTH = 3
C_WIDTH = 512
POOL_WINDOWS = (2, 4, 8, 16)
C_GROUP = C_WIDTH // len(POOL_WINDOWS)
IN_TOTAL = 3 * A_WIDTH + 4 * B_WIDTH + 2 * C_WIDTH + 3 * D_MODEL
RMS_EPS = 1e-6
LN_EPS = 1e-5

_WIDTHS = [A_WIDTH] * 3 + [B_WIDTH] * 4 + [C_WIDTH] * 2 + [D_MODEL] * 3
_OFFS = [sum(_WIDTHS[:i]) for i in range(len(_WIDTHS) + 1)]
(SEG_U, SEG_V, SEG_ZA, SEG_XB, SEG_BG, SEG_CG, SEG_ZB, SEG_XC, SEG_ZC,
 SEG_GA, SEG_GB, SEG_GC) = [(_OFFS[i], _OFFS[i + 1]) for i in range(len(_WIDTHS))]

LANES = 128
HALO = 16
TS = 256
VMEM_LIMIT_BYTES = 56 * 1024 * 1024


def _sigmoid(x):
    return 0.5 * jnp.tanh(0.5 * x) + 0.5


def _silu(x):
    return x * _sigmoid(x)


def _layer_kernel(x_ref, norm_g_ref, w_in_ref, ln_g_ref, ln_b_ref, w_sp_ref, b_sp_ref,
                  conv_w_ref, conv_b_ref, w_pool_ref, pool_scale_ref,
                  w_pa_ref, w_pb_ref, w_pc_ref, w_o_ref, final_g_ref,
                  out_ref, h_ref, cx_ref, xc_ref, *, apply_final_norm):
    f32 = jnp.float32
    bf16 = jnp.bfloat16
    j = pl.program_id(1)

    @pl.when(j == 0)
    def _():
        cx_ref[0:HALO, :] = jnp.zeros((HALO, B_WIDTH), f32)
        xc_ref[0:HALO, :] = jnp.zeros((HALO, C_WIDTH), f32)

    x = x_ref[0]
    ms = jnp.mean(x * x, axis=-1, keepdims=True)
    h_ref[...] = (x * lax.rsqrt(ms + RMS_EPS) * norm_g_ref[...]).astype(bf16)

    def proj(seg):
        return jnp.dot(h_ref[...], w_in_ref[:, seg[0]:seg[1]], preferred_element_type=f32)

    u = jax.nn.gelu(proj(SEG_U))
    v = jax.nn.gelu(proj(SEG_V))
    mu = jnp.mean(v, axis=-1, keepdims=True)
    vc = v - mu
    var = jnp.mean(vc * vc, axis=-1, keepdims=True)
    vn = vc * lax.rsqrt(var + LN_EPS) * ln_g_ref[...] + ln_b_ref[...]
    lane = lax.broadcasted_iota(jnp.int32, (CHUNK, LANES), 1)
    first_group = lane < A_HEAD
    chunk_rows = []
    for c in range(TS // CHUNK):
        blocks = []
        for jb in range(A_WIDTH // LANES):
            vb = vn[c * CHUNK:(c + 1) * CHUNK, jb * LANES:(jb + 1) * LANES]
            rhs = jnp.concatenate([jnp.where(first_group, vb, 0.0),
                                   jnp.where(first_group, 0.0, vb)], axis=0).astype(bf16)
            blocks.append(jnp.dot(w_sp_ref[jb], rhs, preferred_element_type=f32))
        chunk_rows.append(jnp.concatenate(blocks, axis=1) + b_sp_ref[...])
    sg = jnp.concatenate(chunk_rows, axis=0)
    ya = (u * sg * _silu(proj(SEG_ZA))).astype(bf16)
    merged = _sigmoid(proj(SEG_GA)) * jnp.dot(ya, w_pa_ref[...], preferred_element_type=f32)

    cx_ref[HALO:HALO + TS, :] = proj(SEG_CG) * proj(SEG_XB)
    conv = conv_b_ref[...] + conv_w_ref[CONV_WIDTH - 1:CONV_WIDTH, :] * cx_ref[HALO:HALO + TS, :]
    for k in range(CONV_WIDTH - 1):
        back = CONV_WIDTH - 1 - k
        conv = conv + conv_w_ref[k:k + 1, :] * cx_ref[HALO - back:HALO - back + TS, :]
    yb = (proj(SEG_BG) * conv * _silu(proj(SEG_ZB))).astype(bf16)
    merged = merged + _sigmoid(proj(SEG_GB)) * jnp.dot(yb, w_pb_ref[...],
                                                        preferred_element_type=f32)
    cx_ref[0:HALO, :] = cx_ref[TS:TS + HALO, :]

    xc_ref[HALO:HALO + TS, :] = proj(SEG_XC)
    t1 = (j * TS + 1 + lax.broadcasted_iota(jnp.int32, (TS, C_GROUP), 0)).astype(f32)
    inv_t1 = 1.0 / t1
    pooled = []
    for gi, w in enumerate(POOL_WINDOWS):
        cols = slice(gi * C_GROUP, (gi + 1) * C_GROUP)
        cur = xc_ref[HALO:HALO + TS, cols]
        win = cur
        for back in range(1, w):
            win = win + xc_ref[HALO - back:HALO - back + TS, cols]
        pooled.append(win * jnp.maximum(inv_t1, 1.0 / w) - cur)
    yc_groups = []
    for half in range(2):
        lhs = jnp.concatenate(pooled[2 * half:2 * half + 2], axis=1).astype(bf16)
        yc_groups.append(jnp.dot(lhs, w_pool_ref[half], preferred_element_type=f32))
    yc = jnp.concatenate(yc_groups, axis=1) * pool_scale_ref[...]
    yc = (yc * _silu(proj(SEG_ZC))).astype(bf16)
    merged = merged + _sigmoid(proj(SEG_GC)) * jnp.dot(yc, w_pc_ref[...],
                                                        preferred_element_type=f32)
    xc_ref[0:HALO, :] = xc_ref[TS:TS + HALO, :]

    y = x_ref[0] + jnp.dot(merged.astype(bf16), w_o_ref[...], preferred_element_type=f32)
    if apply_final_norm:
        ms_y = jnp.mean(y * y, axis=-1, keepdims=True)
        y = y * lax.rsqrt(ms_y + RMS_EPS) * final_g_ref[...]
    out_ref[0] = y


def _const_spec(shape):
    zeros = (0,) * len(shape)
    return pl.BlockSpec(shape, lambda b, j: zeros, pipeline_mode=pl.Buffered(1))


def _layer(x, norm_g, w_in, ln_g, ln_b, w_s, b_s, conv_w, conv_b, w_pool, pool_scale,
           w_pa, w_pb, w_pc, w_o, final_g, *, apply_final_norm):
    bsz, seq, _ = x.shape
    assert seq % TS == 0 and TS % CHUNK == 0 and TS >= HALO
    bf16 = jnp.bfloat16
    f32 = jnp.float32

    causal = jnp.tril(jnp.ones((CHUNK, CHUNK), dtype=bool))
    w_m = jnp.where(causal, w_s, 0.0)
    w_sp = w_m.reshape(A_GROUPS // 2, 2, CHUNK, CHUNK).transpose(0, 2, 1, 3)
    w_sp = w_sp.reshape(A_GROUPS // 2, CHUNK, 2 * CHUNK).astype(bf16)
    b_sp = jnp.repeat(b_s.T, A_HEAD, axis=1).astype(f32)
    zero = jnp.zeros((C_GROUP, C_GROUP), w_pool.dtype)
    w_pool_bd = jnp.stack([
        jnp.block([[w_pool[2 * i], zero], [zero, w_pool[2 * i + 1]]]) for i in range(2)
    ]).astype(bf16)

    row = lambda a: a.reshape(1, -1).astype(f32)
    operands = (
        x, row(norm_g), w_in.astype(bf16), row(ln_g), row(ln_b), w_sp, b_sp,
        conv_w.astype(f32), row(conv_b), w_pool_bd, row(pool_scale),
        w_pa.astype(bf16), w_pb.astype(bf16), w_pc.astype(bf16), w_o.astype(bf16),
        row(final_g),
    )
    tile_spec = pl.BlockSpec((1, TS, D_MODEL), lambda b, j: (b, j, 0))
    in_specs = [tile_spec] + [_const_spec(a.shape) for a in operands[1:]]

    return pl.pallas_call(
        functools.partial(_layer_kernel, apply_final_norm=apply_final_norm),
        out_shape=jax.ShapeDtypeStruct(x.shape, x.dtype),
        grid=(bsz, seq // TS),
        in_specs=in_specs,
        out_specs=tile_spec,
        scratch_shapes=[
            pltpu.VMEM((TS, D_MODEL), bf16),
            pltpu.VMEM((TS + HALO, B_WIDTH), f32),
            pltpu.VMEM((TS + HALO, C_WIDTH), f32),
        ],
        compiler_params=pltpu.CompilerParams(
            dimension_semantics=("arbitrary", "arbitrary"),
            vmem_limit_bytes=VMEM_LIMIT_BYTES,
        ),
        name="hybrid_layer_final" if apply_final_norm else "hybrid_layer",
    )(*operands)


def kernel(x, norm_g, w_in, ln_g, ln_b, w_s, b_s, conv_w, conv_b, w_pool, pool_scale,
           w_pa, w_pb, w_pc, w_o, final_g):
    for l in range(DEPTH):
        x = _layer(x, norm_g[l], w_in[l], ln_g[l], ln_b[l], w_s[l], b_s[l], conv_w[l],
                   conv_b[l], w_pool[l], pool_scale[l], w_pa[l], w_pb[l], w_pc[l], w_o[l],
                   final_g, apply_final_norm=(l == DEPTH - 1))
    return x
```

```python
import functools

import jax
import jax.numpy as jnp
from jax import lax
from jax.experimental import pallas as pl
from jax.experimental.pallas import tpu as pltpu

D_MODEL = 1024
DEPTH = 2
CHUNK = 128
A_GROUPS = 8
A_WIDTH = 512
A_HEAD = A_WIDTH // A_GROUPS
B_WIDTH = 512
CONV_WIDTH = 3
C_WIDTH = 512
POOL_WINDOWS = (2, 4, 8, 16)
C_GROUP = C_WIDTH // len(POOL_WINDOWS)
IN_TOTAL = 3 * A_WIDTH + 4 * B_WIDTH + 2 * C_WIDTH + 3 * D_MODEL
RMS_EPS = 1e-6
LN_EPS = 1e-5

_WIDTHS = [A_WIDTH] * 3 + [B_WIDTH] * 4 + [C_WIDTH] * 2 + [D_MODEL] * 3
_OFFS = [sum(_WIDTHS[:i]) for i in range(len(_WIDTHS) + 1)]
(SEG_U, SEG_V, SEG_ZA, SEG_XB, SEG_BG, SEG_CG, SEG_ZB, SEG_XC, SEG_ZC,
 SEG_GA, SEG_GB, SEG_GC) = [(_OFFS[i], _OFFS[i + 1]) for i in range(len(_WIDTHS))]

LANES = 128
HALO = 16
TS = 512
VMEM_LIMIT_BYTES = 56 * 1024 * 1024


def _sigmoid(x):
    return 0.5 * jnp.tanh(0.5 * x) + 0.5


def _silu(x):
    return x * _sigmoid(x)


def _layer_kernel(x_ref, norm_g_ref, w_in_ref, ln_g_ref, ln_b_ref, w_sp_ref, b_sp_ref,
                  conv_w_ref, conv_b_ref, w_pool_ref, pool_scale_ref,
                  w_pa_ref, w_pb_ref, w_pc_ref, w_o_ref, final_g_ref,
                  out_ref, h_ref, cx_ref, xc_ref, *, apply_final_norm):
    f32 = jnp.float32
    bf16 = jnp.bfloat16
    j = pl.program_id(1)

    @pl.when(j == 0)
    def _():
        cx_ref[0:HALO, :] = jnp.zeros((HALO, B_WIDTH), f32)
        xc_ref[0:HALO, :] = jnp.zeros((HALO, C_WIDTH), f32)

    x = x_ref[0]
    ms = jnp.mean(x * x, axis=-1, keepdims=True)
    h_ref[...] = (x * lax.rsqrt(ms + RMS_EPS) * norm_g_ref[...]).astype(bf16)

    def proj(seg):
        return jnp.dot(h_ref[...], w_in_ref[:, seg[0]:seg[1]], preferred_element_type=f32)

    n_chunks = TS // CHUNK

    p_u = proj(SEG_U)
    p_v = proj(SEG_V)
    xc_ref[HALO:HALO + TS, :] = proj(SEG_XC)

    u = jax.nn.gelu(p_u)
    v = jax.nn.gelu(p_v)
    mu = jnp.mean(v, axis=-1, keepdims=True)
    vc = v - mu
    var = jnp.mean(vc * vc, axis=-1, keepdims=True)
    vn = vc * lax.rsqrt(var + LN_EPS) * ln_g_ref[...] + ln_b_ref[...]
    p_za = proj(SEG_ZA)
    cx_ref[HALO:HALO + TS, :] = proj(SEG_CG) * proj(SEG_XB)

    lane = lax.broadcasted_iota(jnp.int32, (CHUNK, LANES), 1)
    first_group = lane < A_HEAD
    sg_blocks = []
    for jb in range(A_WIDTH // LANES):
        rhs = []
        for c in range(n_chunks):
            vb = vn[c * CHUNK:(c + 1) * CHUNK, jb * LANES:(jb + 1) * LANES]
            rhs.append(jnp.concatenate([jnp.where(first_group, vb, 0.0),
                                        jnp.where(first_group, 0.0, vb)], axis=0))
        rhs = jnp.concatenate(rhs, axis=1).astype(bf16)
        mixed = jnp.dot(w_sp_ref[jb], rhs, preferred_element_type=f32)
        sg_blocks.append(jnp.concatenate(
            [mixed[:, c * LANES:(c + 1) * LANES] for c in range(n_chunks)], axis=0))
    sg = jnp.concatenate(sg_blocks, axis=1)
    sg = sg + jnp.concatenate([b_sp_ref[...]] * n_chunks, axis=0)
    p_bg = proj(SEG_BG)
    ya = (u * sg * _silu(p_za)).astype(bf16)
    p_zb = proj(SEG_ZB)
    acc_a = jnp.dot(ya, w_pa_ref[...], preferred_element_type=f32)

    conv = conv_b_ref[...] + conv_w_ref[CONV_WIDTH - 1:CONV_WIDTH, :] * cx_ref[HALO:HALO + TS, :]
    for k in range(CONV_WIDTH - 1):
        back = CONV_WIDTH - 1 - k
        conv = conv + conv_w_ref[k:k + 1, :] * cx_ref[HALO - back:HALO - back + TS, :]
    yb = (p_bg * conv * _silu(p_zb)).astype(bf16)
    cx_ref[0:HALO, :] = cx_ref[TS:TS + HALO, :]

    t1 = (j * TS + 1 + lax.broadcasted_iota(jnp.int32, (TS, C_GROUP), 0)).astype(f32)
    inv_t1 = 1.0 / t1
    pooled = []
    for gi, w in enumerate(POOL_WINDOWS):
        cols = slice(gi * C_GROUP, (gi + 1) * C_GROUP)
        ext = xc_ref[:, cols]
        win = ext
        span = 1
        while span < w:
            win = win + pltpu.roll(win, span, axis=0)
            span *= 2
        pooled.append(win[HALO:, :] * jnp.maximum(inv_t1, 1.0 / w) - ext[HALO:, :])
    p_zc = proj(SEG_ZC)
    yc_groups = []
    for half in range(2):
        lhs = jnp.concatenate(pooled[2 * half:2 * half + 2], axis=1).astype(bf16)
        yc_groups.append(jnp.dot(lhs, w_pool_ref[half], preferred_element_type=f32))
    xc_ref[0:HALO, :] = xc_ref[TS:TS + HALO, :]

    merged = _sigmoid(proj(SEG_GA)) * acc_a
    acc_b = jnp.dot(yb, w_pb_ref[...], preferred_element_type=f32)
    yc = jnp.concatenate(yc_groups, axis=1) * pool_scale_ref[...]
    yc = (yc * _silu(p_zc)).astype(bf16)
    merged = merged + _sigmoid(proj(SEG_GB)) * acc_b
    acc_c = jnp.dot(yc, w_pc_ref[...], preferred_element_type=f32)
    merged = merged + _sigmoid(proj(SEG_GC)) * acc_c

    y = x_ref[0] + jnp.dot(merged.astype(bf16), w_o_ref[...], preferred_element_type=f32)
    if apply_final_norm:
        ms_y = jnp.mean(y * y, axis=-1, keepdims=True)
        y = y * lax.rsqrt(ms_y + RMS_EPS) * final_g_ref[...]
    out_ref[0] = y


def _const_spec(shape):
    zeros = (0,) * len(shape)
    return pl.BlockSpec(shape, lambda b, j: zeros, pipeline_mode=pl.Buffered(1))


def _layer(x, norm_g, w_in, ln_g, ln_b, w_s, b_s, conv_w, conv_b, w_pool, pool_scale,
           w_pa, w_pb, w_pc, w_o, final_g, *, apply_final_norm):
    bsz, seq, _ = x.shape
    assert seq % TS == 0 and TS % CHUNK == 0 and TS >= HALO
    bf16 = jnp.bfloat16
    f32 = jnp.float32

    causal = jnp.tril(jnp.ones((CHUNK, CHUNK), dtype=bool))
    w_m = jnp.where(causal, w_s, 0.0)
    w_sp = w_m.reshape(A_GROUPS // 2, 2, CHUNK, CHUNK).transpose(0, 2, 1, 3)
    w_sp = w_sp.reshape(A_GROUPS // 2, CHUNK, 2 * CHUNK).astype(bf16)
    b_sp = jnp.repeat(b_s.T, A_HEAD, axis=1).astype(f32)
    zero = jnp.zeros((C_GROUP, C_GROUP), w_pool.dtype)
    w_pool_bd = jnp.stack([
        jnp.block([[w_pool[2 * i], zero], [zero, w_pool[2 * i + 1]]]) for i in range(2)
    ]).astype(bf16)

    row = lambda a: a.reshape(1, -1).astype(f32)
    operands = (
        x, row(norm_g), w_in.astype(bf16), row(ln_g), row(ln_b), w_sp, b_sp,
        conv_w.astype(f32), row(conv_b), w_pool_bd, row(pool_scale),
        w_pa.astype(bf16), w_pb.astype(bf16), w_pc.astype(bf16), w_o.astype(bf16),
        row(final_g),
    )
    tile_spec = pl.BlockSpec((1, TS, D_MODEL), lambda b, j: (b, j, 0))
    in_specs = [tile_spec] + [_const_spec(a.shape) for a in operands[1:]]

    return pl.pallas_call(
        functools.partial(_layer_kernel, apply_final_norm=apply_final_norm),
        out_shape=jax.ShapeDtypeStruct(x.shape, x.dtype),
        grid=(bsz, seq // TS),
        in_specs=in_specs,
        out_specs=tile_spec,
        scratch_shapes=[
            pltpu.VMEM((TS, D_MODEL), bf16),
            pltpu.VMEM((TS + HALO, B_WIDTH), f32),
            pltpu.VMEM((TS + HALO, C_WIDTH), f32),
        ],
        compiler_params=pltpu.CompilerParams(
            dimension_semantics=("arbitrary", "arbitrary"),
            vmem_limit_bytes=VMEM_LIMIT_BYTES,
        ),
        name="hybrid_layer_final" if apply_final_norm else "hybrid_layer",
    )(*operands)


def kernel(x, norm_g, w_in, ln_g, ln_b, w_s, b_s, conv_w, conv_b, w_pool, pool_scale,
           w_pa, w_pb, w_pc, w_o, final_g):
    for l in range(DEPTH):
        x = _layer(x, norm_g[l], w_in[l], ln_g[l], ln_b[l], w_s[l], b_s[l], conv_w[l],
                   conv_b[l], w_pool[l], pool_scale[l], w_pa[l], w_pb[l], w_pc[l], w_o[l],
                   final_g, apply_final_norm=(l == DEPTH - 1))
    return x
```

```python
import functools
import math

import jax
import jax.numpy as jnp
import numpy as np
from jax import lax
from jax.experimental import pallas as pl
from jax.experimental.pallas import tpu as pltpu

D_MODEL = 1024
DEPTH = 2
CHUNK = 128
A_GROUPS = 8
A_WIDTH = 512
A_HEAD = A_WIDTH // A_GROUPS
B_WIDTH = 512
CONV_WIDTH = 3
C_WIDTH = 512
POOL_WINDOWS = (2, 4, 8, 16)
C_GROUP = C_WIDTH // len(POOL_WINDOWS)
IN_TOTAL = 3 * A_WIDTH + 4 * B_WIDTH + 2 * C_WIDTH + 3 * D_MODEL
RMS_EPS = 1e-6
LN_EPS = 1e-5

_WIDTHS = [A_WIDTH] * 3 + [B_WIDTH] * 4 + [C_WIDTH] * 2 + [D_MODEL] * 3
_OFFS = [sum(_WIDTHS[:i]) for i in range(len(_WIDTHS) + 1)]
(SEG_U, SEG_V, SEG_ZA, SEG_XB, SEG_BG, SEG_CG, SEG_ZB, SEG_XC, SEG_ZC,
 SEG_GA, SEG_GB, SEG_GC) = [(_OFFS[i], _OFFS[i + 1]) for i in range(len(_WIDTHS))]

LANES = 128
HALO = 16
TS = 512
VMEM_LIMIT_BYTES = 56 * 1024 * 1024


HALVED_SEGS = (SEG_U, SEG_V, SEG_ZA, SEG_ZB, SEG_ZC, SEG_GA, SEG_GB, SEG_GC)
_GELU_C1 = 2.0 * math.sqrt(2.0 / math.pi)
_GELU_C3 = 8.0 * 0.044715 * math.sqrt(2.0 / math.pi)


def _gelu_of_half(xh):
    t = jnp.tanh(xh * (_GELU_C1 + _GELU_C3 * (xh * xh)))
    return xh + xh * t


def _silu_of_half(zh):
    return zh + zh * jnp.tanh(zh)


def _twice_sigmoid_of_half(gh):
    return jnp.tanh(gh) + 1.0


def _rmsnorm_rows(x, gain):
    ms = jnp.mean(x * x, axis=-1, keepdims=True)
    return x * lax.rsqrt(ms + RMS_EPS) * gain


def _layer_kernel(x_ref, x_next_ref, vec_ref, w_in_ref, w_sp_ref, b_sp_ref, w_pool_ref,
                  w_br_ref, w_o_ref, out_ref, h_ref, pu_ref, cx_ref, xc_ref, *,
                  steps_per_seq, apply_final_norm):
    f32 = jnp.float32
    bf16 = jnp.bfloat16
    j = pl.program_id(1)
    step = pl.program_id(0) * steps_per_seq + j

    norm_g = vec_ref[0:1, :]
    ln_g, ln_b = vec_ref[1:2, 0:A_WIDTH], vec_ref[1:2, A_WIDTH:]
    conv_b, pool_scale = vec_ref[2:3, 0:B_WIDTH], vec_ref[2:3, B_WIDTH:]
    conv_w = (vec_ref[3:4, 0:B_WIDTH], vec_ref[3:4, B_WIDTH:], vec_ref[4:5, 0:B_WIDTH])
    final_g = vec_ref[5:6, :]

    @pl.when(j == 0)
    def _():
        cx_ref[0:HALO, :] = jnp.zeros((HALO, B_WIDTH), f32)
        xc_ref[0:HALO, :] = jnp.zeros((HALO, C_WIDTH), f32)

    def proj(seg):
        return jnp.dot(h_ref[...], w_in_ref[:, seg[0]:seg[1]], preferred_element_type=f32)

    @pl.when(step == 0)
    def _():
        h_ref[...] = _rmsnorm_rows(x_ref[0], norm_g).astype(bf16)
        pu_ref[...] = proj(SEG_U)

    n_chunks = TS // CHUNK

    p_u = pu_ref[...]
    p_v = proj(SEG_V)
    xc_ref[HALO:HALO + TS, :] = proj(SEG_XC)

    u = _gelu_of_half(p_u)
    v = _gelu_of_half(p_v)
    mu = jnp.mean(v, axis=-1, keepdims=True)
    vc = v - mu
    var = jnp.mean(vc * vc, axis=-1, keepdims=True)
    vn = vc * lax.rsqrt(var + LN_EPS) * ln_g + ln_b
    p_za = proj(SEG_ZA)
    cx_ref[HALO:HALO + TS, :] = proj(SEG_CG) * proj(SEG_XB)

    lane = lax.broadcasted_iota(jnp.int32, (CHUNK, LANES), 1)
    first_group = lane < A_HEAD
    sg_blocks = []
    for jb in range(A_WIDTH // LANES):
        rhs = []
        for c in range(n_chunks):
            vb = vn[c * CHUNK:(c + 1) * CHUNK, jb * LANES:(jb + 1) * LANES]
            rhs.append(jnp.concatenate([jnp.where(first_group, vb, 0.0),
                                        jnp.where(first_group, 0.0, vb)], axis=0))
        rhs = jnp.concatenate(rhs, axis=1).astype(bf16)
        mixed = jnp.dot(w_sp_ref[jb], rhs, preferred_element_type=f32)
        sg_blocks.append(jnp.concatenate(
            [mixed[:, c * LANES:(c + 1) * LANES] for c in range(n_chunks)], axis=0))
    sg = jnp.concatenate(sg_blocks, axis=1)
    sg = sg + jnp.concatenate([b_sp_ref[...]] * n_chunks, axis=0)
    p_bg = proj(SEG_BG)
    ya = (u * sg * _silu_of_half(p_za)).astype(bf16)
    p_zb = proj(SEG_ZB)
    acc_a = jnp.dot(ya, w_br_ref[0], preferred_element_type=f32)

    conv = conv_b + conv_w[CONV_WIDTH - 1] * cx_ref[HALO:HALO + TS, :]
    for k in range(CONV_WIDTH - 1):
        back = CONV_WIDTH - 1 - k
        conv = conv + conv_w[k] * cx_ref[HALO - back:HALO - back + TS, :]
    yb = (p_bg * conv * _silu_of_half(p_zb)).astype(bf16)
    cx_ref[0:HALO, :] = cx_ref[TS:TS + HALO, :]

    t1 = (j * TS + 1 + lax.broadcasted_iota(jnp.int32, (TS, C_GROUP), 0)).astype(f32)
    inv_t1 = 1.0 / t1
    pooled = []
    for gi, w in enumerate(POOL_WINDOWS):
        cols = slice(gi * C_GROUP, (gi + 1) * C_GROUP)
        ext = xc_ref[:, cols]
        win = ext
        span = 1
        while span < w:
            win = win + pltpu.roll(win, span, axis=0)
            span *= 2
        pooled.append(win[HALO:, :] * jnp.maximum(inv_t1, 1.0 / w) - ext[HALO:, :])
    p_zc = proj(SEG_ZC)
    yc_groups = []
    for half in range(2):
        lhs = jnp.concatenate(pooled[2 * half:2 * half + 2], axis=1).astype(bf16)
        yc_groups.append(jnp.dot(lhs, w_pool_ref[half], preferred_element_type=f32))
    xc_ref[0:HALO, :] = xc_ref[TS:TS + HALO, :]

    merged = _twice_sigmoid_of_half(proj(SEG_GA)) * acc_a
    acc_b = jnp.dot(yb, w_br_ref[1], preferred_element_type=f32)
    yc = jnp.concatenate(yc_groups, axis=1) * pool_scale
    yc = (yc * _silu_of_half(p_zc)).astype(bf16)
    merged = merged + _twice_sigmoid_of_half(proj(SEG_GB)) * acc_b
    gate_c = _twice_sigmoid_of_half(proj(SEG_GC))
    acc_c = jnp.dot(yc, w_br_ref[2], preferred_element_type=f32)
    merged = merged + gate_c * acc_c

    h_ref[...] = _rmsnorm_rows(x_next_ref[0], norm_g).astype(bf16)
    y = x_ref[0] + jnp.dot(merged.astype(bf16), w_o_ref[...], preferred_element_type=f32)
    if apply_final_norm:
        y = _rmsnorm_rows(y, final_g)
    out_ref[0] = y
    pu_ref[...] = proj(SEG_U)


def _pack_vectors(norm_g, ln_g, ln_b, conv_w, conv_b, pool_scale, final_g):
    depth = norm_g.shape[0]
    pair = lambda a, b: jnp.concatenate([a, b], axis=-1)
    rows = [
        norm_g,
        pair(ln_g, ln_b),
        pair(conv_b, pool_scale),
        pair(conv_w[:, 0], conv_w[:, 1]),
        pair(conv_w[:, 2], jnp.zeros_like(conv_b)),
        jnp.broadcast_to(final_g, (depth, D_MODEL)),
        jnp.zeros_like(norm_g),
        jnp.zeros_like(norm_g),
    ]
    return jnp.stack(rows, axis=1).astype(jnp.float32)


def _prepare_weights(w_in, w_s, b_s, w_pool, w_pa, w_pb, w_pc, w_o):
    bf16 = jnp.bfloat16
    depth = w_in.shape[0]
    causal = jnp.tril(jnp.ones((CHUNK, CHUNK), dtype=bool))
    w_m = jnp.where(causal, w_s, 0.0)
    w_sp = w_m.reshape(depth, A_GROUPS // 2, 2, CHUNK, CHUNK).transpose(0, 1, 3, 2, 4)
    w_sp = w_sp.reshape(depth, A_GROUPS // 2, CHUNK, 2 * CHUNK).astype(bf16)
    b_sp = jnp.repeat(jnp.swapaxes(b_s, 1, 2), A_HEAD, axis=2).astype(jnp.float32)
    n_pairs = len(POOL_WINDOWS) // 2
    eye = jnp.eye(2, dtype=w_pool.dtype)
    w_pairs = w_pool.reshape(depth, n_pairs, 2, C_GROUP, C_GROUP)
    w_pool_bd = w_pairs[:, :, :, :, None, :] * eye[None, None, :, None, :, None]
    w_pool_bd = w_pool_bd.reshape(depth, n_pairs, 2 * C_GROUP, 2 * C_GROUP).astype(bf16)
    w_br = jnp.stack([w_pa, w_pb, w_pc], axis=1).astype(bf16)
    col_scale = np.ones((IN_TOTAL,), np.float32)
    for lo, hi in HALVED_SEGS:
        col_scale[lo:hi] = 0.5
    w_in_b = (w_in * col_scale).astype(bf16)
    w_o_b = (w_o * 0.5).astype(bf16)
    return w_in_b, w_sp, b_sp, w_pool_bd, w_br, w_o_b


def _layer_spec(array, layer):
    zeros = (0,) * (array.ndim - 1)
    return pl.BlockSpec((None,) + array.shape[1:], lambda b, j: (layer,) + zeros,
                        pipeline_mode=pl.Buffered(1))


def _layer(x, stacked, layer, *, apply_final_norm):
    bsz, seq, _ = x.shape
    assert seq % TS == 0 and TS % CHUNK == 0 and TS >= HALO
    steps_per_seq = seq // TS
    last_step = bsz * steps_per_seq - 1

    def next_tile(b, j):
        s = jnp.minimum(b * steps_per_seq + j + 1, last_step)
        return (s // steps_per_seq, s % steps_per_seq, 0)

    tile = (1, TS, D_MODEL)
    in_specs = [pl.BlockSpec(tile, lambda b, j: (b, j, 0)), pl.BlockSpec(tile, next_tile)]
    in_specs += [_layer_spec(a, layer) for a in stacked]

    return pl.pallas_call(
        functools.partial(_layer_kernel, steps_per_seq=steps_per_seq,
                          apply_final_norm=apply_final_norm),
        out_shape=jax.ShapeDtypeStruct(x.shape, x.dtype),
        grid=(bsz, steps_per_seq),
        in_specs=in_specs,
        out_specs=pl.BlockSpec(tile, lambda b, j: (b, j, 0)),
        scratch_shapes=[
            pltpu.VMEM((TS, D_MODEL), jnp.bfloat16),
            pltpu.VMEM((TS, A_WIDTH), jnp.float32),
            pltpu.VMEM((TS + HALO, B_WIDTH), jnp.float32),
            pltpu.VMEM((TS + HALO, C_WIDTH), jnp.float32),
        ],
        compiler_params=pltpu.CompilerParams(
            dimension_semantics=("arbitrary", "arbitrary"),
            vmem_limit_bytes=VMEM_LIMIT_BYTES,
        ),
        name="hybrid_layer_final" if apply_final_norm else "hybrid_layer",
    )(x, x, *stacked)


def kernel(x, norm_g, w_in, ln_g, ln_b, w_s, b_s, conv_w, conv_b, w_pool, pool_scale,
           w_pa, w_pb, w_pc, w_o, final_g):
    vecs = _pack_vectors(norm_g, ln_g, ln_b, conv_w, conv_b, pool_scale, final_g)
    w_in_b, w_sp, b_sp, w_pool_bd, w_br, w_o_b = _prepare_weights(
        w_in, w_s, b_s, w_pool, w_pa, w_pb, w_pc, w_o)
    stacked = (vecs, w_in_b, w_sp, b_sp, w_pool_bd, w_br, w_o_b)
    for layer in range(DEPTH):
        x = _layer(x, stacked, layer, apply_final_norm=(layer == DEPTH - 1))
    return x
```

```python
import functools
import math

import jax
import jax.numpy as jnp
from jax import lax
from jax.experimental import pallas as pl
from jax.experimental.pallas import tpu as pltpu

D_MODEL = 1024
DEPTH = 2
CHUNK = 128
A_GROUPS = 8
A_WIDTH = 512
A_HEAD = A_WIDTH // A_GROUPS
B_WIDTH = 512
CONV_WIDTH = 3
C_WIDTH = 512
POOL_WINDOWS = (2, 4, 8, 16)
C_GROUP = C_WIDTH // len(POOL_WINDOWS)
IN_TOTAL = 3 * A_WIDTH + 4 * B_WIDTH + 2 * C_WIDTH + 3 * D_MODEL
RMS_EPS = 1e-6
LN_EPS = 1e-5

_WIDTHS = [A_WIDTH] * 3 + [B_WIDTH] * 4 + [C_WIDTH] * 2 + [D_MODEL] * 3
_OFFS = [sum(_WIDTHS[:i]) for i in range(len(_WIDTHS) + 1)]
(SEG_U, SEG_V, SEG_ZA, SEG_XB, SEG_BG, SEG_CG, SEG_ZB, SEG_XC, SEG_ZC,
 SEG_GA, SEG_GB, SEG_GC) = [(_OFFS[i], _OFFS[i + 1]) for i in range(len(_WIDTHS))]
SEG_UV = (SEG_U[0], SEG_V[1])

LANES = 128
HALO = 16
TS = 512
W_SEG = 512
VMEM_LIMIT_BYTES = 56 * 1024 * 1024


HALVED_SEGS = (SEG_U, SEG_V, SEG_ZA, SEG_ZB, SEG_ZC, SEG_GA, SEG_GB, SEG_GC)
_GELU_C1 = 2.0 * math.sqrt(2.0 / math.pi)
_GELU_C3 = 8.0 * 0.044715 * math.sqrt(2.0 / math.pi)


def _gelu_of_half(xh):
    t = jnp.tanh(xh * (_GELU_C1 + _GELU_C3 * (xh * xh)))
    return xh + xh * t


def _silu_of_half(zh):
    return zh + zh * jnp.tanh(zh)


def _twice_sigmoid_of_half(gh):
    return jnp.tanh(gh) + 1.0


def _rmsnorm_rows(x, gain):
    ms = jnp.mean(x * x, axis=-1, keepdims=True)
    return x * lax.rsqrt(ms + RMS_EPS) * gain


def _w_in_copy(w_in_hbm_ref, stage_ref, sem_ref, layer, k):
    return pltpu.make_async_copy(
        w_in_hbm_ref.at[layer, :, pl.ds(k * W_SEG, W_SEG)],
        stage_ref.at[k % 2], sem_ref.at[k % 2])


def _load_w_in(w_in_hbm_ref, w_in_ref, stage_ref, sem_ref, layer):
    n_seg = IN_TOTAL // W_SEG
    halved = [any(lo <= k * W_SEG < hi for lo, hi in HALVED_SEGS) for k in range(n_seg)]
    _w_in_copy(w_in_hbm_ref, stage_ref, sem_ref, layer, 0).start()
    for k in range(n_seg):
        if k + 1 < n_seg:
            _w_in_copy(w_in_hbm_ref, stage_ref, sem_ref, layer, k + 1).start()
        _w_in_copy(w_in_hbm_ref, stage_ref, sem_ref, layer, k).wait()
        w = stage_ref[k % 2]
        if halved[k]:
            w = w * 0.5
        w_in_ref[:, k * W_SEG:(k + 1) * W_SEG] = w.astype(jnp.bfloat16)


def _layer_kernel(x_ref, x_next_ref, w_in_hbm_ref, vec_ref, w_sp_ref, b_sp_ref, w_pool_ref,
                  w_br_ref, w_o_ref, out_ref, w_in_ref, stage_ref, sem_ref, h_ref, puv_ref,
                  cx_ref, xc_ref, *, layer, steps_per_seq, apply_final_norm):
    f32 = jnp.float32
    bf16 = jnp.bfloat16
    j = pl.program_id(1)
    step = pl.program_id(0) * steps_per_seq + j

    norm_g = vec_ref[0:1, :]
    ln_g, ln_b = vec_ref[1:2, 0:A_WIDTH], vec_ref[1:2, A_WIDTH:]
    conv_b, pool_scale = vec_ref[2:3, 0:B_WIDTH], vec_ref[2:3, B_WIDTH:]
    conv_w = (vec_ref[3:4, 0:B_WIDTH], vec_ref[3:4, B_WIDTH:], vec_ref[4:5, 0:B_WIDTH])
    final_g = vec_ref[5:6, :]

    @pl.when(j == 0)
    def _():
        cx_ref[0:HALO, :] = jnp.zeros((HALO, B_WIDTH), f32)
        xc_ref[0:HALO, :] = jnp.zeros((HALO, C_WIDTH), f32)

    def proj(seg):
        return jnp.dot(h_ref[...], w_in_ref[:, seg[0]:seg[1]], preferred_element_type=f32)

    @pl.when(step == 0)
    def _():
        _load_w_in(w_in_hbm_ref, w_in_ref, stage_ref, sem_ref, layer)
        h_ref[...] = _rmsnorm_rows(x_ref[0], norm_g).astype(bf16)
        puv_ref[...] = proj(SEG_UV)

    n_chunks = TS // CHUNK

    p_u = puv_ref[:, 0:A_WIDTH]
    p_v = puv_ref[:, A_WIDTH:]
    xc_ref[HALO:HALO + TS, :] = proj(SEG_XC)

    u = _gelu_of_half(p_u)
    v = _gelu_of_half(p_v)
    mu = jnp.mean(v, axis=-1, keepdims=True)
    vc = v - mu
    var = jnp.mean(vc * vc, axis=-1, keepdims=True)
    vn = vc * lax.rsqrt(var + LN_EPS) * ln_g + ln_b
    p_za = proj(SEG_ZA)
    cx_ref[HALO:HALO + TS, :] = proj(SEG_CG) * proj(SEG_XB)

    lane = lax.broadcasted_iota(jnp.int32, (CHUNK, LANES), 1)
    first_group = lane < A_HEAD
    sg_blocks = []
    for jb in range(A_WIDTH // LANES):
        rhs = []
        for c in range(n_chunks):
            vb = vn[c * CHUNK:(c + 1) * CHUNK, jb * LANES:(jb + 1) * LANES]
            rhs.append(jnp.concatenate([jnp.where(first_group, vb, 0.0),
                                        jnp.where(first_group, 0.0, vb)], axis=0))
        rhs = jnp.concatenate(rhs, axis=1).astype(bf16)
        mixed = jnp.dot(w_sp_ref[jb], rhs, preferred_element_type=f32)
        sg_blocks.append(jnp.concatenate(
            [mixed[:, c * LANES:(c + 1) * LANES] for c in range(n_chunks)], axis=0))
    sg = jnp.concatenate(sg_blocks, axis=1)
    sg = sg + jnp.concatenate([b_sp_ref[...]] * n_chunks, axis=0)
    p_bg = proj(SEG_BG)
    ya = (u * sg * _silu_of_half(p_za)).astype(bf16)
    p_zb = proj(SEG_ZB)
    acc_a = jnp.dot(ya, w_br_ref[0], preferred_element_type=f32)

    conv = conv_b + conv_w[CONV_WIDTH - 1] * cx_ref[HALO:HALO + TS, :]
    for k in range(CONV_WIDTH - 1):
        back = CONV_WIDTH - 1 - k
        conv = conv + conv_w[k] * cx_ref[HALO - back:HALO - back + TS, :]
    yb = (p_bg * conv * _silu_of_half(p_zb)).astype(bf16)
    cx_ref[0:HALO, :] = cx_ref[TS:TS + HALO, :]

    t1 = (j * TS + 1 + lax.broadcasted_iota(jnp.int32, (TS, C_GROUP), 0)).astype(f32)
    inv_t1 = 1.0 / t1
    pooled = []
    for gi, w in enumerate(POOL_WINDOWS):
        cols = slice(gi * C_GROUP, (gi + 1) * C_GROUP)
        ext = xc_ref[:, cols]
        win = ext
        span = 1
        while span < w:
            win = win + pltpu.roll(win, span, axis=0)
            span *= 2
        pooled.append(win[HALO:, :] * jnp.maximum(inv_t1, 1.0 / w) - ext[HALO:, :])
    p_zc = proj(SEG_ZC)
    yc_groups = []
    for half in range(2):
        lhs = jnp.concatenate(pooled[2 * half:2 * half + 2], axis=1).astype(bf16)
        yc_groups.append(jnp.dot(lhs, w_pool_ref[half], preferred_element_type=f32))
    xc_ref[0:HALO, :] = xc_ref[TS:TS + HALO, :]

    merged = _twice_sigmoid_of_half(proj(SEG_GA)) * acc_a
    acc_b = jnp.dot(yb, w_br_ref[1], preferred_element_type=f32)
    yc = jnp.concatenate(yc_groups, axis=1) * pool_scale
    yc = (yc * _silu_of_half(p_zc)).astype(bf16)
    merged = merged + _twice_sigmoid_of_half(proj(SEG_GB)) * acc_b
    gate_c = _twice_sigmoid_of_half(proj(SEG_GC))
    acc_c = jnp.dot(yc, w_br_ref[2], preferred_element_type=f32)

    h_ref[...] = _rmsnorm_rows(x_next_ref[0], norm_g).astype(bf16)
    puv_ref[:, 0:A_WIDTH] = proj(SEG_U)
    merged = merged + gate_c * acc_c

    y = x_ref[0] + jnp.dot(merged.astype(bf16), w_o_ref[...], preferred_element_type=f32)
    if apply_final_norm:
        y = _rmsnorm_rows(y, final_g)
    out_ref[0] = y
    puv_ref[:, A_WIDTH:] = proj(SEG_V)


def _pack_vectors(norm_g, ln_g, ln_b, conv_w, conv_b, pool_scale, final_g):
    depth = norm_g.shape[0]
    pair = lambda a, b: jnp.concatenate([a, b], axis=-1)
    rows = [
        norm_g,
        pair(ln_g, ln_b),
        pair(conv_b, pool_scale),
        pair(conv_w[:, 0], conv_w[:, 1]),
        pair(conv_w[:, 2], jnp.zeros_like(conv_b)),
        jnp.broadcast_to(final_g, (depth, D_MODEL)),
        jnp.zeros_like(norm_g),
        jnp.zeros_like(norm_g),
    ]
    return jnp.stack(rows, axis=1).astype(jnp.float32)


def _prepare_weights(w_s, b_s, w_pool, w_pa, w_pb, w_pc, w_o):
    bf16 = jnp.bfloat16
    depth = w_s.shape[0]
    causal = jnp.tril(jnp.ones((CHUNK, CHUNK), dtype=bool))
    w_m = jnp.where(causal, w_s, 0.0)
    w_sp = w_m.reshape(depth, A_GROUPS // 2, 2, CHUNK, CHUNK).transpose(0, 1, 3, 2, 4)
    w_sp = w_sp.reshape(depth, A_GROUPS // 2, CHUNK, 2 * CHUNK).astype(bf16)
    b_sp = jnp.repeat(jnp.swapaxes(b_s, 1, 2), A_HEAD, axis=2).astype(jnp.float32)
    n_pairs = len(POOL_WINDOWS) // 2
    eye = jnp.eye(2, dtype=w_pool.dtype)
    w_pairs = w_pool.reshape(depth, n_pairs, 2, C_GROUP, C_GROUP)
    w_pool_bd = w_pairs[:, :, :, :, None, :] * eye[None, None, :, None, :, None]
    w_pool_bd = w_pool_bd.reshape(depth, n_pairs, 2 * C_GROUP, 2 * C_GROUP).astype(bf16)
    w_br = jnp.stack([w_pa, w_pb, w_pc], axis=1).astype(bf16)
    w_o_b = (w_o * 0.5).astype(bf16)
    return w_sp, b_sp, w_pool_bd, w_br, w_o_b


def _layer_spec(array, layer):
    zeros = (0,) * (array.ndim - 1)
    return pl.BlockSpec((None,) + array.shape[1:], lambda b, j: (layer,) + zeros,
                        pipeline_mode=pl.Buffered(1))


def _layer(x, w_in, stacked, layer, *, apply_final_norm):
    bsz, seq, _ = x.shape
    assert seq % TS == 0 and TS % CHUNK == 0 and TS >= HALO
    assert w_in.dtype == jnp.float32 and w_in.shape[1:] == (D_MODEL, IN_TOTAL)
    assert all(lo % W_SEG == 0 and hi % W_SEG == 0 for lo, hi in HALVED_SEGS)
    steps_per_seq = seq // TS
    last_step = bsz * steps_per_seq - 1

    def next_tile(b, j):
        s = jnp.minimum(b * steps_per_seq + j + 1, last_step)
        return (s // steps_per_seq, s % steps_per_seq, 0)

    tile = (1, TS, D_MODEL)
    in_specs = [pl.BlockSpec(tile, lambda b, j: (b, j, 0)), pl.BlockSpec(tile, next_tile),
                pl.BlockSpec(memory_space=pl.ANY)]
    in_specs += [_layer_spec(a, layer) for a in stacked]

    return pl.pallas_call(
        functools.partial(_layer_kernel, layer=layer, steps_per_seq=steps_per_seq,
                          apply_final_norm=apply_final_norm),
        out_shape=jax.ShapeDtypeStruct(x.shape, x.dtype),
        grid=(bsz, steps_per_seq),
        in_specs=in_specs,
        out_specs=pl.BlockSpec(tile, lambda b, j: (b, j, 0)),
        scratch_shapes=[
            pltpu.VMEM((D_MODEL, IN_TOTAL), jnp.bfloat16),
            pltpu.VMEM((2, D_MODEL, W_SEG), jnp.float32),
            pltpu.SemaphoreType.DMA((2,)),
            pltpu.VMEM((TS, D_MODEL), jnp.bfloat16),
            pltpu.VMEM((TS, 2 * A_WIDTH), jnp.float32),
            pltpu.VMEM((TS + HALO, B_WIDTH), jnp.float32),
            pltpu.VMEM((TS + HALO, C_WIDTH), jnp.float32),
        ],
        compiler_params=pltpu.CompilerParams(
            dimension_semantics=("arbitrary", "arbitrary"),
            vmem_limit_bytes=VMEM_LIMIT_BYTES,
        ),
        name="hybrid_layer_final" if apply_final_norm else "hybrid_layer",
    )(x, x, w_in, *stacked)


def kernel(x, norm_g, w_in, ln_g, ln_b, w_s, b_s, conv_w, conv_b, w_pool, pool_scale,
           w_pa, w_pb, w_pc, w_o, final_g):
    vecs = _pack_vectors(norm_g, ln_g, ln_b, conv_w, conv_b, pool_scale, final_g)
    stacked = (vecs,) + _prepare_weights(w_s, b_s, w_pool, w_pa, w_pb, w_pc, w_o)
    for layer in range(DEPTH):
        x = _layer(x, w_in, stacked, layer, apply_final_norm=(layer == DEPTH - 1))
    return x
```

```python
import functools
import math

import jax
import jax.numpy as jnp
from jax import lax
from jax.experimental import pallas as pl
from jax.experimental.pallas import tpu as pltpu

D_MODEL = 1024
DEPTH = 2
CHUNK = 128
A_GROUPS = 8
A_WIDTH = 512
A_HEAD = A_WIDTH // A_GROUPS
B_WIDTH = 512
CONV_WIDTH = 3
C_WIDTH = 512
POOL_WINDOWS = (2, 4, 8, 16)
C_GROUP = C_WIDTH // len(POOL_WINDOWS)
IN_TOTAL = 3 * A_WIDTH + 4 * B_WIDTH + 2 * C_WIDTH + 3 * D_MODEL
RMS_EPS = 1e-6
LN_EPS = 1e-5

_WIDTHS = [A_WIDTH] * 3 + [B_WIDTH] * 4 + [C_WIDTH] * 2 + [D_MODEL] * 3
_OFFS = [sum(_WIDTHS[:i]) for i in range(len(_WIDTHS) + 1)]
(SEG_U, SEG_V, SEG_ZA, SEG_XB, SEG_BG, SEG_CG, SEG_ZB, SEG_XC, SEG_ZC,
 SEG_GA, SEG_GB, SEG_GC) = [(_OFFS[i], _OFFS[i + 1]) for i in range(len(_WIDTHS))]
SEG_UV = (SEG_U[0], SEG_V[1])
PROJ_ORDER = (SEG_UV, SEG_XC, SEG_ZA, SEG_CG, SEG_XB, SEG_BG, SEG_ZB, SEG_ZC, SEG_GA, SEG_GB,
              SEG_GC)

LANES = 128
HALO = 16
TS = 512
W_SEG = 512
N_STAGE = 3
VMEM_LIMIT_BYTES = 56 * 1024 * 1024


HALVED_SEGS = (SEG_U, SEG_V, SEG_ZA, SEG_ZB, SEG_ZC, SEG_GA, SEG_GB, SEG_GC)
_GELU_C1 = 2.0 * math.sqrt(2.0 / math.pi)
_GELU_C3 = 8.0 * 0.044715 * math.sqrt(2.0 / math.pi)


def _gelu_of_half(xh):
    t = jnp.tanh(xh * (_GELU_C1 + _GELU_C3 * (xh * xh)))
    return xh + xh * t


def _silu_of_half(zh):
    return zh + zh * jnp.tanh(zh)


def _twice_sigmoid_of_half(gh):
    return jnp.tanh(gh) + 1.0


def _rmsnorm_rows(x, gain):
    ms = jnp.mean(x * x, axis=-1, keepdims=True)
    return x * lax.rsqrt(ms + RMS_EPS) * gain


class _WInLoader:
    def __init__(self, w_in_hbm_ref, w_in_ref, stage_ref, sem_ref, layer):
        self.refs = (w_in_hbm_ref, w_in_ref, stage_ref, sem_ref)
        self.layer = layer
        self.blocks = [k for lo, hi in PROJ_ORDER for k in range(lo // W_SEG, hi // W_SEG)]
        self.n_done = 0

    def _copy(self, i):
        w_in_hbm_ref, _, stage_ref, sem_ref = self.refs
        cols = pl.ds(self.blocks[i] * W_SEG, W_SEG)
        return pltpu.make_async_copy(w_in_hbm_ref.at[self.layer, :, cols],
                                     stage_ref.at[i % N_STAGE], sem_ref.at[i % N_STAGE])

    def start(self):
        for i in range(N_STAGE):
            self._copy(i).start()

    def need(self, seg):
        _, w_in_ref, stage_ref, _ = self.refs
        for k in range(seg[0] // W_SEG, seg[1] // W_SEG):
            if k in self.blocks[:self.n_done]:
                continue
            i = self.n_done
            assert self.blocks[i] == k, "projections must follow PROJ_ORDER"
            self._copy(i).wait()
            w = stage_ref[i % N_STAGE]
            if any(lo <= k * W_SEG < hi for lo, hi in HALVED_SEGS):
                w = w * 0.5
            w_in_ref[:, k * W_SEG:(k + 1) * W_SEG] = w.astype(jnp.bfloat16)
            if i + N_STAGE < len(self.blocks):
                self._copy(i + N_STAGE).start()
            self.n_done += 1


def _layer_kernel(x_ref, x_next_ref, w_in_hbm_ref, vec_ref, w_sp_ref, b_sp_ref, w_pool_ref,
                  w_br_ref, w_o_ref, out_ref, w_in_ref, stage_ref, sem_ref, h_ref, puv_ref,
                  cx_ref, xc_ref, *, layer, steps_per_seq, apply_final_norm):
    j = pl.program_id(1)
    step = pl.program_id(0) * steps_per_seq + j

    @pl.when(j == 0)
    def _():
        cx_ref[0:HALO, :] = jnp.zeros((HALO, B_WIDTH), jnp.float32)
        xc_ref[0:HALO, :] = jnp.zeros((HALO, C_WIDTH), jnp.float32)

    def body(loader):
        _step_body(x_ref, x_next_ref, vec_ref, w_in_ref, w_sp_ref, b_sp_ref, w_pool_ref,
                   w_br_ref, w_o_ref, out_ref, h_ref, puv_ref, cx_ref, xc_ref,
                   seq_tile=j, loader=loader, apply_final_norm=apply_final_norm)

    @pl.when(step == 0)
    def _():
        body(_WInLoader(w_in_hbm_ref, w_in_ref, stage_ref, sem_ref, layer))

    @pl.when(step != 0)
    def _():
        body(None)


def _step_body(x_ref, x_next_ref, vec_ref, w_in_ref, w_sp_ref, b_sp_ref, w_pool_ref, w_br_ref,
               w_o_ref, out_ref, h_ref, puv_ref, cx_ref, xc_ref, *, seq_tile, loader,
               apply_final_norm):
    f32 = jnp.float32
    bf16 = jnp.bfloat16

    norm_g = vec_ref[0:1, :]
    ln_g, ln_b = vec_ref[1:2, 0:A_WIDTH], vec_ref[1:2, A_WIDTH:]
    conv_b, pool_scale = vec_ref[2:3, 0:B_WIDTH], vec_ref[2:3, B_WIDTH:]
    conv_w = (vec_ref[3:4, 0:B_WIDTH], vec_ref[3:4, B_WIDTH:], vec_ref[4:5, 0:B_WIDTH])
    final_g = vec_ref[5:6, :]

    def proj(seg):
        if loader is not None:
            loader.need(seg)
        return jnp.dot(h_ref[...], w_in_ref[:, seg[0]:seg[1]], preferred_element_type=f32)

    if loader is not None:
        loader.start()
        h_ref[...] = _rmsnorm_rows(x_ref[0], norm_g).astype(bf16)
        puv_ref[...] = proj(SEG_UV)

    n_chunks = TS // CHUNK

    p_u = puv_ref[:, 0:A_WIDTH]
    p_v = puv_ref[:, A_WIDTH:]
    xc_ref[HALO:HALO + TS, :] = proj(SEG_XC)

    u = _gelu_of_half(p_u)
    v = _gelu_of_half(p_v)
    mu = jnp.mean(v, axis=-1, keepdims=True)
    vc = v - mu
    var = jnp.mean(vc * vc, axis=-1, keepdims=True)
    vn = vc * lax.rsqrt(var + LN_EPS) * ln_g + ln_b
    p_za = proj(SEG_ZA)
    cx_ref[HALO:HALO + TS, :] = proj(SEG_CG) * proj(SEG_XB)

    lane = lax.broadcasted_iota(jnp.int32, (CHUNK, LANES), 1)
    first_group = lane < A_HEAD
    sg_blocks = []
    for jb in range(A_WIDTH // LANES):
        rhs = []
        for c in range(n_chunks):
            vb = vn[c * CHUNK:(c + 1) * CHUNK, jb * LANES:(jb + 1) * LANES]
            rhs.append(jnp.concatenate([jnp.where(first_group, vb, 0.0),
                                        jnp.where(first_group, 0.0, vb)], axis=0))
        rhs = jnp.concatenate(rhs, axis=1).astype(bf16)
        mixed = jnp.dot(w_sp_ref[jb], rhs, preferred_element_type=f32)
        sg_blocks.append(jnp.concatenate(
            [mixed[:, c * LANES:(c + 1) * LANES] for c in range(n_chunks)], axis=0))
    sg = jnp.concatenate(sg_blocks, axis=1)
    sg = sg + jnp.concatenate([b_sp_ref[...]] * n_chunks, axis=0)
    p_bg = proj(SEG_BG)
    ya = (u * sg * _silu_of_half(p_za)).astype(bf16)
    p_zb = proj(SEG_ZB)
    acc_a = jnp.dot(ya, w_br_ref[0], preferred_element_type=f32)

    conv = conv_b + conv_w[CONV_WIDTH - 1] * cx_ref[HALO:HALO + TS, :]
    for k in range(CONV_WIDTH - 1):
        back = CONV_WIDTH - 1 - k
        conv = conv + conv_w[k] * cx_ref[HALO - back:HALO - back + TS, :]
    yb = (p_bg * conv * _silu_of_half(p_zb)).astype(bf16)
    cx_ref[0:HALO, :] = cx_ref[TS:TS + HALO, :]

    t1 = (seq_tile * TS + 1 + lax.broadcasted_iota(jnp.int32, (TS, C_GROUP), 0)).astype(f32)
    inv_t1 = 1.0 / t1
    pooled = []
    for gi, w in enumerate(POOL_WINDOWS):
        cols = slice(gi * C_GROUP, (gi + 1) * C_GROUP)
        ext = xc_ref[:, cols]
        win = ext
        span = 1
        while span < w:
            win = win + pltpu.roll(win, span, axis=0)
            span *= 2
        pooled.append(win[HALO:, :] * jnp.maximum(inv_t1, 1.0 / w) - ext[HALO:, :])
    p_zc = proj(SEG_ZC)
    yc_groups = []
    for half in range(2):
        lhs = jnp.concatenate(pooled[2 * half:2 * half + 2], axis=1).astype(bf16)
        yc_groups.append(jnp.dot(lhs, w_pool_ref[half], preferred_element_type=f32))
    xc_ref[0:HALO, :] = xc_ref[TS:TS + HALO, :]

    merged = _twice_sigmoid_of_half(proj(SEG_GA)) * acc_a
    acc_b = jnp.dot(yb, w_br_ref[1], preferred_element_type=f32)
    yc = jnp.concatenate(yc_groups, axis=1) * pool_scale
    yc = (yc * _silu_of_half(p_zc)).astype(bf16)
    merged = merged + _twice_sigmoid_of_half(proj(SEG_GB)) * acc_b
    gate_c = _twice_sigmoid_of_half(proj(SEG_GC))
    acc_c = jnp.dot(yc, w_br_ref[2], preferred_element_type=f32)

    h_ref[...] = _rmsnorm_rows(x_next_ref[0], norm_g).astype(bf16)
    puv_ref[:, 0:A_WIDTH] = proj(SEG_U)
    merged = merged + gate_c * acc_c

    y = x_ref[0] + jnp.dot(merged.astype(bf16), w_o_ref[...], preferred_element_type=f32)
    if apply_final_norm:
        y = _rmsnorm_rows(y, final_g)
    out_ref[0] = y
    puv_ref[:, A_WIDTH:] = proj(SEG_V)


def _pack_vectors(norm_g, ln_g, ln_b, conv_w, conv_b, pool_scale, final_g):
    depth = norm_g.shape[0]
    pair = lambda a, b: jnp.concatenate([a, b], axis=-1)
    rows = [
        norm_g,
        pair(ln_g, ln_b),
        pair(conv_b, pool_scale),
        pair(conv_w[:, 0], conv_w[:, 1]),
        pair(conv_w[:, 2], jnp.zeros_like(conv_b)),
        jnp.broadcast_to(final_g, (depth, D_MODEL)),
        jnp.zeros_like(norm_g),
        jnp.zeros_like(norm_g),
    ]
    return jnp.stack(rows, axis=1).astype(jnp.float32)


def _prepare_weights(w_s, b_s, w_pool, w_pa, w_pb, w_pc, w_o):
    bf16 = jnp.bfloat16
    depth = w_s.shape[0]
    causal = jnp.tril(jnp.ones((CHUNK, CHUNK), dtype=bool))
    w_m = jnp.where(causal, w_s, 0.0)
    w_sp = w_m.reshape(depth, A_GROUPS // 2, 2, CHUNK, CHUNK).transpose(0, 1, 3, 2, 4)
    w_sp = w_sp.reshape(depth, A_GROUPS // 2, CHUNK, 2 * CHUNK).astype(bf16)
    b_sp = jnp.repeat(jnp.swapaxes(b_s, 1, 2), A_HEAD, axis=2).astype(jnp.float32)
    n_pairs = len(POOL_WINDOWS) // 2
    eye = jnp.eye(2, dtype=w_pool.dtype)
    w_pairs = w_pool.reshape(depth, n_pairs, 2, C_GROUP, C_GROUP)
    w_pool_bd = w_pairs[:, :, :, :, None, :] * eye[None, None, :, None, :, None]
    w_pool_bd = w_pool_bd.reshape(depth, n_pairs, 2 * C_GROUP, 2 * C_GROUP).astype(bf16)
    w_br = jnp.stack([w_pa, w_pb, w_pc], axis=1).astype(bf16)
    w_o_b = (w_o * 0.5).astype(bf16)
    return w_sp, b_sp, w_pool_bd, w_br, w_o_b


def _layer_spec(array, layer):
    zeros = (0,) * (array.ndim - 1)
    return pl.BlockSpec((None,) + array.shape[1:], lambda b, j: (layer,) + zeros,
                        pipeline_mode=pl.Buffered(1))


def _layer(x, w_in, stacked, layer, *, apply_final_norm):
    bsz, seq, _ = x.shape
    assert seq % TS == 0 and TS % CHUNK == 0 and TS >= HALO
    assert w_in.dtype == jnp.float32 and w_in.shape[1:] == (D_MODEL, IN_TOTAL)
    assert all(lo % W_SEG == 0 and hi % W_SEG == 0 for lo, hi in HALVED_SEGS)
    steps_per_seq = seq // TS
    last_step = bsz * steps_per_seq - 1

    def next_tile(b, j):
        s = jnp.minimum(b * steps_per_seq + j + 1, last_step)
        return (s // steps_per_seq, s % steps_per_seq, 0)

    tile = (1, TS, D_MODEL)
    in_specs = [pl.BlockSpec(tile, lambda b, j: (b, j, 0)), pl.BlockSpec(tile, next_tile),
                pl.BlockSpec(memory_space=pl.ANY)]
    in_specs += [_layer_spec(a, layer) for a in stacked]

    return pl.pallas_call(
        functools.partial(_layer_kernel, layer=layer, steps_per_seq=steps_per_seq,
                          apply_final_norm=apply_final_norm),
        out_shape=jax.ShapeDtypeStruct(x.shape, x.dtype),
        grid=(bsz, steps_per_seq),
        in_specs=in_specs,
        out_specs=pl.BlockSpec(tile, lambda b, j: (b, j, 0)),
        scratch_shapes=[
            pltpu.VMEM((D_MODEL, IN_TOTAL), jnp.bfloat16),
            pltpu.VMEM((N_STAGE, D_MODEL, W_SEG), jnp.float32),
            pltpu.SemaphoreType.DMA((N_STAGE,)),
            pltpu.VMEM((TS, D_MODEL), jnp.bfloat16),
            pltpu.VMEM((TS, 2 * A_WIDTH), jnp.float32),
            pltpu.VMEM((TS + HALO, B_WIDTH), jnp.float32),
            pltpu.VMEM((TS + HALO, C_WIDTH), jnp.float32),
        ],
        compiler_params=pltpu.CompilerParams(
            dimension_semantics=("arbitrary", "arbitrary"),
            vmem_limit_bytes=VMEM_LIMIT_BYTES,
        ),
        name="hybrid_layer_final" if apply_final_norm else "hybrid_layer",
    )(x, x, w_in, *stacked)


def kernel(x, norm_g, w_in, ln_g, ln_b, w_s, b_s, conv_w, conv_b, w_pool, pool_scale,
           w_pa, w_pb, w_pc, w_o, final_g):
    vecs = _pack_vectors(norm_g, ln_g, ln_b, conv_w, conv_b, pool_scale, final_g)
    stacked = (vecs,) + _prepare_weights(w_s, b_s, w_pool, w_pa, w_pb, w_pc, w_o)
    for layer in range(DEPTH):
        x = _layer(x, w_in, stacked, layer, apply_final_norm=(layer == DEPTH - 1))
    return x
```

```python
import functools
import math

import jax
import jax.numpy as jnp
from jax import lax
from jax.experimental import pallas as pl
from jax.experimental.pallas import tpu as pltpu

D_MODEL = 1024
DEPTH = 2
CHUNK = 128
A_GROUPS = 8
A_WIDTH = 512
A_HEAD = A_WIDTH // A_GROUPS
B_WIDTH = 512
CONV_WIDTH = 3
C_WIDTH = 512
POOL_WINDOWS = (2, 4, 8, 16)
C_GROUP = C_WIDTH // len(POOL_WINDOWS)
IN_TOTAL = 3 * A_WIDTH + 4 * B_WIDTH + 2 * C_WIDTH + 3 * D_MODEL
RMS_EPS = 1e-6
LN_EPS = 1e-5

_WIDTHS = [A_WIDTH] * 3 + [B_WIDTH] * 4 + [C_WIDTH] * 2 + [D_MODEL] * 3
_OFFS = [sum(_WIDTHS[:i]) for i in range(len(_WIDTHS) + 1)]
PROJ_SEGS = tuple((_OFFS[i], _OFFS[i + 1]) for i in range(len(_WIDTHS)))
(SEG_U, SEG_V, SEG_ZA, SEG_XB, SEG_BG, SEG_CG, SEG_ZB, SEG_XC, SEG_ZC,
 SEG_GA, SEG_GB, SEG_GC) = PROJ_SEGS
SEG_UV = (SEG_U[0], SEG_V[1])

LANES = 128
HALO = 16
TS = 512
W_ROWS = 64
N_STAGE = 3
VMEM_LIMIT_BYTES = 56 * 1024 * 1024


HALVED_SEGS = (SEG_U, SEG_V, SEG_ZA, SEG_ZB, SEG_ZC, SEG_GA, SEG_GB, SEG_GC)
_GELU_C1 = 2.0 * math.sqrt(2.0 / math.pi)
_GELU_C3 = 8.0 * 0.044715 * math.sqrt(2.0 / math.pi)


def _gelu_of_half(xh):
    t = jnp.tanh(xh * (_GELU_C1 + _GELU_C3 * (xh * xh)))
    return xh + xh * t


def _silu_of_half(zh):
    return zh + zh * jnp.tanh(zh)


def _twice_sigmoid_of_half(gh):
    return jnp.tanh(gh) + 1.0


def _rmsnorm_rows(x, gain):
    ms = jnp.mean(x * x, axis=-1, keepdims=True)
    return x * lax.rsqrt(ms + RMS_EPS) * gain


def _w_in_copy(w_in_hbm_ref, stage_ref, sem_ref, layer, i):
    return pltpu.make_async_copy(
        w_in_hbm_ref.at[layer, pl.ds(i * W_ROWS, W_ROWS), :],
        stage_ref.at[i % N_STAGE], sem_ref.at[i % N_STAGE])


def _load_w_in(w_in_hbm_ref, w_in_ref, stage_ref, sem_ref, layer):
    n_blocks = D_MODEL // W_ROWS
    for i in range(N_STAGE):
        _w_in_copy(w_in_hbm_ref, stage_ref, sem_ref, layer, i).start()

    def convert_block(i, carry):
        _w_in_copy(w_in_hbm_ref, stage_ref, sem_ref, layer, i).wait()
        rows = pl.ds(pl.multiple_of(i * W_ROWS, W_ROWS), W_ROWS)
        for lo, hi in PROJ_SEGS:
            w = stage_ref[i % N_STAGE, :, lo:hi]
            if (lo, hi) in HALVED_SEGS:
                w = w * 0.5
            w_in_ref[rows, lo:hi] = w.astype(jnp.bfloat16)

        @pl.when(i + N_STAGE < n_blocks)
        def _():
            _w_in_copy(w_in_hbm_ref, stage_ref, sem_ref, layer, i + N_STAGE).start()

        return carry

    lax.fori_loop(0, n_blocks, convert_block, 0)


def _layer_kernel(x_ref, x_next_ref, w_in_hbm_ref, vec_ref, w_sp_ref, b_sp_ref, w_pool_ref,
                  w_br_ref, w_o_ref, out_ref, w_in_ref, stage_ref, sem_ref, h_ref, puv_ref,
                  cx_ref, xc_ref, *, layer, steps_per_seq, apply_final_norm):
    f32 = jnp.float32
    bf16 = jnp.bfloat16
    j = pl.program_id(1)
    step = pl.program_id(0) * steps_per_seq + j

    norm_g = vec_ref[0:1, :]
    ln_g, ln_b = vec_ref[1:2, 0:A_WIDTH], vec_ref[1:2, A_WIDTH:]
    conv_b, pool_scale = vec_ref[2:3, 0:B_WIDTH], vec_ref[2:3, B_WIDTH:]
    conv_w = (vec_ref[3:4, 0:B_WIDTH], vec_ref[3:4, B_WIDTH:], vec_ref[4:5, 0:B_WIDTH])
    final_g = vec_ref[5:6, :]

    @pl.when(j == 0)
    def _():
        cx_ref[0:HALO, :] = jnp.zeros((HALO, B_WIDTH), f32)
        xc_ref[0:HALO, :] = jnp.zeros((HALO, C_WIDTH), f32)

    def proj(seg):
        return jnp.dot(h_ref[...], w_in_ref[:, seg[0]:seg[1]], preferred_element_type=f32)

    @pl.when(step == 0)
    def _():
        _load_w_in(w_in_hbm_ref, w_in_ref, stage_ref, sem_ref, layer)
        h_ref[...] = _rmsnorm_rows(x_ref[0], norm_g).astype(bf16)
        puv_ref[...] = proj(SEG_UV)

    n_chunks = TS // CHUNK

    p_u = puv_ref[:, 0:A_WIDTH]
    p_v = puv_ref[:, A_WIDTH:]
    xc_ref[HALO:HALO + TS, :] = proj(SEG_XC)

    u = _gelu_of_half(p_u)
    v = _gelu_of_half(p_v)
    mu = jnp.mean(v, axis=-1, keepdims=True)
    vc = v - mu
    var = jnp.mean(vc * vc, axis=-1, keepdims=True)
    vn = vc * lax.rsqrt(var + LN_EPS) * ln_g + ln_b
    p_za = proj(SEG_ZA)
    cx_ref[HALO:HALO + TS, :] = proj(SEG_CG) * proj(SEG_XB)

    lane = lax.broadcasted_iota(jnp.int32, (CHUNK, LANES), 1)
    first_group = lane < A_HEAD
    sg_blocks = []
    for jb in range(A_WIDTH // LANES):
        rhs = []
        for c in range(n_chunks):
            vb = vn[c * CHUNK:(c + 1) * CHUNK, jb * LANES:(jb + 1) * LANES]
            rhs.append(jnp.concatenate([jnp.where(first_group, vb, 0.0),
                                        jnp.where(first_group, 0.0, vb)], axis=0))
        rhs = jnp.concatenate(rhs, axis=1).astype(bf16)
        mixed = jnp.dot(w_sp_ref[jb], rhs, preferred_element_type=f32)
        sg_blocks.append(jnp.concatenate(
            [mixed[:, c * LANES:(c + 1) * LANES] for c in range(n_chunks)], axis=0))
    sg = jnp.concatenate(sg_blocks, axis=1)
    sg = sg + jnp.concatenate([b_sp_ref[...]] * n_chunks, axis=0)
    p_bg = proj(SEG_BG)
    ya = (u * sg * _silu_of_half(p_za)).astype(bf16)
    p_zb = proj(SEG_ZB)
    acc_a = jnp.dot(ya, w_br_ref[0], preferred_element_type=f32)

    conv = conv_b + conv_w[CONV_WIDTH - 1] * cx_ref[HALO:HALO + TS, :]
    for k in range(CONV_WIDTH - 1):
        back = CONV_WIDTH - 1 - k
        conv = conv + conv_w[k] * cx_ref[HALO - back:HALO - back + TS, :]
    yb = (p_bg * conv * _silu_of_half(p_zb)).astype(bf16)
    cx_ref[0:HALO, :] = cx_ref[TS:TS + HALO, :]

    t1 = (j * TS + 1 + lax.broadcasted_iota(jnp.int32, (TS, C_GROUP), 0)).astype(f32)
    inv_t1 = 1.0 / t1
    pooled = []
    for gi, w in enumerate(POOL_WINDOWS):
        cols = slice(gi * C_GROUP, (gi + 1) * C_GROUP)
        ext = xc_ref[:, cols]
        win = ext
        span = 1
        while span < w:
            win = win + pltpu.roll(win, span, axis=0)
            span *= 2
        pooled.append(win[HALO:, :] * jnp.maximum(inv_t1, 1.0 / w) - ext[HALO:, :])
    p_zc = proj(SEG_ZC)
    yc_groups = []
    for half in range(2):
        lhs = jnp.concatenate(pooled[2 * half:2 * half + 2], axis=1).astype(bf16)
        yc_groups.append(jnp.dot(lhs, w_pool_ref[half], preferred_element_type=f32))
    xc_ref[0:HALO, :] = xc_ref[TS:TS + HALO, :]

    merged = _twice_sigmoid_of_half(proj(SEG_GA)) * acc_a
    acc_b = jnp.dot(yb, w_br_ref[1], preferred_element_type=f32)
    yc = jnp.concatenate(yc_groups, axis=1) * pool_scale
    yc = (yc * _silu_of_half(p_zc)).astype(bf16)
    merged = merged + _twice_sigmoid_of_half(proj(SEG_GB)) * acc_b
    gate_c = _twice_sigmoid_of_half(proj(SEG_GC))
    acc_c = jnp.dot(yc, w_br_ref[2], preferred_element_type=f32)

    h_ref[...] = _rmsnorm_rows(x_next_ref[0], norm_g).astype(bf16)
    puv_ref[:, 0:A_WIDTH] = proj(SEG_U)
    merged = merged + gate_c * acc_c

    y = x_ref[0] + jnp.dot(merged.astype(bf16), w_o_ref[...], preferred_element_type=f32)
    if apply_final_norm:
        y = _rmsnorm_rows(y, final_g)
    out_ref[0] = y
    puv_ref[:, A_WIDTH:] = proj(SEG_V)


def _pack_vectors(norm_g, ln_g, ln_b, conv_w, conv_b, pool_scale, final_g):
    depth = norm_g.shape[0]
    pair = lambda a, b: jnp.concatenate([a, b], axis=-1)
    rows = [
        norm_g,
        pair(ln_g, ln_b),
        pair(conv_b, pool_scale),
        pair(conv_w[:, 0], conv_w[:, 1]),
        pair(conv_w[:, 2], jnp.zeros_like(conv_b)),
        jnp.broadcast_to(final_g, (depth, D_MODEL)),
        jnp.zeros_like(norm_g),
        jnp.zeros_like(norm_g),
    ]
    return jnp.stack(rows, axis=1).astype(jnp.float32)


def _prepare_weights(w_s, b_s, w_pool, w_pa, w_pb, w_pc, w_o):
    bf16 = jnp.bfloat16
    depth = w_s.shape[0]
    causal = jnp.tril(jnp.ones((CHUNK, CHUNK), dtype=bool))
    w_m = jnp.where(causal, w_s, 0.0)
    w_sp = w_m.reshape(depth, A_GROUPS // 2, 2, CHUNK, CHUNK).transpose(0, 1, 3, 2, 4)
    w_sp = w_sp.reshape(depth, A_GROUPS // 2, CHUNK, 2 * CHUNK).astype(bf16)
    b_sp = jnp.repeat(jnp.swapaxes(b_s, 1, 2), A_HEAD, axis=2).astype(jnp.float32)
    n_pairs = len(POOL_WINDOWS) // 2
    eye = jnp.eye(2, dtype=w_pool.dtype)
    w_pairs = w_pool.reshape(depth, n_pairs, 2, C_GROUP, C_GROUP)
    w_pool_bd = w_pairs[:, :, :, :, None, :] * eye[None, None, :, None, :, None]
    w_pool_bd = w_pool_bd.reshape(depth, n_pairs, 2 * C_GROUP, 2 * C_GROUP).astype(bf16)
    w_br = jnp.stack([w_pa, w_pb, w_pc], axis=1).astype(bf16)
    w_o_b = (w_o * 0.5).astype(bf16)
    return w_sp, b_sp, w_pool_bd, w_br, w_o_b


def _layer_spec(array, layer):
    zeros = (0,) * (array.ndim - 1)
    return pl.BlockSpec((None,) + array.shape[1:], lambda b, j: (layer,) + zeros,
                        pipeline_mode=pl.Buffered(1))


def _layer(x, w_in, stacked, layer, *, apply_final_norm):
    bsz, seq, _ = x.shape
    assert seq % TS == 0 and TS % CHUNK == 0 and TS >= HALO
    assert w_in.dtype == jnp.float32 and w_in.shape[1:] == (D_MODEL, IN_TOTAL)
    assert D_MODEL % W_ROWS == 0
    steps_per_seq = seq // TS
    last_step = bsz * steps_per_seq - 1

    def next_tile(b, j):
        s = jnp.minimum(b * steps_per_seq + j + 1, last_step)
        return (s // steps_per_seq, s % steps_per_seq, 0)

    tile = (1, TS, D_MODEL)
    in_specs = [pl.BlockSpec(tile, lambda b, j: (b, j, 0)), pl.BlockSpec(tile, next_tile),
                pl.BlockSpec(memory_space=pl.ANY)]
    in_specs += [_layer_spec(a, layer) for a in stacked]

    return pl.pallas_call(
        functools.partial(_layer_kernel, layer=layer, steps_per_seq=steps_per_seq,
                          apply_final_norm=apply_final_norm),
        out_shape=jax.ShapeDtypeStruct(x.shape, x.dtype),
        grid=(bsz, steps_per_seq),
        in_specs=in_specs,
        out_specs=pl.BlockSpec(tile, lambda b, j: (b, j, 0)),
        scratch_shapes=[
            pltpu.VMEM((D_MODEL, IN_TOTAL), jnp.bfloat16),
            pltpu.VMEM((N_STAGE, W_ROWS, IN_TOTAL), jnp.float32),
            pltpu.SemaphoreType.DMA((N_STAGE,)),
            pltpu.VMEM((TS, D_MODEL), jnp.bfloat16),
            pltpu.VMEM((TS, 2 * A_WIDTH), jnp.float32),
            pltpu.VMEM((TS + HALO, B_WIDTH), jnp.float32),
            pltpu.VMEM((TS + HALO, C_WIDTH), jnp.float32),
        ],
        compiler_params=pltpu.CompilerParams(
            dimension_semantics=("arbitrary", "arbitrary"),
            vmem_limit_bytes=VMEM_LIMIT_BYTES,
        ),
        name="hybrid_layer_final" if apply_final_norm else "hybrid_layer",
    )(x, x, w_in, *stacked)


def kernel(x, norm_g, w_in, ln_g, ln_b, w_s, b_s, conv_w, conv_b, w_pool, pool_scale,
           w_pa, w_pb, w_pc, w_o, final_g):
    vecs = _pack_vectors(norm_g, ln_g, ln_b, conv_w, conv_b, pool_scale, final_g)
    stacked = (vecs,) + _prepare_weights(w_s, b_s, w_pool, w_pa, w_pb, w_pc, w_o)
    for layer in range(DEPTH):
        x = _layer(x, w_in, stacked, layer, apply_final_norm=(layer == DEPTH - 1))
    return x
```

```python
import functools
import math

import jax
import jax.numpy as jnp
from jax import lax
from jax.experimental import pallas as pl
from jax.experimental.pallas import tpu as pltpu

D_MODEL = 1024
DEPTH = 2
CHUNK = 128
A_GROUPS = 8
A_WIDTH = 512
A_HEAD = A_WIDTH // A_GROUPS
B_WIDTH = 512
CONV_WIDTH = 3
C_WIDTH = 512
POOL_WINDOWS = (2, 4, 8, 16)
C_GROUP = C_WIDTH // len(POOL_WINDOWS)
IN_TOTAL = 3 * A_WIDTH + 4 * B_WIDTH + 2 * C_WIDTH + 3 * D_MODEL
RMS_EPS = 1e-6
LN_EPS = 1e-5

_WIDTHS = [A_WIDTH] * 3 + [B_WIDTH] * 4 + [C_WIDTH] * 2 + [D_MODEL] * 3
_OFFS = [sum(_WIDTHS[:i]) for i in range(len(_WIDTHS) + 1)]
PROJ_SEGS = tuple((_OFFS[i], _OFFS[i + 1]) for i in range(len(_WIDTHS)))
(SEG_U, SEG_V, SEG_ZA, SEG_XB, SEG_BG, SEG_CG, SEG_ZB, SEG_XC, SEG_ZC,
 SEG_GA, SEG_GB, SEG_GC) = PROJ_SEGS
SEG_UV = (SEG_U[0], SEG_V[1])

LANES = 128
HALO = 16
TS = 512
W_ROWS = 64
N_STAGE = 3
VMEM_LIMIT_BYTES = 56 * 1024 * 1024


HALVED_SEGS = (SEG_U, SEG_V, SEG_ZA, SEG_ZB, SEG_ZC, SEG_GA, SEG_GB, SEG_GC)
_GELU_C1 = 2.0 * math.sqrt(2.0 / math.pi)
_GELU_C3 = 8.0 * 0.044715 * math.sqrt(2.0 / math.pi)


def _gelu_of_half(xh):
    t = jnp.tanh(xh * (_GELU_C1 + _GELU_C3 * (xh * xh)))
    return xh + xh * t


def _silu_of_half(zh):
    return zh + zh * jnp.tanh(zh)


def _twice_sigmoid_of_half(gh):
    return jnp.tanh(gh) + 1.0


def _rmsnorm_rows(x, gain):
    ms = jnp.mean(x * x, axis=-1, keepdims=True)
    return x * lax.rsqrt(ms + RMS_EPS) * gain


def _w_in_copy(w_in_hbm_ref, stage_ref, sem_ref, layer, i):
    return pltpu.make_async_copy(
        w_in_hbm_ref.at[layer, pl.ds(i * W_ROWS, W_ROWS), :],
        stage_ref.at[i % N_STAGE], sem_ref.at[i % N_STAGE])


def _load_w_in(w_in_hbm_ref, w_in_ref, stage_ref, sem_ref, layer):
    n_blocks = D_MODEL // W_ROWS
    for i in range(N_STAGE):
        _w_in_copy(w_in_hbm_ref, stage_ref, sem_ref, layer, i).start()

    def convert_block(i, carry):
        _w_in_copy(w_in_hbm_ref, stage_ref, sem_ref, layer, i).wait()
        rows = pl.ds(pl.multiple_of(i * W_ROWS, W_ROWS), W_ROWS)
        for lo, hi in PROJ_SEGS:
            w = stage_ref[i % N_STAGE, :, lo:hi]
            if (lo, hi) in HALVED_SEGS:
                w = w * 0.5
            w_in_ref[rows, lo:hi] = w.astype(jnp.bfloat16)

        @pl.when(i + N_STAGE < n_blocks)
        def _():
            _w_in_copy(w_in_hbm_ref, stage_ref, sem_ref, layer, i + N_STAGE).start()

        return carry

    lax.fori_loop(0, n_blocks, convert_block, 0)


def _prepare_small_weights(w_s_ref, b_st_ref, w_pool_ref, w_sp_ref, b_sp_ref, w_pool_bd_ref):
    f32 = jnp.float32
    bf16 = jnp.bfloat16
    row = lax.broadcasted_iota(jnp.int32, (CHUNK, CHUNK), 0)
    col = lax.broadcasted_iota(jnp.int32, (CHUNK, CHUNK), 1)
    causal = row >= col
    first_group = col < A_HEAD
    b_t = b_st_ref[...]
    for jb in range(A_GROUPS // 2):
        w_pair = [jnp.where(causal, w_s_ref[2 * jb + k], 0.0) for k in range(2)]
        w_sp_ref[jb] = jnp.concatenate(w_pair, axis=1).astype(bf16)
        b_pair = [jnp.broadcast_to(b_t[:, 2 * jb + k:2 * jb + k + 1], (CHUNK, LANES))
                  for k in range(2)]
        b_sp_ref[:, jb * LANES:(jb + 1) * LANES] = jnp.where(first_group, b_pair[0], b_pair[1])
    zero = jnp.zeros((C_GROUP, C_GROUP), f32)
    for i in range(len(POOL_WINDOWS) // 2):
        top = jnp.concatenate([w_pool_ref[2 * i], zero], axis=1)
        bottom = jnp.concatenate([zero, w_pool_ref[2 * i + 1]], axis=1)
        w_pool_bd_ref[i] = jnp.concatenate([top, bottom], axis=0).astype(bf16)


def _layer_kernel(x_ref, x_next_ref, w_in_hbm_ref, norm_g_ref, ln_g_ref, ln_b_ref, conv_w_ref,
                  conv_b_ref, pool_scale_ref, final_g_ref, w_s_ref, b_st_ref, w_pool_ref,
                  w_br_ref, w_o_ref, out_ref, w_in_ref, stage_ref, sem_ref, w_sp_ref, b_sp_ref,
                  w_pool_bd_ref, h_ref, puv_ref, cx_ref, xc_ref, *, layer, steps_per_seq,
                  apply_final_norm):
    f32 = jnp.float32
    bf16 = jnp.bfloat16
    j = pl.program_id(1)
    step = pl.program_id(0) * steps_per_seq + j

    norm_g = norm_g_ref[...]
    ln_g, ln_b = ln_g_ref[...], ln_b_ref[...]
    conv_b, pool_scale = conv_b_ref[...], pool_scale_ref[...]
    conv_w = [conv_w_ref[k:k + 1, :] for k in range(CONV_WIDTH)]
    final_g = final_g_ref[...]

    @pl.when(j == 0)
    def _():
        cx_ref[0:HALO, :] = jnp.zeros((HALO, B_WIDTH), f32)
        xc_ref[0:HALO, :] = jnp.zeros((HALO, C_WIDTH), f32)

    def proj(seg):
        return jnp.dot(h_ref[...], w_in_ref[:, seg[0]:seg[1]], preferred_element_type=f32)

    @pl.when(step == 0)
    def _():
        _load_w_in(w_in_hbm_ref, w_in_ref, stage_ref, sem_ref, layer)
        _prepare_small_weights(w_s_ref, b_st_ref, w_pool_ref, w_sp_ref, b_sp_ref, w_pool_bd_ref)
        h_ref[...] = _rmsnorm_rows(x_ref[0], norm_g).astype(bf16)
        puv_ref[...] = proj(SEG_UV)

    n_chunks = TS // CHUNK

    p_u = puv_ref[:, 0:A_WIDTH]
    p_v = puv_ref[:, A_WIDTH:]
    xc_ref[HALO:HALO + TS, :] = proj(SEG_XC)

    u = _gelu_of_half(p_u)
    v = _gelu_of_half(p_v)
    mu = jnp.mean(v, axis=-1, keepdims=True)
    vc = v - mu
    var = jnp.mean(vc * vc, axis=-1, keepdims=True)
    vn = vc * lax.rsqrt(var + LN_EPS) * ln_g + ln_b
    p_za = proj(SEG_ZA)
    cx_ref[HALO:HALO + TS, :] = proj(SEG_CG) * proj(SEG_XB)

    lane = lax.broadcasted_iota(jnp.int32, (CHUNK, LANES), 1)
    first_group = lane < A_HEAD
    sg_blocks = []
    for jb in range(A_WIDTH // LANES):
        rhs = []
        for c in range(n_chunks):
            vb = vn[c * CHUNK:(c + 1) * CHUNK, jb * LANES:(jb + 1) * LANES]
            rhs.append(jnp.concatenate([jnp.where(first_group, vb, 0.0),
                                        jnp.where(first_group, 0.0, vb)], axis=0))
        rhs = jnp.concatenate(rhs, axis=1).astype(bf16)
        mixed = jnp.dot(w_sp_ref[jb], rhs, preferred_element_type=f32)
        sg_blocks.append(jnp.concatenate(
            [mixed[:, c * LANES:(c + 1) * LANES] for c in range(n_chunks)], axis=0))
    sg = jnp.concatenate(sg_blocks, axis=1)
    sg = sg + jnp.concatenate([b_sp_ref[...]] * n_chunks, axis=0)
    p_bg = proj(SEG_BG)
    ya = (u * sg * _silu_of_half(p_za)).astype(bf16)
    p_zb = proj(SEG_ZB)
    acc_a = jnp.dot(ya, w_br_ref[0], preferred_element_type=f32)

    conv = conv_b + conv_w[CONV_WIDTH - 1] * cx_ref[HALO:HALO + TS, :]
    for k in range(CONV_WIDTH - 1):
        back = CONV_WIDTH - 1 - k
        conv = conv + conv_w[k] * cx_ref[HALO - back:HALO - back + TS, :]
    yb = (p_bg * conv * _silu_of_half(p_zb)).astype(bf16)
    cx_ref[0:HALO, :] = cx_ref[TS:TS + HALO, :]

    t1 = (j * TS + 1 + lax.broadcasted_iota(jnp.int32, (TS, C_GROUP), 0)).astype(f32)
    inv_t1 = 1.0 / t1
    pooled = []
    for gi, w in enumerate(POOL_WINDOWS):
        cols = slice(gi * C_GROUP, (gi + 1) * C_GROUP)
        ext = xc_ref[:, cols]
        win = ext
        span = 1
        while span < w:
            win = win + pltpu.roll(win, span, axis=0)
            span *= 2
        pooled.append(win[HALO:, :] * jnp.maximum(inv_t1, 1.0 / w) - ext[HALO:, :])
    p_zc = proj(SEG_ZC)
    yc_groups = []
    for half in range(2):
        lhs = jnp.concatenate(pooled[2 * half:2 * half + 2], axis=1).astype(bf16)
        yc_groups.append(jnp.dot(lhs, w_pool_bd_ref[half], preferred_element_type=f32))
    xc_ref[0:HALO, :] = xc_ref[TS:TS + HALO, :]

    merged = _twice_sigmoid_of_half(proj(SEG_GA)) * acc_a
    acc_b = jnp.dot(yb, w_br_ref[1], preferred_element_type=f32)
    yc = jnp.concatenate(yc_groups, axis=1) * pool_scale
    yc = (yc * _silu_of_half(p_zc)).astype(bf16)
    merged = merged + _twice_sigmoid_of_half(proj(SEG_GB)) * acc_b
    gate_c = _twice_sigmoid_of_half(proj(SEG_GC))
    acc_c = jnp.dot(yc, w_br_ref[2], preferred_element_type=f32)

    h_ref[...] = _rmsnorm_rows(x_next_ref[0], norm_g).astype(bf16)
    puv_ref[:, 0:A_WIDTH] = proj(SEG_U)
    merged = merged + gate_c * acc_c

    y = x_ref[0] + jnp.dot(merged.astype(bf16), w_o_ref[...], preferred_element_type=f32)
    if apply_final_norm:
        y = _rmsnorm_rows(y, final_g)
    out_ref[0] = y
    puv_ref[:, A_WIDTH:] = proj(SEG_V)


def _layer_spec(array, layer):
    zeros = (0,) * (array.ndim - 1)
    return pl.BlockSpec((None,) + array.shape[1:], lambda b, j: (layer,) + zeros,
                        pipeline_mode=pl.Buffered(1))


def _layer(x, w_in, per_layer, final_g, layer, *, apply_final_norm):
    bsz, seq, _ = x.shape
    assert seq % TS == 0 and TS % CHUNK == 0 and TS >= HALO
    assert w_in.dtype == jnp.float32 and w_in.shape[1:] == (D_MODEL, IN_TOTAL)
    assert D_MODEL % W_ROWS == 0
    steps_per_seq = seq // TS
    last_step = bsz * steps_per_seq - 1

    def next_tile(b, j):
        s = jnp.minimum(b * steps_per_seq + j + 1, last_step)
        return (s // steps_per_seq, s % steps_per_seq, 0)

    tile = (1, TS, D_MODEL)
    in_specs = [pl.BlockSpec(tile, lambda b, j: (b, j, 0)), pl.BlockSpec(tile, next_tile),
                pl.BlockSpec(memory_space=pl.ANY)]
    in_specs += [_layer_spec(a, layer) for a in per_layer[:6]]
    in_specs += [pl.BlockSpec(final_g.shape, lambda b, j: (0, 0), pipeline_mode=pl.Buffered(1))]
    in_specs += [_layer_spec(a, layer) for a in per_layer[6:]]

    return pl.pallas_call(
        functools.partial(_layer_kernel, layer=layer, steps_per_seq=steps_per_seq,
                          apply_final_norm=apply_final_norm),
        out_shape=jax.ShapeDtypeStruct(x.shape, x.dtype),
        grid=(bsz, steps_per_seq),
        in_specs=in_specs,
        out_specs=pl.BlockSpec(tile, lambda b, j: (b, j, 0)),
        scratch_shapes=[
            pltpu.VMEM((D_MODEL, IN_TOTAL), jnp.bfloat16),
            pltpu.VMEM((N_STAGE, W_ROWS, IN_TOTAL), jnp.float32),
            pltpu.SemaphoreType.DMA((N_STAGE,)),
            pltpu.VMEM((A_GROUPS // 2, CHUNK, 2 * CHUNK), jnp.bfloat16),
            pltpu.VMEM((CHUNK, A_WIDTH), jnp.float32),
            pltpu.VMEM((len(POOL_WINDOWS) // 2, 2 * C_GROUP, 2 * C_GROUP), jnp.bfloat16),
            pltpu.VMEM((TS, D_MODEL), jnp.bfloat16),
            pltpu.VMEM((TS, 2 * A_WIDTH), jnp.float32),
            pltpu.VMEM((TS + HALO, B_WIDTH), jnp.float32),
            pltpu.VMEM((TS + HALO, C_WIDTH), jnp.float32),
        ],
        compiler_params=pltpu.CompilerParams(
            dimension_semantics=("arbitrary", "arbitrary"),
            vmem_limit_bytes=VMEM_LIMIT_BYTES,
        ),
        name="hybrid_layer_final" if apply_final_norm else "hybrid_layer",
    )(x, x, w_in, *per_layer[:6], final_g, *per_layer[6:])


def kernel(x, norm_g, w_in, ln_g, ln_b, w_s, b_s, conv_w, conv_b, w_pool, pool_scale,
           w_pa, w_pb, w_pc, w_o, final_g):
    bf16 = jnp.bfloat16
    row = lambda a: a[:, None, :]
    w_br = jnp.stack([w_pa, w_pb, w_pc], axis=1).astype(bf16)
    w_o_half = (w_o * 0.5).astype(bf16)
    per_layer = (row(norm_g), row(ln_g), row(ln_b), conv_w, row(conv_b), row(pool_scale),
                 w_s, jnp.swapaxes(b_s, 1, 2), w_pool, w_br, w_o_half)
    for layer in range(DEPTH):
        x = _layer(x, w_in, per_layer, final_g[None, :], layer,
                   apply_final_norm=(layer == DEPTH - 1))
    return x
```

```python
import functools
import math

import jax
import jax.numpy as jnp
from jax import lax
from jax.experimental import pallas as pl
from jax.experimental.pallas import tpu as pltpu

D_MODEL = 1024
DEPTH = 2
CHUNK = 128
A_GROUPS = 8
A_WIDTH = 512
A_HEAD = A_WIDTH // A_GROUPS
B_WIDTH = 512
CONV_WIDTH = 3
C_WIDTH = 512
POOL_WINDOWS = (2, 4, 8, 16)
C_GROUP = C_WIDTH // len(POOL_WINDOWS)
IN_TOTAL = 3 * A_WIDTH + 4 * B_WIDTH + 2 * C_WIDTH + 3 * D_MODEL
RMS_EPS = 1e-6
LN_EPS = 1e-5

_WIDTHS = [A_WIDTH] * 3 + [B_WIDTH] * 4 + [C_WIDTH] * 2 + [D_MODEL] * 3
_OFFS = [sum(_WIDTHS[:i]) for i in range(len(_WIDTHS) + 1)]
PROJ_SEGS = tuple((_OFFS[i], _OFFS[i + 1]) for i in range(len(_WIDTHS)))
(SEG_U, SEG_V, SEG_ZA, SEG_XB, SEG_BG, SEG_CG, SEG_ZB, SEG_XC, SEG_ZC,
 SEG_GA, SEG_GB, SEG_GC) = PROJ_SEGS
SEG_UV = (SEG_U[0], SEG_V[1])

LANES = 128
HALO = 16
TS = 512
W_ROWS = 64
N_STAGE = 3
VMEM_LIMIT_BYTES = 56 * 1024 * 1024


HALVED_SEGS = (SEG_U, SEG_V, SEG_ZA, SEG_ZB, SEG_ZC, SEG_GA, SEG_GB, SEG_GC)
_GELU_C1 = 2.0 * math.sqrt(2.0 / math.pi)
_GELU_C3 = 8.0 * 0.044715 * math.sqrt(2.0 / math.pi)


def _gelu_of_half(xh):
    t = jnp.tanh(xh * (_GELU_C1 + _GELU_C3 * (xh * xh)))
    return xh + xh * t


def _silu_of_half(zh):
    return zh + zh * jnp.tanh(zh)


def _twice_sigmoid_of_half(gh):
    return jnp.tanh(gh) + 1.0


def _rmsnorm_rows(x, gain):
    ms = jnp.mean(x * x, axis=-1, keepdims=True)
    return x * lax.rsqrt(ms + RMS_EPS) * gain


def _w_in_copy(w_in_hbm_ref, stage_ref, sem_ref, layer, i):
    return pltpu.make_async_copy(
        w_in_hbm_ref.at[layer, pl.ds(i * W_ROWS, W_ROWS), :],
        stage_ref.at[i % N_STAGE], sem_ref.at[i % N_STAGE])


def _load_w_in(w_in_hbm_ref, w_in_ref, stage_ref, sem_ref, layer):
    n_blocks = D_MODEL // W_ROWS
    for i in range(N_STAGE):
        _w_in_copy(w_in_hbm_ref, stage_ref, sem_ref, layer, i).start()

    def convert_block(i, carry):
        _w_in_copy(w_in_hbm_ref, stage_ref, sem_ref, layer, i).wait()
        rows = pl.ds(pl.multiple_of(i * W_ROWS, W_ROWS), W_ROWS)
        for lo, hi in PROJ_SEGS:
            w = stage_ref[i % N_STAGE, :, lo:hi]
            if (lo, hi) in HALVED_SEGS:
                w = w * 0.5
            w_in_ref[rows, lo:hi] = w.astype(jnp.bfloat16)

        @pl.when(i + N_STAGE < n_blocks)
        def _():
            _w_in_copy(w_in_hbm_ref, stage_ref, sem_ref, layer, i + N_STAGE).start()

        return carry

    lax.fori_loop(0, n_blocks, convert_block, 0)


def _prepare_small_weights(w_s_ref, b_st_ref, w_pool_ref, w_sp_ref, b_sp_ref, w_pool_bd_ref):
    f32 = jnp.float32
    bf16 = jnp.bfloat16
    row = lax.broadcasted_iota(jnp.int32, (CHUNK, CHUNK), 0)
    col = lax.broadcasted_iota(jnp.int32, (CHUNK, CHUNK), 1)
    causal = row >= col
    first_group = col < A_HEAD
    b_t = b_st_ref[...]
    for jb in range(A_GROUPS // 2):
        w_pair = [jnp.where(causal, w_s_ref[2 * jb + k], 0.0) for k in range(2)]
        w_sp_ref[jb] = jnp.concatenate(w_pair, axis=1).astype(bf16)
        b_pair = [jnp.broadcast_to(b_t[:, 2 * jb + k:2 * jb + k + 1], (CHUNK, LANES))
                  for k in range(2)]
        b_sp_ref[:, jb * LANES:(jb + 1) * LANES] = jnp.where(first_group, b_pair[0], b_pair[1])
    zero = jnp.zeros((C_GROUP, C_GROUP), f32)
    for i in range(len(POOL_WINDOWS) // 2):
        top = jnp.concatenate([w_pool_ref[2 * i], zero], axis=1)
        bottom = jnp.concatenate([zero, w_pool_ref[2 * i + 1]], axis=1)
        w_pool_bd_ref[i] = jnp.concatenate([top, bottom], axis=0).astype(bf16)


def _layer_kernel(x_ref, x_next_ref, w_in_hbm_ref, norm_g_ref, ln_g_ref, ln_b_ref, conv_w_ref,
                  conv_b_ref, pool_scale_ref, final_g_ref, w_s_ref, b_st_ref, w_pool_ref,
                  w_pa_ref, w_pb_ref, w_pc_ref, w_o_ref, out_ref, w_in_ref, stage_ref, sem_ref,
                  w_sp_ref, b_sp_ref, w_pool_bd_ref, h_ref, puv_ref, cx_ref, xc_ref, *, layer,
                  steps_per_seq, apply_final_norm):
    f32 = jnp.float32
    bf16 = jnp.bfloat16
    j = pl.program_id(1)
    step = pl.program_id(0) * steps_per_seq + j

    this_layer = slice(layer, layer + 1)
    norm_g, final_g = norm_g_ref[this_layer, :], final_g_ref[...]
    ln_g, ln_b = ln_g_ref[this_layer, :], ln_b_ref[this_layer, :]
    conv_b, pool_scale = conv_b_ref[this_layer, :], pool_scale_ref[this_layer, :]
    conv_w = [conv_w_ref[layer, k:k + 1, :] for k in range(CONV_WIDTH)]

    @pl.when(j == 0)
    def _():
        cx_ref[0:HALO, :] = jnp.zeros((HALO, B_WIDTH), f32)
        xc_ref[0:HALO, :] = jnp.zeros((HALO, C_WIDTH), f32)

    def proj(seg):
        return jnp.dot(h_ref[...], w_in_ref[:, seg[0]:seg[1]], preferred_element_type=f32)

    @pl.when(step == 0)
    def _():
        _load_w_in(w_in_hbm_ref, w_in_ref, stage_ref, sem_ref, layer)
        _prepare_small_weights(w_s_ref, b_st_ref, w_pool_ref, w_sp_ref, b_sp_ref, w_pool_bd_ref)
        h_ref[...] = _rmsnorm_rows(x_ref[0], norm_g).astype(bf16)
        puv_ref[...] = proj(SEG_UV)

    n_chunks = TS // CHUNK

    p_u = puv_ref[:, 0:A_WIDTH]
    p_v = puv_ref[:, A_WIDTH:]
    xc_ref[HALO:HALO + TS, :] = proj(SEG_XC)

    u = _gelu_of_half(p_u)
    v = _gelu_of_half(p_v)
    mu = jnp.mean(v, axis=-1, keepdims=True)
    vc = v - mu
    var = jnp.mean(vc * vc, axis=-1, keepdims=True)
    vn = vc * lax.rsqrt(var + LN_EPS) * ln_g + ln_b
    p_za = proj(SEG_ZA)
    cx_ref[HALO:HALO + TS, :] = proj(SEG_CG) * proj(SEG_XB)

    lane = lax.broadcasted_iota(jnp.int32, (CHUNK, LANES), 1)
    first_group = lane < A_HEAD
    sg_blocks = []
    for jb in range(A_WIDTH // LANES):
        rhs = []
        for c in range(n_chunks):
            vb = vn[c * CHUNK:(c + 1) * CHUNK, jb * LANES:(jb + 1) * LANES]
            rhs.append(jnp.concatenate([jnp.where(first_group, vb, 0.0),
                                        jnp.where(first_group, 0.0, vb)], axis=0))
        rhs = jnp.concatenate(rhs, axis=1).astype(bf16)
        mixed = jnp.dot(w_sp_ref[jb], rhs, preferred_element_type=f32)
        sg_blocks.append(jnp.concatenate(
            [mixed[:, c * LANES:(c + 1) * LANES] for c in range(n_chunks)], axis=0))
    sg = jnp.concatenate(sg_blocks, axis=1)
    sg = sg + jnp.concatenate([b_sp_ref[...]] * n_chunks, axis=0)
    p_bg = proj(SEG_BG)
    ya = (u * sg * _silu_of_half(p_za)).astype(bf16)
    p_zb = proj(SEG_ZB)
    acc_a = jnp.dot(ya, w_pa_ref[...], preferred_element_type=f32)

    conv = conv_b + conv_w[CONV_WIDTH - 1] * cx_ref[HALO:HALO + TS, :]
    for k in range(CONV_WIDTH - 1):
        back = CONV_WIDTH - 1 - k
        conv = conv + conv_w[k] * cx_ref[HALO - back:HALO - back + TS, :]
    yb = (p_bg * conv * _silu_of_half(p_zb)).astype(bf16)
    cx_ref[0:HALO, :] = cx_ref[TS:TS + HALO, :]

    t1 = (j * TS + 1 + lax.broadcasted_iota(jnp.int32, (TS, C_GROUP), 0)).astype(f32)
    inv_t1 = 1.0 / t1
    pooled = []
    for gi, w in enumerate(POOL_WINDOWS):
        cols = slice(gi * C_GROUP, (gi + 1) * C_GROUP)
        ext = xc_ref[:, cols]
        win = ext
        span = 1
        while span < w:
            win = win + pltpu.roll(win, span, axis=0)
            span *= 2
        pooled.append(win[HALO:, :] * jnp.maximum(inv_t1, 1.0 / w) - ext[HALO:, :])
    p_zc = proj(SEG_ZC)
    yc_groups = []
    for half in range(2):
        lhs = jnp.concatenate(pooled[2 * half:2 * half + 2], axis=1).astype(bf16)
        yc_groups.append(jnp.dot(lhs, w_pool_bd_ref[half], preferred_element_type=f32))
    xc_ref[0:HALO, :] = xc_ref[TS:TS + HALO, :]

    merged = _twice_sigmoid_of_half(proj(SEG_GA)) * acc_a
    acc_b = jnp.dot(yb, w_pb_ref[...], preferred_element_type=f32)
    yc = jnp.concatenate(yc_groups, axis=1) * pool_scale
    yc = (yc * _silu_of_half(p_zc)).astype(bf16)
    merged = merged + _twice_sigmoid_of_half(proj(SEG_GB)) * acc_b
    gate_c = _twice_sigmoid_of_half(proj(SEG_GC))
    acc_c = jnp.dot(yc, w_pc_ref[...], preferred_element_type=f32)

    h_ref[...] = _rmsnorm_rows(x_next_ref[0], norm_g).astype(bf16)
    puv_ref[:, 0:A_WIDTH] = proj(SEG_U)
    merged = merged + gate_c * acc_c

    y = x_ref[0] + jnp.dot(merged.astype(bf16), w_o_ref[...], preferred_element_type=f32)
    if apply_final_norm:
        y = _rmsnorm_rows(y, final_g)
    out_ref[0] = y
    puv_ref[:, A_WIDTH:] = proj(SEG_V)


def _whole_spec(array):
    zeros = (0,) * array.ndim
    return pl.BlockSpec(array.shape, lambda b, j: zeros, pipeline_mode=pl.Buffered(1))


def _layer_spec(array, layer):
    zeros = (0,) * (array.ndim - 1)
    return pl.BlockSpec((None,) + array.shape[1:], lambda b, j: (layer,) + zeros,
                        pipeline_mode=pl.Buffered(1))


def _layer(x, w_in, vectors, per_layer, layer, *, apply_final_norm):
    bsz, seq, _ = x.shape
    assert seq % TS == 0 and TS % CHUNK == 0 and TS >= HALO
    assert w_in.dtype == jnp.float32 and w_in.shape[1:] == (D_MODEL, IN_TOTAL)
    assert D_MODEL % W_ROWS == 0
    steps_per_seq = seq // TS
    last_step = bsz * steps_per_seq - 1

    def next_tile(b, j):
        s = jnp.minimum(b * steps_per_seq + j + 1, last_step)
        return (s // steps_per_seq, s % steps_per_seq, 0)

    tile = (1, TS, D_MODEL)
    in_specs = [pl.BlockSpec(tile, lambda b, j: (b, j, 0)), pl.BlockSpec(tile, next_tile),
                pl.BlockSpec(memory_space=pl.ANY)]
    in_specs += [_whole_spec(a) for a in vectors]
    in_specs += [_layer_spec(a, layer) for a in per_layer]

    return pl.pallas_call(
        functools.partial(_layer_kernel, layer=layer, steps_per_seq=steps_per_seq,
                          apply_final_norm=apply_final_norm),
        out_shape=jax.ShapeDtypeStruct(x.shape, x.dtype),
        grid=(bsz, steps_per_seq),
        in_specs=in_specs,
        out_specs=pl.BlockSpec(tile, lambda b, j: (b, j, 0)),
        scratch_shapes=[
            pltpu.VMEM((D_MODEL, IN_TOTAL), jnp.bfloat16),
            pltpu.VMEM((N_STAGE, W_ROWS, IN_TOTAL), jnp.float32),
            pltpu.SemaphoreType.DMA((N_STAGE,)),
            pltpu.VMEM((A_GROUPS // 2, CHUNK, 2 * CHUNK), jnp.bfloat16),
            pltpu.VMEM((CHUNK, A_WIDTH), jnp.float32),
            pltpu.VMEM((len(POOL_WINDOWS) // 2, 2 * C_GROUP, 2 * C_GROUP), jnp.bfloat16),
            pltpu.VMEM((TS, D_MODEL), jnp.bfloat16),
            pltpu.VMEM((TS, 2 * A_WIDTH), jnp.float32),
            pltpu.VMEM((TS + HALO, B_WIDTH), jnp.float32),
            pltpu.VMEM((TS + HALO, C_WIDTH), jnp.float32),
        ],
        compiler_params=pltpu.CompilerParams(
            dimension_semantics=("arbitrary", "arbitrary"),
            vmem_limit_bytes=VMEM_LIMIT_BYTES,
        ),
        name="hybrid_layer_final" if apply_final_norm else "hybrid_layer",
    )(x, x, w_in, *vectors, *per_layer)


def kernel(x, norm_g, w_in, ln_g, ln_b, w_s, b_s, conv_w, conv_b, w_pool, pool_scale,
           w_pa, w_pb, w_pc, w_o, final_g):
    bf16 = jnp.bfloat16
    vectors = (norm_g, ln_g, ln_b, conv_w, conv_b, pool_scale, final_g[None, :])
    per_layer = (w_s, jnp.swapaxes(b_s, 1, 2), w_pool, w_pa.astype(bf16), w_pb.astype(bf16),
                 w_pc.astype(bf16), (w_o * 0.5).astype(bf16))
    for layer in range(DEPTH):
        x = _layer(x, w_in, vectors, per_layer, layer, apply_final_norm=(layer == DEPTH - 1))
    return x
```

```python
import functools
import math

import jax
import jax.numpy as jnp
from jax import lax
from jax.experimental import pallas as pl
from jax.experimental.pallas import tpu as pltpu

D_MODEL = 1024
DEPTH = 2
CHUNK = 128
A_GROUPS = 8
A_WIDTH = 512
A_HEAD = A_WIDTH // A_GROUPS
B_WIDTH = 512
CONV_WIDTH = 3
C_WIDTH = 512
POOL_WINDOWS = (2, 4, 8, 16)
C_GROUP = C_WIDTH // len(POOL_WINDOWS)
IN_TOTAL = 3 * A_WIDTH + 4 * B_WIDTH + 2 * C_WIDTH + 3 * D_MODEL
RMS_EPS = 1e-6
LN_EPS = 1e-5

_WIDTHS = [A_WIDTH] * 3 + [B_WIDTH] * 4 + [C_WIDTH] * 2 + [D_MODEL] * 3
_OFFS = [sum(_WIDTHS[:i]) for i in range(len(_WIDTHS) + 1)]
PROJ_SEGS = tuple((_OFFS[i], _OFFS[i + 1]) for i in range(len(_WIDTHS)))
(SEG_U, SEG_V, SEG_ZA, SEG_XB, SEG_BG, SEG_CG, SEG_ZB, SEG_XC, SEG_ZC,
 SEG_GA, SEG_GB, SEG_GC) = PROJ_SEGS
SEG_UV = (SEG_U[0], SEG_V[1])

LANES = 128
HALO = 16
TS = 512
W_ROWS = 64
N_STAGE = 3
O_ROWS = 256
VMEM_LIMIT_BYTES = 56 * 1024 * 1024


HALVED_SEGS = (SEG_U, SEG_V, SEG_ZA, SEG_ZB, SEG_ZC, SEG_GA, SEG_GB, SEG_GC)
_GELU_C1 = 2.0 * math.sqrt(2.0 / math.pi)
_GELU_C3 = 8.0 * 0.044715 * math.sqrt(2.0 / math.pi)


def _gelu_of_half(xh):
    t = jnp.tanh(xh * (_GELU_C1 + _GELU_C3 * (xh * xh)))
    return xh + xh * t


def _silu_of_half(zh):
    return zh + zh * jnp.tanh(zh)


def _twice_sigmoid_of_half(gh):
    return jnp.tanh(gh) + 1.0


def _rmsnorm_rows(x, gain):
    ms = jnp.mean(x * x, axis=-1, keepdims=True)
    return x * lax.rsqrt(ms + RMS_EPS) * gain


def _w_in_copy(w_in_hbm_ref, stage_ref, sem_ref, layer, i):
    return pltpu.make_async_copy(
        w_in_hbm_ref.at[layer, pl.ds(i * W_ROWS, W_ROWS), :],
        stage_ref.at[i % N_STAGE], sem_ref.at[i % N_STAGE])


def _load_w_in(w_in_hbm_ref, w_in_ref, stage_ref, sem_ref, layer):
    n_blocks = D_MODEL // W_ROWS
    for i in range(N_STAGE):
        _w_in_copy(w_in_hbm_ref, stage_ref, sem_ref, layer, i).start()

    def convert_block(i, carry):
        _w_in_copy(w_in_hbm_ref, stage_ref, sem_ref, layer, i).wait()
        rows = pl.ds(pl.multiple_of(i * W_ROWS, W_ROWS), W_ROWS)
        for lo, hi in PROJ_SEGS:
            w = stage_ref[i % N_STAGE, :, lo:hi]
            if (lo, hi) in HALVED_SEGS:
                w = w * 0.5
            w_in_ref[rows, lo:hi] = w.astype(jnp.bfloat16)

        @pl.when(i + N_STAGE < n_blocks)
        def _():
            _w_in_copy(w_in_hbm_ref, stage_ref, sem_ref, layer, i + N_STAGE).start()

        return carry

    lax.fori_loop(0, n_blocks, convert_block, 0)


def _load_out_weights(hbm_refs, vmem_refs, stage_ref, sem_ref, layer):
    blocks = [(src, dst, r) for src, dst in zip(hbm_refs, vmem_refs)
              for r in range(0, dst.shape[0], O_ROWS)]

    def copy(i):
        src, _, r = blocks[i]
        return pltpu.make_async_copy(src.at[layer, pl.ds(r, O_ROWS), :],
                                     stage_ref.at[pl.ds((i % 2) * O_ROWS, O_ROWS), :],
                                     sem_ref.at[i % 2])

    copy(0).start()
    copy(1).start()
    for i, (_, dst, r) in enumerate(blocks):
        copy(i).wait()
        w = stage_ref[(i % 2) * O_ROWS:(i % 2 + 1) * O_ROWS, :]
        if dst is vmem_refs[-1]:
            w = w * 0.5
        dst[r:r + O_ROWS, :] = w.astype(jnp.bfloat16)
        if i + 2 < len(blocks):
            copy(i + 2).start()


def _prepare_small_weights(w_s_ref, b_s_ref, w_pool_ref, w_sp_ref, b_sp_ref, w_pool_bd_ref):
    f32 = jnp.float32
    bf16 = jnp.bfloat16
    row = lax.broadcasted_iota(jnp.int32, (CHUNK, CHUNK), 0)
    col = lax.broadcasted_iota(jnp.int32, (CHUNK, CHUNK), 1)
    causal = row >= col
    first_group = col < A_HEAD
    b_t = b_s_ref[...].T
    for jb in range(A_GROUPS // 2):
        w_pair = [jnp.where(causal, w_s_ref[2 * jb + k], 0.0) for k in range(2)]
        w_sp_ref[jb] = jnp.concatenate(w_pair, axis=1).astype(bf16)
        b_pair = [jnp.broadcast_to(b_t[:, 2 * jb + k:2 * jb + k + 1], (CHUNK, LANES))
                  for k in range(2)]
        b_sp_ref[:, jb * LANES:(jb + 1) * LANES] = jnp.where(first_group, b_pair[0], b_pair[1])
    zero = jnp.zeros((C_GROUP, C_GROUP), f32)
    for i in range(len(POOL_WINDOWS) // 2):
        top = jnp.concatenate([w_pool_ref[2 * i], zero], axis=1)
        bottom = jnp.concatenate([zero, w_pool_ref[2 * i + 1]], axis=1)
        w_pool_bd_ref[i] = jnp.concatenate([top, bottom], axis=0).astype(bf16)


def _layer_kernel(x_ref, x_next_ref, w_in_hbm_ref, norm_g_ref, ln_g_ref, ln_b_ref, conv_w_ref,
                  conv_b_ref, pool_scale_ref, final_g_ref, w_s_ref, b_s_ref, w_pool_ref,
                  w_pa_hbm_ref, w_pb_hbm_ref, w_pc_hbm_ref, w_o_hbm_ref, out_ref, w_in_ref,
                  stage_ref, sem_ref, w_pa_ref, w_pb_ref, w_pc_ref, w_o_ref, w_sp_ref, b_sp_ref,
                  w_pool_bd_ref, h_ref, puv_ref, cx_ref, xc_ref, *, layer, steps_per_seq,
                  apply_final_norm):
    f32 = jnp.float32
    bf16 = jnp.bfloat16
    j = pl.program_id(1)
    step = pl.program_id(0) * steps_per_seq + j

    this_layer = slice(layer, layer + 1)
    norm_g, final_g = norm_g_ref[this_layer, :], final_g_ref[...]
    ln_g, ln_b = ln_g_ref[this_layer, :], ln_b_ref[this_layer, :]
    conv_b, pool_scale = conv_b_ref[this_layer, :], pool_scale_ref[this_layer, :]
    conv_w = [conv_w_ref[layer, k:k + 1, :] for k in range(CONV_WIDTH)]

    @pl.when(j == 0)
    def _():
        cx_ref[0:HALO, :] = jnp.zeros((HALO, B_WIDTH), f32)
        xc_ref[0:HALO, :] = jnp.zeros((HALO, C_WIDTH), f32)

    def proj(seg):
        return jnp.dot(h_ref[...], w_in_ref[:, seg[0]:seg[1]], preferred_element_type=f32)

    @pl.when(step == 0)
    def _():
        _load_w_in(w_in_hbm_ref, w_in_ref, stage_ref, sem_ref, layer)
        _load_out_weights((w_pa_hbm_ref, w_pb_hbm_ref, w_pc_hbm_ref, w_o_hbm_ref),
                          (w_pa_ref, w_pb_ref, w_pc_ref, w_o_ref), puv_ref, sem_ref, layer)
        _prepare_small_weights(w_s_ref, b_s_ref, w_pool_ref, w_sp_ref, b_sp_ref, w_pool_bd_ref)
        h_ref[...] = _rmsnorm_rows(x_ref[0], norm_g).astype(bf16)
        puv_ref[...] = proj(SEG_UV)

    n_chunks = TS // CHUNK

    p_u = puv_ref[:, 0:A_WIDTH]
    p_v = puv_ref[:, A_WIDTH:]
    xc_ref[HALO:HALO + TS, :] = proj(SEG_XC)

    u = _gelu_of_half(p_u)
    v = _gelu_of_half(p_v)
    mu = jnp.mean(v, axis=-1, keepdims=True)
    vc = v - mu
    var = jnp.mean(vc * vc, axis=-1, keepdims=True)
    vn = vc * lax.rsqrt(var + LN_EPS) * ln_g + ln_b
    p_za = proj(SEG_ZA)
    cx_ref[HALO:HALO + TS, :] = proj(SEG_CG) * proj(SEG_XB)

    lane = lax.broadcasted_iota(jnp.int32, (CHUNK, LANES), 1)
    first_group = lane < A_HEAD
    sg_blocks = []
    for jb in range(A_WIDTH // LANES):
        rhs = []
        for c in range(n_chunks):
            vb = vn[c * CHUNK:(c + 1) * CHUNK, jb * LANES:(jb + 1) * LANES]
            rhs.append(jnp.concatenate([jnp.where(first_group, vb, 0.0),
                                        jnp.where(first_group, 0.0, vb)], axis=0))
        rhs = jnp.concatenate(rhs, axis=1).astype(bf16)
        mixed = jnp.dot(w_sp_ref[jb], rhs, preferred_element_type=f32)
        sg_blocks.append(jnp.concatenate(
            [mixed[:, c * LANES:(c + 1) * LANES] for c in range(n_chunks)], axis=0))
    sg = jnp.concatenate(sg_blocks, axis=1)
    sg = sg + jnp.concatenate([b_sp_ref[...]] * n_chunks, axis=0)
    p_bg = proj(SEG_BG)
    ya = (u * sg * _silu_of_half(p_za)).astype(bf16)
    p_zb = proj(SEG_ZB)
    acc_a = jnp.dot(ya, w_pa_ref[...], preferred_element_type=f32)

    conv = conv_b + conv_w[CONV_WIDTH - 1] * cx_ref[HALO:HALO + TS, :]
    for k in range(CONV_WIDTH - 1):
        back = CONV_WIDTH - 1 - k
        conv = conv + conv_w[k] * cx_ref[HALO - back:HALO - back + TS, :]
    yb = (p_bg * conv * _silu_of_half(p_zb)).astype(bf16)
    cx_ref[0:HALO, :] = cx_ref[TS:TS + HALO, :]

    t1 = (j * TS + 1 + lax.broadcasted_iota(jnp.int32, (TS, C_GROUP), 0)).astype(f32)
    inv_t1 = 1.0 / t1
    pooled = []
    for gi, w in enumerate(POOL_WINDOWS):
        cols = slice(gi * C_GROUP, (gi + 1) * C_GROUP)
        ext = xc_ref[:, cols]
        win = ext
        span = 1
        while span < w:
            win = win + pltpu.roll(win, span, axis=0)
            span *= 2
        pooled.append(win[HALO:, :] * jnp.maximum(inv_t1, 1.0 / w) - ext[HALO:, :])
    p_zc = proj(SEG_ZC)
    yc_groups = []
    for half in range(2):
        lhs = jnp.concatenate(pooled[2 * half:2 * half + 2], axis=1).astype(bf16)
        yc_groups.append(jnp.dot(lhs, w_pool_bd_ref[half], preferred_element_type=f32))
    xc_ref[0:HALO, :] = xc_ref[TS:TS + HALO, :]

    merged = _twice_sigmoid_of_half(proj(SEG_GA)) * acc_a
    acc_b = jnp.dot(yb, w_pb_ref[...], preferred_element_type=f32)
    yc = jnp.concatenate(yc_groups, axis=1) * pool_scale
    yc = (yc * _silu_of_half(p_zc)).astype(bf16)
    merged = merged + _twice_sigmoid_of_half(proj(SEG_GB)) * acc_b
    gate_c = _twice_sigmoid_of_half(proj(SEG_GC))
    acc_c = jnp.dot(yc, w_pc_ref[...], preferred_element_type=f32)

    h_ref[...] = _rmsnorm_rows(x_next_ref[0], norm_g).astype(bf16)
    puv_ref[:, 0:A_WIDTH] = proj(SEG_U)
    merged = merged + gate_c * acc_c

    y = x_ref[0] + jnp.dot(merged.astype(bf16), w_o_ref[...], preferred_element_type=f32)
    if apply_final_norm:
        y = _rmsnorm_rows(y, final_g)
    out_ref[0] = y
    puv_ref[:, A_WIDTH:] = proj(SEG_V)


def _whole_spec(array):
    zeros = (0,) * array.ndim
    return pl.BlockSpec(array.shape, lambda b, j: zeros, pipeline_mode=pl.Buffered(1))


def _layer_spec(array, layer):
    zeros = (0,) * (array.ndim - 1)
    return pl.BlockSpec((None,) + array.shape[1:], lambda b, j: (layer,) + zeros,
                        pipeline_mode=pl.Buffered(1))


def _layer(x, w_in, vectors, per_layer, out_weights, layer, *, apply_final_norm):
    bsz, seq, _ = x.shape
    assert seq % TS == 0 and TS % CHUNK == 0 and TS >= HALO
    assert w_in.dtype == jnp.float32 and w_in.shape[1:] == (D_MODEL, IN_TOTAL)
    assert D_MODEL % W_ROWS == 0 and TS == 2 * O_ROWS and N_STAGE >= 2
    steps_per_seq = seq // TS
    last_step = bsz * steps_per_seq - 1

    def next_tile(b, j):
        s = jnp.minimum(b * steps_per_seq + j + 1, last_step)
        return (s // steps_per_seq, s % steps_per_seq, 0)

    tile = (1, TS, D_MODEL)
    in_specs = [pl.BlockSpec(tile, lambda b, j: (b, j, 0)), pl.BlockSpec(tile, next_tile),
                pl.BlockSpec(memory_space=pl.ANY)]
    in_specs += [_whole_spec(a) for a in vectors]
    in_specs += [_layer_spec(a, layer) for a in per_layer]
    in_specs += [pl.BlockSpec(memory_space=pl.ANY) for _ in out_weights]

    return pl.pallas_call(
        functools.partial(_layer_kernel, layer=layer, steps_per_seq=steps_per_seq,
                          apply_final_norm=apply_final_norm),
        out_shape=jax.ShapeDtypeStruct(x.shape, x.dtype),
        grid=(bsz, steps_per_seq),
        in_specs=in_specs,
        out_specs=pl.BlockSpec(tile, lambda b, j: (b, j, 0)),
        scratch_shapes=[
            pltpu.VMEM((D_MODEL, IN_TOTAL), jnp.bfloat16),
            pltpu.VMEM((N_STAGE, W_ROWS, IN_TOTAL), jnp.float32),
            pltpu.SemaphoreType.DMA((N_STAGE,)),
            pltpu.VMEM((A_WIDTH, D_MODEL), jnp.bfloat16),
            pltpu.VMEM((B_WIDTH, D_MODEL), jnp.bfloat16),
            pltpu.VMEM((C_WIDTH, D_MODEL), jnp.bfloat16),
            pltpu.VMEM((D_MODEL, D_MODEL), jnp.bfloat16),
            pltpu.VMEM((A_GROUPS // 2, CHUNK, 2 * CHUNK), jnp.bfloat16),
            pltpu.VMEM((CHUNK, A_WIDTH), jnp.float32),
            pltpu.VMEM((len(POOL_WINDOWS) // 2, 2 * C_GROUP, 2 * C_GROUP), jnp.bfloat16),
            pltpu.VMEM((TS, D_MODEL), jnp.bfloat16),
            pltpu.VMEM((TS, 2 * A_WIDTH), jnp.float32),
            pltpu.VMEM((TS + HALO, B_WIDTH), jnp.float32),
            pltpu.VMEM((TS + HALO, C_WIDTH), jnp.float32),
        ],
        compiler_params=pltpu.CompilerParams(
            dimension_semantics=("arbitrary", "arbitrary"),
            vmem_limit_bytes=VMEM_LIMIT_BYTES,
        ),
        name="hybrid_layer_final" if apply_final_norm else "hybrid_layer",
    )(x, x, w_in, *vectors, *per_layer, *out_weights)


def kernel(x, norm_g, w_in, ln_g, ln_b, w_s, b_s, conv_w, conv_b, w_pool, pool_scale,
           w_pa, w_pb, w_pc, w_o, final_g):
    vectors = (norm_g, ln_g, ln_b, conv_w, conv_b, pool_scale, final_g[None, :])
    per_layer = (w_s, b_s, w_pool)
    out_weights = (w_pa, w_pb, w_pc, w_o)
    for layer in range(DEPTH):
        x = _layer(x, w_in, vectors, per_layer, out_weights, layer,
                   apply_final_norm=(layer == DEPTH - 1))
    return x
```

```python
import functools
import math

import jax
import jax.numpy as jnp
from jax import lax
from jax.experimental import pallas as pl
from jax.experimental.pallas import tpu as pltpu

D_MODEL = 1024
DEPTH = 2
CHUNK = 128
A_GROUPS = 8
A_WIDTH = 512
A_HEAD = A_WIDTH // A_GROUPS
B_WIDTH = 512
CONV_WIDTH = 3
C_WIDTH = 512
POOL_WINDOWS = (2, 4, 8, 16)
C_GROUP = C_WIDTH // len(POOL_WINDOWS)
IN_TOTAL = 3 * A_WIDTH + 4 * B_WIDTH + 2 * C_WIDTH + 3 * D_MODEL
RMS_EPS = 1e-6
LN_EPS = 1e-5

_WIDTHS = [A_WIDTH] * 3 + [B_WIDTH] * 4 + [C_WIDTH] * 2 + [D_MODEL] * 3
_OFFS = [sum(_WIDTHS[:i]) for i in range(len(_WIDTHS) + 1)]
PROJ_SEGS = tuple((_OFFS[i], _OFFS[i + 1]) for i in range(len(_WIDTHS)))
(SEG_U, SEG_V, SEG_ZA, SEG_XB, SEG_BG, SEG_CG, SEG_ZB, SEG_XC, SEG_ZC,
 SEG_GA, SEG_GB, SEG_GC) = PROJ_SEGS
SEG_UV = (SEG_U[0], SEG_V[1])

LANES = 128
HALO = 16
TS = 512
W_ROWS = 64
N_STAGE = 3
O_ROWS = 256
VMEM_LIMIT_BYTES = 56 * 1024 * 1024


HALVED_SEGS = (SEG_U, SEG_V, SEG_ZA, SEG_ZB, SEG_ZC, SEG_GA, SEG_GB, SEG_GC)
_GELU_C1 = 2.0 * math.sqrt(2.0 / math.pi)
_GELU_C3 = 8.0 * 0.044715 * math.sqrt(2.0 / math.pi)


def _gelu_of_half(xh):
    t = jnp.tanh(xh * (_GELU_C1 + _GELU_C3 * (xh * xh)))
    return xh + xh * t


def _silu_of_half(zh):
    return zh + zh * jnp.tanh(zh)


def _twice_sigmoid_of_half(gh):
    return jnp.tanh(gh) + 1.0


def _rmsnorm_rows(x, gain):
    ms = jnp.mean(x * x, axis=-1, keepdims=True)
    return x * lax.rsqrt(ms + RMS_EPS) * gain


def _w_in_copy(w_in_hbm_ref, stage_ref, sem_ref, layer, i):
    return pltpu.make_async_copy(
        w_in_hbm_ref.at[layer, pl.ds(i * W_ROWS, W_ROWS), :],
        stage_ref.at[i % N_STAGE], sem_ref.at[i % N_STAGE])


def _begin_w_in_load(w_in_hbm_ref, w_in_ref, stage_ref, sem_ref, layer):
    n_blocks = D_MODEL // W_ROWS
    for i in range(N_STAGE):
        _w_in_copy(w_in_hbm_ref, stage_ref, sem_ref, layer, i).start()

    def convert_block(i, carry):
        _w_in_copy(w_in_hbm_ref, stage_ref, sem_ref, layer, i).wait()
        rows = pl.ds(pl.multiple_of(i * W_ROWS, W_ROWS), W_ROWS)
        for lo, hi in PROJ_SEGS:
            w = stage_ref[i % N_STAGE, :, lo:hi]
            if (lo, hi) in HALVED_SEGS:
                w = w * 0.5
            w_in_ref[rows, lo:hi] = w.astype(jnp.bfloat16)

        @pl.when(i + N_STAGE < n_blocks)
        def _():
            _w_in_copy(w_in_hbm_ref, stage_ref, sem_ref, layer, i + N_STAGE).start()

        return carry

    return lambda: lax.fori_loop(0, n_blocks, convert_block, 0)


def _begin_out_weights_load(hbm_refs, vmem_refs, stage_ref, sem_ref, layer):
    blocks = [(src, dst, r) for src, dst in zip(hbm_refs, vmem_refs)
              for r in range(0, dst.shape[0], O_ROWS)]

    def copy(i):
        src, _, r = blocks[i]
        return pltpu.make_async_copy(src.at[layer, pl.ds(r, O_ROWS), :],
                                     stage_ref.at[pl.ds((i % 2) * O_ROWS, O_ROWS), :],
                                     sem_ref.at[N_STAGE + i % 2])

    copy(0).start()
    copy(1).start()

    def finish():
        for i, (_, dst, r) in enumerate(blocks):
            copy(i).wait()
            w = stage_ref[(i % 2) * O_ROWS:(i % 2 + 1) * O_ROWS, :]
            if dst is vmem_refs[-1]:
                w = w * 0.5
            dst[r:r + O_ROWS, :] = w.astype(jnp.bfloat16)
            if i + 2 < len(blocks):
                copy(i + 2).start()

    return finish


def _prepare_small_weights(w_s_ref, b_s_ref, w_pool_ref, w_sp_ref, b_sp_ref, w_pool_bd_ref):
    f32 = jnp.float32
    bf16 = jnp.bfloat16
    row = lax.broadcasted_iota(jnp.int32, (CHUNK, CHUNK), 0)
    col = lax.broadcasted_iota(jnp.int32, (CHUNK, CHUNK), 1)
    causal = row >= col
    first_group = col < A_HEAD
    b_t = b_s_ref[...].T
    for jb in range(A_GROUPS // 2):
        w_pair = [jnp.where(causal, w_s_ref[2 * jb + k], 0.0) for k in range(2)]
        w_sp_ref[jb] = jnp.concatenate(w_pair, axis=1).astype(bf16)
        b_pair = [jnp.broadcast_to(b_t[:, 2 * jb + k:2 * jb + k + 1], (CHUNK, LANES))
                  for k in range(2)]
        b_sp_ref[:, jb * LANES:(jb + 1) * LANES] = jnp.where(first_group, b_pair[0], b_pair[1])
    zero = jnp.zeros((C_GROUP, C_GROUP), f32)
    for i in range(len(POOL_WINDOWS) // 2):
        top = jnp.concatenate([w_pool_ref[2 * i], zero], axis=1)
        bottom = jnp.concatenate([zero, w_pool_ref[2 * i + 1]], axis=1)
        w_pool_bd_ref[i] = jnp.concatenate([top, bottom], axis=0).astype(bf16)


def _layer_kernel(x_ref, x_next_ref, w_in_hbm_ref, norm_g_ref, ln_g_ref, ln_b_ref, conv_w_ref,
                  conv_b_ref, pool_scale_ref, final_g_ref, w_s_ref, b_s_ref, w_pool_ref,
                  w_pa_hbm_ref, w_pb_hbm_ref, w_pc_hbm_ref, w_o_hbm_ref, out_ref, w_in_ref,
                  stage_ref, sem_ref, w_pa_ref, w_pb_ref, w_pc_ref, w_o_ref, w_sp_ref, b_sp_ref,
                  w_pool_bd_ref, h_ref, puv_ref, cx_ref, xc_ref, *, layer, steps_per_seq,
                  apply_final_norm):
    f32 = jnp.float32
    bf16 = jnp.bfloat16
    j = pl.program_id(1)
    step = pl.program_id(0) * steps_per_seq + j

    this_layer = slice(layer, layer + 1)
    norm_g, final_g = norm_g_ref[this_layer, :], final_g_ref[...]
    ln_g, ln_b = ln_g_ref[this_layer, :], ln_b_ref[this_layer, :]
    conv_b, pool_scale = conv_b_ref[this_layer, :], pool_scale_ref[this_layer, :]
    conv_w = [conv_w_ref[layer, k:k + 1, :] for k in range(CONV_WIDTH)]

    @pl.when(j == 0)
    def _():
        cx_ref[0:HALO, :] = jnp.zeros((HALO, B_WIDTH), f32)
        xc_ref[0:HALO, :] = jnp.zeros((HALO, C_WIDTH), f32)

    def proj(seg):
        return jnp.dot(h_ref[...], w_in_ref[:, seg[0]:seg[1]], preferred_element_type=f32)

    @pl.when(step == 0)
    def _():
        finish_w_in = _begin_w_in_load(w_in_hbm_ref, w_in_ref, stage_ref, sem_ref, layer)
        finish_out_weights = _begin_out_weights_load(
            (w_pa_hbm_ref, w_pb_hbm_ref, w_pc_hbm_ref, w_o_hbm_ref),
            (w_pa_ref, w_pb_ref, w_pc_ref, w_o_ref), puv_ref, sem_ref, layer)
        _prepare_small_weights(w_s_ref, b_s_ref, w_pool_ref, w_sp_ref, b_sp_ref, w_pool_bd_ref)
        h_ref[...] = _rmsnorm_rows(x_ref[0], norm_g).astype(bf16)
        finish_w_in()
        finish_out_weights()
        puv_ref[...] = proj(SEG_UV)

    n_chunks = TS // CHUNK

    p_u = puv_ref[:, 0:A_WIDTH]
    p_v = puv_ref[:, A_WIDTH:]
    xc_ref[HALO:HALO + TS, :] = proj(SEG_XC)

    u = _gelu_of_half(p_u)
    v = _gelu_of_half(p_v)
    mu = jnp.mean(v, axis=-1, keepdims=True)
    vc = v - mu
    var = jnp.mean(vc * vc, axis=-1, keepdims=True)
    vn = vc * lax.rsqrt(var + LN_EPS) * ln_g + ln_b
    p_za = proj(SEG_ZA)
    cx_ref[HALO:HALO + TS, :] = proj(SEG_CG) * proj(SEG_XB)

    lane = lax.broadcasted_iota(jnp.int32, (CHUNK, LANES), 1)
    first_group = lane < A_HEAD
    sg_blocks = []
    for jb in range(A_WIDTH // LANES):
        rhs = []
        for c in range(n_chunks):
            vb = vn[c * CHUNK:(c + 1) * CHUNK, jb * LANES:(jb + 1) * LANES]
            rhs.append(jnp.concatenate([jnp.where(first_group, vb, 0.0),
                                        jnp.where(first_group, 0.0, vb)], axis=0))
        rhs = jnp.concatenate(rhs, axis=1).astype(bf16)
        mixed = jnp.dot(w_sp_ref[jb], rhs, preferred_element_type=f32)
        sg_blocks.append(jnp.concatenate(
            [mixed[:, c * LANES:(c + 1) * LANES] for c in range(n_chunks)], axis=0))
    sg = jnp.concatenate(sg_blocks, axis=1)
    sg = sg + jnp.concatenate([b_sp_ref[...]] * n_chunks, axis=0)
    p_bg = proj(SEG_BG)
    ya = (u * sg * _silu_of_half(p_za)).astype(bf16)
    p_zb = proj(SEG_ZB)
    acc_a = jnp.dot(ya, w_pa_ref[...], preferred_element_type=f32)

    conv = conv_b + conv_w[CONV_WIDTH - 1] * cx_ref[HALO:HALO + TS, :]
    for k in range(CONV_WIDTH - 1):
        back = CONV_WIDTH - 1 - k
        conv = conv + conv_w[k] * cx_ref[HALO - back:HALO - back + TS, :]
    yb = (p_bg * conv * _silu_of_half(p_zb)).astype(bf16)
    cx_ref[0:HALO, :] = cx_ref[TS:TS + HALO, :]

    t1 = (j * TS + 1 + lax.broadcasted_iota(jnp.int32, (TS, C_GROUP), 0)).astype(f32)
    inv_t1 = 1.0 / t1
    pooled = []
    for gi, w in enumerate(POOL_WINDOWS):
        cols = slice(gi * C_GROUP, (gi + 1) * C_GROUP)
        ext = xc_ref[:, cols]
        win = ext
        span = 1
        while span < w:
            win = win + pltpu.roll(win, span, axis=0)
            span *= 2
        pooled.append(win[HALO:, :] * jnp.maximum(inv_t1, 1.0 / w) - ext[HALO:, :])
    p_zc = proj(SEG_ZC)
    yc_groups = []
    for half in range(2):
        lhs = jnp.concatenate(pooled[2 * half:2 * half + 2], axis=1).astype(bf16)
        yc_groups.append(jnp.dot(lhs, w_pool_bd_ref[half], preferred_element_type=f32))
    xc_ref[0:HALO, :] = xc_ref[TS:TS + HALO, :]

    merged = _twice_sigmoid_of_half(proj(SEG_GA)) * acc_a
    acc_b = jnp.dot(yb, w_pb_ref[...], preferred_element_type=f32)
    yc = jnp.concatenate(yc_groups, axis=1) * pool_scale
    yc = (yc * _silu_of_half(p_zc)).astype(bf16)
    merged = merged + _twice_sigmoid_of_half(proj(SEG_GB)) * acc_b
    gate_c = _twice_sigmoid_of_half(proj(SEG_GC))
    acc_c = jnp.dot(yc, w_pc_ref[...], preferred_element_type=f32)

    h_ref[...] = _rmsnorm_rows(x_next_ref[0], norm_g).astype(bf16)
    puv_ref[:, 0:A_WIDTH] = proj(SEG_U)
    merged = merged + gate_c * acc_c

    y = x_ref[0] + jnp.dot(merged.astype(bf16), w_o_ref[...], preferred_element_type=f32)
    if apply_final_norm:
        y = _rmsnorm_rows(y, final_g)
    out_ref[0] = y
    puv_ref[:, A_WIDTH:] = proj(SEG_V)


def _whole_spec(array):
    zeros = (0,) * array.ndim
    return pl.BlockSpec(array.shape, lambda b, j: zeros, pipeline_mode=pl.Buffered(1))


def _layer_spec(array, layer):
    zeros = (0,) * (array.ndim - 1)
    return pl.BlockSpec((None,) + array.shape[1:], lambda b, j: (layer,) + zeros,
                        pipeline_mode=pl.Buffered(1))


def _layer(x, w_in, vectors, per_layer, out_weights, layer, *, apply_final_norm):
    bsz, seq, _ = x.shape
    assert seq % TS == 0 and TS % CHUNK == 0 and TS >= HALO
    assert w_in.dtype == jnp.float32 and w_in.shape[1:] == (D_MODEL, IN_TOTAL)
    assert D_MODEL % W_ROWS == 0 and TS == 2 * O_ROWS and N_STAGE >= 2
    steps_per_seq = seq // TS
    last_step = bsz * steps_per_seq - 1

    def next_tile(b, j):
        s = jnp.minimum(b * steps_per_seq + j + 1, last_step)
        return (s // steps_per_seq, s % steps_per_seq, 0)

    tile = (1, TS, D_MODEL)
    in_specs = [pl.BlockSpec(tile, lambda b, j: (b, j, 0)), pl.BlockSpec(tile, next_tile),
                pl.BlockSpec(memory_space=pl.ANY)]
    in_specs += [_whole_spec(a) for a in vectors]
    in_specs += [_layer_spec(a, layer) for a in per_layer]
    in_specs += [pl.BlockSpec(memory_space=pl.ANY) for _ in out_weights]

    return pl.pallas_call(
        functools.partial(_layer_kernel, layer=layer, steps_per_seq=steps_per_seq,
                          apply_final_norm=apply_final_norm),
        out_shape=jax.ShapeDtypeStruct(x.shape, x.dtype),
        grid=(bsz, steps_per_seq),
        in_specs=in_specs,
        out_specs=pl.BlockSpec(tile, lambda b, j: (b, j, 0)),
        scratch_shapes=[
            pltpu.VMEM((D_MODEL, IN_TOTAL), jnp.bfloat16),
            pltpu.VMEM((N_STAGE, W_ROWS, IN_TOTAL), jnp.float32),
            pltpu.SemaphoreType.DMA((N_STAGE + 2,)),
            pltpu.VMEM((A_WIDTH, D_MODEL), jnp.bfloat16),
            pltpu.VMEM((B_WIDTH, D_MODEL), jnp.bfloat16),
            pltpu.VMEM((C_WIDTH, D_MODEL), jnp.bfloat16),
            pltpu.VMEM((D_MODEL, D_MODEL), jnp.bfloat16),
            pltpu.VMEM((A_GROUPS // 2, CHUNK, 2 * CHUNK), jnp.bfloat16),
            pltpu.VMEM((CHUNK, A_WIDTH), jnp.float32),
            pltpu.VMEM((len(POOL_WINDOWS) // 2, 2 * C_GROUP, 2 * C_GROUP), jnp.bfloat16),
            pltpu.VMEM((TS, D_MODEL), jnp.bfloat16),
            pltpu.VMEM((TS, 2 * A_WIDTH), jnp.float32),
            pltpu.VMEM((TS + HALO, B_WIDTH), jnp.float32),
            pltpu.VMEM((TS + HALO, C_WIDTH), jnp.float32),
        ],
        compiler_params=pltpu.CompilerParams(
            dimension_semantics=("arbitrary", "arbitrary"),
            vmem_limit_bytes=VMEM_LIMIT_BYTES,
        ),
        name="hybrid_layer_final" if apply_final_norm else "hybrid_layer",
    )(x, x, w_in, *vectors, *per_layer, *out_weights)


def kernel(x, norm_g, w_in, ln_g, ln_b, w_s, b_s, conv_w, conv_b, w_pool, pool_scale,
           w_pa, w_pb, w_pc, w_o, final_g):
    vectors = (norm_g, ln_g, ln_b, conv_w, conv_b, pool_scale, final_g[None, :])
    per_layer = (w_s, b_s, w_pool)
    out_weights = (w_pa, w_pb, w_pc, w_o)
    for layer in range(DEPTH):
        x = _layer(x, w_in, vectors, per_layer, out_weights, layer,
                   apply_final_norm=(layer == DEPTH - 1))
    return x
```

```python
import functools
import math

import jax
import jax.numpy as jnp
from jax import lax
from jax.experimental import pallas as pl
from jax.experimental.pallas import tpu as pltpu

D_MODEL = 1024
DEPTH = 2
CHUNK = 128
A_GROUPS = 8
A_WIDTH = 512
A_HEAD = A_WIDTH // A_GROUPS
B_WIDTH = 512
CONV_WIDTH = 3
C_WIDTH = 512
POOL_WINDOWS = (2, 4, 8, 16)
C_GROUP = C_WIDTH // len(POOL_WINDOWS)
IN_TOTAL = 3 * A_WIDTH + 4 * B_WIDTH + 2 * C_WIDTH + 3 * D_MODEL
RMS_EPS = 1e-6
LN_EPS = 1e-5

_WIDTHS = [A_WIDTH] * 3 + [B_WIDTH] * 4 + [C_WIDTH] * 2 + [D_MODEL] * 3
_OFFS = [sum(_WIDTHS[:i]) for i in range(len(_WIDTHS) + 1)]
PROJ_SEGS = tuple((_OFFS[i], _OFFS[i + 1]) for i in range(len(_WIDTHS)))
(SEG_U, SEG_V, SEG_ZA, SEG_XB, SEG_BG, SEG_CG, SEG_ZB, SEG_XC, SEG_ZC,
 SEG_GA, SEG_GB, SEG_GC) = PROJ_SEGS
SEG_UV = (SEG_U[0], SEG_V[1])

LANES = 128
HALO = 16
TS = 512
W_ROWS = 64
N_STAGE = 3
O_ROWS = 256
VMEM_LIMIT_BYTES = 56 * 1024 * 1024


HALVED_SEGS = (SEG_U, SEG_V, SEG_ZA, SEG_ZB, SEG_ZC, SEG_GA, SEG_GB, SEG_GC)
_GELU_C1 = 2.0 * math.sqrt(2.0 / math.pi)
_GELU_C3 = 8.0 * 0.044715 * math.sqrt(2.0 / math.pi)


def _gelu_of_half(xh):
    t = jnp.tanh(xh * (_GELU_C1 + _GELU_C3 * (xh * xh)))
    return xh + xh * t


def _silu_of_half(zh):
    return zh + zh * jnp.tanh(zh)


def _twice_sigmoid_of_half(gh):
    return jnp.tanh(gh) + 1.0


def _rmsnorm_rows(x, gain):
    ms = jnp.mean(x * x, axis=-1, keepdims=True)
    return x * lax.rsqrt(ms + RMS_EPS) * gain


def _cast_w_in_rows(src, dst_ref, dst_rows):
    for lo, hi in PROJ_SEGS:
        w = src(lo, hi)
        if (lo, hi) in HALVED_SEGS:
            w = w * 0.5
        dst_ref[dst_rows, lo:hi] = w.astype(jnp.bfloat16)


def _w_in_copy(w_in_hbm_ref, stage_ref, sem_ref, layer, i):
    return pltpu.make_async_copy(
        w_in_hbm_ref.at[layer, pl.ds(i * W_ROWS, W_ROWS), :],
        stage_ref.at[i % N_STAGE], sem_ref.at[i % N_STAGE])


def _load_w_in(w_in_hbm_ref, w_in_ref, stage_ref, sem_ref, layer):
    n_blocks = D_MODEL // W_ROWS
    for i in range(N_STAGE):
        _w_in_copy(w_in_hbm_ref, stage_ref, sem_ref, layer, i).start()

    def convert_block(i, carry):
        _w_in_copy(w_in_hbm_ref, stage_ref, sem_ref, layer, i).wait()
        rows = pl.ds(pl.multiple_of(i * W_ROWS, W_ROWS), W_ROWS)
        _cast_w_in_rows(lambda lo, hi: stage_ref[i % N_STAGE, :, lo:hi], w_in_ref, rows)

        @pl.when(i + N_STAGE < n_blocks)
        def _():
            _w_in_copy(w_in_hbm_ref, stage_ref, sem_ref, layer, i + N_STAGE).start()

        return carry

    lax.fori_loop(0, n_blocks, convert_block, 0)


def _load_out_weights(hbm_refs, vmem_refs, stage_ref, sem_ref, layer):
    blocks = [(src, dst, r) for src, dst in zip(hbm_refs, vmem_refs)
              for r in range(0, dst.shape[0], O_ROWS)]

    def copy(i):
        src, _, r = blocks[i]
        return pltpu.make_async_copy(src.at[layer, pl.ds(r, O_ROWS), :],
                                     stage_ref.at[pl.ds((i % 2) * O_ROWS, O_ROWS), :],
                                     sem_ref.at[i % 2])

    copy(0).start()
    copy(1).start()
    for i, (_, dst, r) in enumerate(blocks):
        copy(i).wait()
        w = stage_ref[(i % 2) * O_ROWS:(i % 2 + 1) * O_ROWS, :]
        if dst is vmem_refs[-1]:
            w = w * 0.5
        dst[r:r + O_ROWS, :] = w.astype(jnp.bfloat16)
        if i + 2 < len(blocks):
            copy(i + 2).start()


class _NextLayerWIn:
    def __init__(self, w_in_hbm_ref, w_next_hbm_ref, stage_ref, cast_ref, sem_ref, layer,
                 step, n_steps):
        self.refs = (w_in_hbm_ref, w_next_hbm_ref, stage_ref, cast_ref, sem_ref)
        self.layer, self.step, self.n_steps = layer, step, n_steps
        self.rows = D_MODEL // n_steps

    def _rows_of(self, block):
        return pl.ds(pl.multiple_of(block * self.rows, self.rows), self.rows)

    def _fetch(self, block):
        w_in_hbm_ref, _, stage_ref, _, sem_ref = self.refs
        return pltpu.make_async_copy(w_in_hbm_ref.at[self.layer + 1, self._rows_of(block), :],
                                     stage_ref.at[block % 2, pl.ds(0, self.rows), :],
                                     sem_ref.at[block % 2])

    def _write_back(self, block):
        _, w_next_hbm_ref, _, cast_ref, sem_ref = self.refs
        return pltpu.make_async_copy(cast_ref.at[block % 2],
                                     w_next_hbm_ref.at[self._rows_of(block), :],
                                     sem_ref.at[2 + block % 2])

    def start_first_fetch(self):
        self._fetch(0).start()

    def begin_step(self):
        step = self.step
        self._fetch(step).wait()

        @pl.when(step + 1 < self.n_steps)
        def _():
            self._fetch(step + 1).start()

        @pl.when(step >= 2)
        def _():
            self._write_back(step - 2).wait()

    def cast(self):
        _, _, stage_ref, cast_ref, _ = self.refs
        slot = self.step % 2
        _cast_w_in_rows(lambda lo, hi: stage_ref[slot, 0:self.rows, lo:hi], cast_ref.at[slot],
                        slice(None))

    def end_step(self):
        step = self.step
        self._write_back(step).start()

        @pl.when(step == self.n_steps - 1)
        def _():
            self._write_back(step - 1).wait()
            self._write_back(step).wait()


def _prepare_small_weights(w_s_ref, b_s_ref, w_pool_ref, w_sp_ref, b_sp_ref, w_pool_bd_ref):
    f32 = jnp.float32
    bf16 = jnp.bfloat16
    row = lax.broadcasted_iota(jnp.int32, (CHUNK, CHUNK), 0)
    col = lax.broadcasted_iota(jnp.int32, (CHUNK, CHUNK), 1)
    causal = row >= col
    first_group = col < A_HEAD
    b_t = b_s_ref[...].T
    for jb in range(A_GROUPS // 2):
        w_pair = [jnp.where(causal, w_s_ref[2 * jb + k], 0.0) for k in range(2)]
        w_sp_ref[jb] = jnp.concatenate(w_pair, axis=1).astype(bf16)
        b_pair = [jnp.broadcast_to(b_t[:, 2 * jb + k:2 * jb + k + 1], (CHUNK, LANES))
                  for k in range(2)]
        b_sp_ref[:, jb * LANES:(jb + 1) * LANES] = jnp.where(first_group, b_pair[0], b_pair[1])
    zero = jnp.zeros((C_GROUP, C_GROUP), f32)
    for i in range(len(POOL_WINDOWS) // 2):
        top = jnp.concatenate([w_pool_ref[2 * i], zero], axis=1)
        bottom = jnp.concatenate([zero, w_pool_ref[2 * i + 1]], axis=1)
        w_pool_bd_ref[i] = jnp.concatenate([top, bottom], axis=0).astype(bf16)


def _layer_kernel(*refs, names, layer, steps_per_seq, n_steps, apply_final_norm):
    f32 = jnp.float32
    bf16 = jnp.bfloat16
    r = dict(zip(names, refs, strict=True))
    x_ref, x_next_ref, out_ref = r["x"], r["x_next"], r["out"]
    w_in_ref, sem_ref = r["w_in"], r["sem"]
    w_pa_ref, w_pb_ref, w_pc_ref, w_o_ref = r["w_pa"], r["w_pb"], r["w_pc"], r["w_o"]
    w_sp_ref, b_sp_ref, w_pool_bd_ref = r["w_sp"], r["b_sp"], r["w_pool_bd"]
    h_ref, puv_ref, cx_ref, xc_ref = r["h"], r["puv"], r["cx"], r["xc"]
    norm_g_ref, ln_g_ref, ln_b_ref = r["norm_g"], r["ln_g"], r["ln_b"]
    conv_w_ref, conv_b_ref, pool_scale_ref = r["conv_w"], r["conv_b"], r["pool_scale"]
    final_g_ref = r["final_g"]
    j = pl.program_id(1)
    step = pl.program_id(0) * steps_per_seq + j
    next_w_in = None
    if "w_in_next" in r:
        next_w_in = _NextLayerWIn(r["w_in_f32_hbm"], r["w_in_next"], r["stage"], r["cast"],
                                  r["next_sem"], layer, step, n_steps)

    this_layer = slice(layer, layer + 1)
    norm_g, final_g = norm_g_ref[this_layer, :], final_g_ref[...]
    ln_g, ln_b = ln_g_ref[this_layer, :], ln_b_ref[this_layer, :]
    conv_b, pool_scale = conv_b_ref[this_layer, :], pool_scale_ref[this_layer, :]
    conv_w = [conv_w_ref[layer, k:k + 1, :] for k in range(CONV_WIDTH)]

    @pl.when(j == 0)
    def _():
        cx_ref[0:HALO, :] = jnp.zeros((HALO, B_WIDTH), f32)
        xc_ref[0:HALO, :] = jnp.zeros((HALO, C_WIDTH), f32)

    def proj(seg):
        return jnp.dot(h_ref[...], w_in_ref[:, seg[0]:seg[1]], preferred_element_type=f32)

    @pl.when(step == 0)
    def _():
        if "w_in_bf16_hbm" in r:
            ready_w_in = pltpu.make_async_copy(r["w_in_bf16_hbm"], w_in_ref, sem_ref.at[2])
            ready_w_in.start()
        else:
            _load_w_in(r["w_in_f32_hbm"], w_in_ref, r["stage"], sem_ref, layer)
        _load_out_weights((r["w_pa_hbm"], r["w_pb_hbm"], r["w_pc_hbm"], r["w_o_hbm"]),
                          (w_pa_ref, w_pb_ref, w_pc_ref, w_o_ref), puv_ref, sem_ref, layer)
        _prepare_small_weights(r["w_s"], r["b_s"], r["w_pool"], w_sp_ref, b_sp_ref,
                               w_pool_bd_ref)
        h_ref[...] = _rmsnorm_rows(x_ref[0], norm_g).astype(bf16)
        if "w_in_bf16_hbm" in r:
            ready_w_in.wait()
        puv_ref[...] = proj(SEG_UV)
        if next_w_in is not None:
            next_w_in.start_first_fetch()

    if next_w_in is not None:
        next_w_in.begin_step()
        next_w_in.cast()

    n_chunks = TS // CHUNK

    p_u = puv_ref[:, 0:A_WIDTH]
    p_v = puv_ref[:, A_WIDTH:]
    xc_ref[HALO:HALO + TS, :] = proj(SEG_XC)

    u = _gelu_of_half(p_u)
    v = _gelu_of_half(p_v)
    mu = jnp.mean(v, axis=-1, keepdims=True)
    vc = v - mu
    var = jnp.mean(vc * vc, axis=-1, keepdims=True)
    vn = vc * lax.rsqrt(var + LN_EPS) * ln_g + ln_b
    p_za = proj(SEG_ZA)
    cx_ref[HALO:HALO + TS, :] = proj(SEG_CG) * proj(SEG_XB)

    lane = lax.broadcasted_iota(jnp.int32, (CHUNK, LANES), 1)
    first_group = lane < A_HEAD
    sg_blocks = []
    for jb in range(A_WIDTH // LANES):
        rhs = []
        for c in range(n_chunks):
            vb = vn[c * CHUNK:(c + 1) * CHUNK, jb * LANES:(jb + 1) * LANES]
            rhs.append(jnp.concatenate([jnp.where(first_group, vb, 0.0),
                                        jnp.where(first_group, 0.0, vb)], axis=0))
        rhs = jnp.concatenate(rhs, axis=1).astype(bf16)
        mixed = jnp.dot(w_sp_ref[jb], rhs, preferred_element_type=f32)
        sg_blocks.append(jnp.concatenate(
            [mixed[:, c * LANES:(c + 1) * LANES] for c in range(n_chunks)], axis=0))
    sg = jnp.concatenate(sg_blocks, axis=1)
    sg = sg + jnp.concatenate([b_sp_ref[...]] * n_chunks, axis=0)
    p_bg = proj(SEG_BG)
    ya = (u * sg * _silu_of_half(p_za)).astype(bf16)
    p_zb = proj(SEG_ZB)
    acc_a = jnp.dot(ya, w_pa_ref[...], preferred_element_type=f32)

    conv = conv_b + conv_w[CONV_WIDTH - 1] * cx_ref[HALO:HALO + TS, :]
    for k in range(CONV_WIDTH - 1):
        back = CONV_WIDTH - 1 - k
        conv = conv + conv_w[k] * cx_ref[HALO - back:HALO - back + TS, :]
    yb = (p_bg * conv * _silu_of_half(p_zb)).astype(bf16)
    cx_ref[0:HALO, :] = cx_ref[TS:TS + HALO, :]

    t1 = (j * TS + 1 + lax.broadcasted_iota(jnp.int32, (TS, C_GROUP), 0)).astype(f32)
    inv_t1 = 1.0 / t1
    pooled = []
    for gi, w in enumerate(POOL_WINDOWS):
        cols = slice(gi * C_GROUP, (gi + 1) * C_GROUP)
        ext = xc_ref[:, cols]
        win = ext
        span = 1
        while span < w:
            win = win + pltpu.roll(win, span, axis=0)
            span *= 2
        pooled.append(win[HALO:, :] * jnp.maximum(inv_t1, 1.0 / w) - ext[HALO:, :])
    p_zc = proj(SEG_ZC)
    yc_groups = []
    for half in range(2):
        lhs = jnp.concatenate(pooled[2 * half:2 * half + 2], axis=1).astype(bf16)
        yc_groups.append(jnp.dot(lhs, w_pool_bd_ref[half], preferred_element_type=f32))
    xc_ref[0:HALO, :] = xc_ref[TS:TS + HALO, :]

    merged = _twice_sigmoid_of_half(proj(SEG_GA)) * acc_a
    acc_b = jnp.dot(yb, w_pb_ref[...], preferred_element_type=f32)
    yc = jnp.concatenate(yc_groups, axis=1) * pool_scale
    yc = (yc * _silu_of_half(p_zc)).astype(bf16)
    merged = merged + _twice_sigmoid_of_half(proj(SEG_GB)) * acc_b
    gate_c = _twice_sigmoid_of_half(proj(SEG_GC))
    acc_c = jnp.dot(yc, w_pc_ref[...], preferred_element_type=f32)

    h_ref[...] = _rmsnorm_rows(x_next_ref[0], norm_g).astype(bf16)
    puv_ref[:, 0:A_WIDTH] = proj(SEG_U)
    merged = merged + gate_c * acc_c

    y = x_ref[0] + jnp.dot(merged.astype(bf16), w_o_ref[...], preferred_element_type=f32)
    if apply_final_norm:
        y = _rmsnorm_rows(y, final_g)
    out_ref[0] = y
    puv_ref[:, A_WIDTH:] = proj(SEG_V)
    if next_w_in is not None:
        next_w_in.end_step()


def _whole_spec(array):
    zeros = (0,) * array.ndim
    return pl.BlockSpec(array.shape, lambda b, j: zeros, pipeline_mode=pl.Buffered(1))


def _layer_spec(array, layer):
    zeros = (0,) * (array.ndim - 1)
    return pl.BlockSpec((None,) + array.shape[1:], lambda b, j: (layer,) + zeros,
                        pipeline_mode=pl.Buffered(1))


VECTOR_NAMES = ("norm_g", "ln_g", "ln_b", "conv_w", "conv_b", "pool_scale", "final_g")
PER_LAYER_NAMES = ("w_s", "b_s", "w_pool")
OUT_WEIGHT_NAMES = ("w_pa", "w_pb", "w_pc", "w_o")


def _layer(x, w_in, w_in_bf16, vectors, per_layer, out_weights, layer, *, convert_next,
           apply_final_norm):
    bsz, seq, _ = x.shape
    assert seq % TS == 0 and TS % CHUNK == 0 and TS >= HALO
    assert w_in.dtype == jnp.float32 and w_in.shape[1:] == (D_MODEL, IN_TOTAL)
    assert D_MODEL % W_ROWS == 0 and TS == 2 * O_ROWS and N_STAGE >= 3
    steps_per_seq = seq // TS
    n_steps = bsz * steps_per_seq
    bf16 = jnp.bfloat16
    f32 = jnp.float32

    def next_tile(b, j):
        s = jnp.minimum(b * steps_per_seq + j + 1, n_steps - 1)
        return (s // steps_per_seq, s % steps_per_seq, 0)

    tile = (1, TS, D_MODEL)
    any_spec = pl.BlockSpec(memory_space=pl.ANY)
    inputs = [("x", x, pl.BlockSpec(tile, lambda b, j: (b, j, 0))),
              ("x_next", x, pl.BlockSpec(tile, next_tile)),
              ("w_in_f32_hbm", w_in, any_spec)]
    if w_in_bf16 is not None:
        inputs.append(("w_in_bf16_hbm", w_in_bf16, any_spec))
    inputs += [(n, a, _whole_spec(a)) for n, a in zip(VECTOR_NAMES, vectors, strict=True)]
    inputs += [(n, a, _layer_spec(a, layer))
               for n, a in zip(PER_LAYER_NAMES, per_layer, strict=True)]
    inputs += [(n + "_hbm", a, any_spec)
               for n, a in zip(OUT_WEIGHT_NAMES, out_weights, strict=True)]

    outputs = [("out", jax.ShapeDtypeStruct(x.shape, x.dtype),
                pl.BlockSpec(tile, lambda b, j: (b, j, 0)))]
    scratch = [
        ("w_in", pltpu.VMEM((D_MODEL, IN_TOTAL), bf16)),
        ("sem", pltpu.SemaphoreType.DMA((N_STAGE,))),
        ("w_pa", pltpu.VMEM((A_WIDTH, D_MODEL), bf16)),
        ("w_pb", pltpu.VMEM((B_WIDTH, D_MODEL), bf16)),
        ("w_pc", pltpu.VMEM((C_WIDTH, D_MODEL), bf16)),
        ("w_o", pltpu.VMEM((D_MODEL, D_MODEL), bf16)),
        ("w_sp", pltpu.VMEM((A_GROUPS // 2, CHUNK, 2 * CHUNK), bf16)),
        ("b_sp", pltpu.VMEM((CHUNK, A_WIDTH), f32)),
        ("w_pool_bd", pltpu.VMEM((len(POOL_WINDOWS) // 2, 2 * C_GROUP, 2 * C_GROUP), bf16)),
        ("h", pltpu.VMEM((TS, D_MODEL), bf16)),
        ("puv", pltpu.VMEM((TS, 2 * A_WIDTH), f32)),
        ("cx", pltpu.VMEM((TS + HALO, B_WIDTH), f32)),
        ("xc", pltpu.VMEM((TS + HALO, C_WIDTH), f32)),
    ]
    if w_in_bf16 is None or convert_next:
        scratch.append(("stage", pltpu.VMEM((N_STAGE, W_ROWS, IN_TOTAL), f32)))
    if convert_next:
        assert D_MODEL % n_steps == 0 and (D_MODEL // n_steps) % 16 == 0
        assert D_MODEL // n_steps <= W_ROWS and n_steps >= 2
        outputs.append(("w_in_next", jax.ShapeDtypeStruct((D_MODEL, IN_TOTAL), bf16), any_spec))
        scratch += [("cast", pltpu.VMEM((2, D_MODEL // n_steps, IN_TOTAL), bf16)),
                    ("next_sem", pltpu.SemaphoreType.DMA((4,)))]

    names = tuple(n for n, *_ in inputs + outputs + scratch)
    results = pl.pallas_call(
        functools.partial(_layer_kernel, names=names, layer=layer, steps_per_seq=steps_per_seq,
                          n_steps=n_steps, apply_final_norm=apply_final_norm),
        out_shape=[s for _, s, _ in outputs],
        grid=(bsz, steps_per_seq),
        in_specs=[spec for _, _, spec in inputs],
        out_specs=[spec for _, _, spec in outputs],
        scratch_shapes=[s for _, s in scratch],
        compiler_params=pltpu.CompilerParams(
            dimension_semantics=("arbitrary", "arbitrary"),
            vmem_limit_bytes=VMEM_LIMIT_BYTES,
        ),
        name="hybrid_layer_final" if apply_final_norm else "hybrid_layer",
    )(*[a for _, a, _ in inputs])
    return results[0], (results[1] if convert_next else None)


def kernel(x, norm_g, w_in, ln_g, ln_b, w_s, b_s, conv_w, conv_b, w_pool, pool_scale,
           w_pa, w_pb, w_pc, w_o, final_g):
    vectors = (norm_g, ln_g, ln_b, conv_w, conv_b, pool_scale, final_g[None, :])
    per_layer = (w_s, b_s, w_pool)
    out_weights = (w_pa, w_pb, w_pc, w_o)
    w_in_bf16 = None
    for layer in range(DEPTH):
        last = layer == DEPTH - 1
        x, w_in_bf16 = _layer(x, w_in, w_in_bf16, vectors, per_layer, out_weights, layer,
                              convert_next=not last, apply_final_norm=last)
    return x
```

```python
import functools
import math

import jax
import jax.numpy as jnp
from jax import lax
from jax.experimental import pallas as pl
from jax.experimental.pallas import tpu as pltpu

D_MODEL = 1024
DEPTH = 2
CHUNK = 128
A_GROUPS = 8
A_WIDTH = 512
A_HEAD = A_WIDTH // A_GROUPS
B_WIDTH = 512
CONV_WIDTH = 3
C_WIDTH = 512
POOL_WINDOWS = (2, 4, 8, 16)
C_GROUP = C_WIDTH // len(POOL_WINDOWS)
IN_TOTAL = 3 * A_WIDTH + 4 * B_WIDTH + 2 * C_WIDTH + 3 * D_MODEL
RMS_EPS = 1e-6
LN_EPS = 1e-5

_WIDTHS = [A_WIDTH] * 3 + [B_WIDTH] * 4 + [C_WIDTH] * 2 + [D_MODEL] * 3
_OFFS = [sum(_WIDTHS[:i]) for i in range(len(_WIDTHS) + 1)]
PROJ_SEGS = tuple((_OFFS[i], _OFFS[i + 1]) for i in range(len(_WIDTHS)))
(SEG_U, SEG_V, SEG_ZA, SEG_XB, SEG_BG, SEG_CG, SEG_ZB, SEG_XC, SEG_ZC,
 SEG_GA, SEG_GB, SEG_GC) = PROJ_SEGS
SEG_UV = (SEG_U[0], SEG_V[1])
PROJ_JOBS = ("xc", "za", "cg", "xb", "bg", "zb", "zc", "ga", "gb", "gc")
MXU_ORDER = ("za", "cg", "xb", "sp", "bg", "gb", "xc", "zb", "pa", "zc", "po", "ga", "pb", "gc",
             "u_next", "pc", "wo", "v_next")

LANES = 128
HALO = 16
TS = 512
W_ROWS = 64
N_STAGE = 3
O_ROWS = 256
VMEM_LIMIT_BYTES = 56 * 1024 * 1024


HALVED_SEGS = (SEG_U, SEG_V, SEG_ZA, SEG_ZB, SEG_ZC, SEG_GA, SEG_GB, SEG_GC)
_GELU_C1 = 2.0 * math.sqrt(2.0 / math.pi)
_GELU_C3 = 8.0 * 0.044715 * math.sqrt(2.0 / math.pi)


def _gelu_of_half(xh):
    t = jnp.tanh(xh * (_GELU_C1 + _GELU_C3 * (xh * xh)))
    return xh + xh * t


def _silu_of_half(zh):
    return zh + zh * jnp.tanh(zh)


def _twice_sigmoid_of_half(gh):
    return jnp.tanh(gh) + 1.0


def _rmsnorm_rows(x, gain):
    ms = jnp.mean(x * x, axis=-1, keepdims=True)
    return x * lax.rsqrt(ms + RMS_EPS) * gain


def _cast_w_in_rows(src, dst_ref, dst_rows):
    for lo, hi in PROJ_SEGS:
        w = src(lo, hi)
        if (lo, hi) in HALVED_SEGS:
            w = w * 0.5
        dst_ref[dst_rows, lo:hi] = w.astype(jnp.bfloat16)


def _w_in_copy(w_in_hbm_ref, stage_ref, sem_ref, layer, i):
    return pltpu.make_async_copy(
        w_in_hbm_ref.at[layer, pl.ds(i * W_ROWS, W_ROWS), :],
        stage_ref.at[i % N_STAGE], sem_ref.at[i % N_STAGE])


def _load_w_in(w_in_hbm_ref, w_in_ref, stage_ref, sem_ref, layer):
    n_blocks = D_MODEL // W_ROWS
    for i in range(N_STAGE):
        _w_in_copy(w_in_hbm_ref, stage_ref, sem_ref, layer, i).start()

    def convert_block(i, carry):
        _w_in_copy(w_in_hbm_ref, stage_ref, sem_ref, layer, i).wait()
        rows = pl.ds(pl.multiple_of(i * W_ROWS, W_ROWS), W_ROWS)
        _cast_w_in_rows(lambda lo, hi: stage_ref[i % N_STAGE, :, lo:hi], w_in_ref, rows)

        @pl.when(i + N_STAGE < n_blocks)
        def _():
            _w_in_copy(w_in_hbm_ref, stage_ref, sem_ref, layer, i + N_STAGE).start()

        return carry

    lax.fori_loop(0, n_blocks, convert_block, 0)


def _load_out_weights(hbm_refs, vmem_refs, stage_ref, sem_ref, layer):
    blocks = [(src, dst, r) for src, dst in zip(hbm_refs, vmem_refs)
              for r in range(0, dst.shape[0], O_ROWS)]

    def copy(i):
        src, _, r = blocks[i]
        return pltpu.make_async_copy(src.at[layer, pl.ds(r, O_ROWS), :],
                                     stage_ref.at[pl.ds((i % 2) * O_ROWS, O_ROWS), :],
                                     sem_ref.at[i % 2])

    copy(0).start()
    copy(1).start()
    for i, (_, dst, r) in enumerate(blocks):
        copy(i).wait()
        w = stage_ref[(i % 2) * O_ROWS:(i % 2 + 1) * O_ROWS, :]
        if dst is vmem_refs[-1]:
            w = w * 0.5
        dst[r:r + O_ROWS, :] = w.astype(jnp.bfloat16)
        if i + 2 < len(blocks):
            copy(i + 2).start()


class _NextLayerWIn:
    def __init__(self, w_in_hbm_ref, w_next_hbm_ref, stage_ref, cast_ref, sem_ref, layer,
                 step, n_steps):
        self.refs = (w_in_hbm_ref, w_next_hbm_ref, stage_ref, cast_ref, sem_ref)
        self.layer, self.step, self.n_steps = layer, step, n_steps
        self.rows = D_MODEL // n_steps

    def _rows_of(self, block):
        return pl.ds(pl.multiple_of(block * self.rows, self.rows), self.rows)

    def _fetch(self, block):
        w_in_hbm_ref, _, stage_ref, _, sem_ref = self.refs
        return pltpu.make_async_copy(w_in_hbm_ref.at[self.layer + 1, self._rows_of(block), :],
                                     stage_ref.at[block % 2, pl.ds(0, self.rows), :],
                                     sem_ref.at[block % 2])

    def _write_back(self, block):
        _, w_next_hbm_ref, _, cast_ref, sem_ref = self.refs
        return pltpu.make_async_copy(cast_ref.at[block % 2],
                                     w_next_hbm_ref.at[self._rows_of(block), :],
                                     sem_ref.at[2 + block % 2])

    def start_first_fetch(self):
        self._fetch(0).start()

    def begin_step(self):
        step = self.step
        self._fetch(step).wait()

        @pl.when(step + 1 < self.n_steps)
        def _():
            self._fetch(step + 1).start()

        @pl.when(step >= 2)
        def _():
            self._write_back(step - 2).wait()

    def cast(self):
        _, _, stage_ref, cast_ref, _ = self.refs
        slot = self.step % 2
        _cast_w_in_rows(lambda lo, hi: stage_ref[slot, 0:self.rows, lo:hi], cast_ref.at[slot],
                        slice(None))

    def end_step(self):
        step = self.step
        self._write_back(step).start()

        @pl.when(step == self.n_steps - 1)
        def _():
            self._write_back(step - 1).wait()
            self._write_back(step).wait()


def _prepare_small_weights(w_s_ref, b_s_ref, w_pool_ref, w_sp_ref, b_sp_ref, w_pool_bd_ref):
    f32 = jnp.float32
    bf16 = jnp.bfloat16
    row = lax.broadcasted_iota(jnp.int32, (CHUNK, CHUNK), 0)
    col = lax.broadcasted_iota(jnp.int32, (CHUNK, CHUNK), 1)
    causal = row >= col
    first_group = col < A_HEAD
    b_t = b_s_ref[...].T
    for jb in range(A_GROUPS // 2):
        w_pair = [jnp.where(causal, w_s_ref[2 * jb + k], 0.0) for k in range(2)]
        w_sp_ref[jb] = jnp.concatenate(w_pair, axis=1).astype(bf16)
        b_pair = [jnp.broadcast_to(b_t[:, 2 * jb + k:2 * jb + k + 1], (CHUNK, LANES))
                  for k in range(2)]
        b_sp_ref[:, jb * LANES:(jb + 1) * LANES] = jnp.where(first_group, b_pair[0], b_pair[1])
    zero = jnp.zeros((C_GROUP, C_GROUP), f32)
    for i in range(len(POOL_WINDOWS) // 2):
        top = jnp.concatenate([w_pool_ref[2 * i], zero], axis=1)
        bottom = jnp.concatenate([zero, w_pool_ref[2 * i + 1]], axis=1)
        w_pool_bd_ref[i] = jnp.concatenate([top, bottom], axis=0).astype(bf16)


def _layer_kernel(*refs, names, layer, steps_per_seq, n_steps, apply_final_norm):
    f32 = jnp.float32
    bf16 = jnp.bfloat16
    r = dict(zip(names, refs, strict=True))
    x_ref, x_next_ref, out_ref = r["x"], r["x_next"], r["out"]
    w_in_ref, sem_ref = r["w_in"], r["sem"]
    w_pa_ref, w_pb_ref, w_pc_ref, w_o_ref = r["w_pa"], r["w_pb"], r["w_pc"], r["w_o"]
    w_sp_ref, b_sp_ref, w_pool_bd_ref = r["w_sp"], r["b_sp"], r["w_pool_bd"]
    h_ref, puv_ref, cx_ref, xc_ref = r["h"], r["puv"], r["cx"], r["xc"]
    norm_g_ref, ln_g_ref, ln_b_ref = r["norm_g"], r["ln_g"], r["ln_b"]
    conv_w_ref, conv_b_ref, pool_scale_ref = r["conv_w"], r["conv_b"], r["pool_scale"]
    final_g_ref = r["final_g"]
    j = pl.program_id(1)
    step = pl.program_id(0) * steps_per_seq + j
    next_w_in = None
    if "w_in_next" in r:
        next_w_in = _NextLayerWIn(r["w_in_f32_hbm"], r["w_in_next"], r["stage"], r["cast"],
                                  r["next_sem"], layer, step, n_steps)

    this_layer = slice(layer, layer + 1)
    norm_g, final_g = norm_g_ref[this_layer, :], final_g_ref[...]
    ln_g, ln_b = ln_g_ref[this_layer, :], ln_b_ref[this_layer, :]
    conv_b, pool_scale = conv_b_ref[this_layer, :], pool_scale_ref[this_layer, :]
    conv_w = [conv_w_ref[layer, k:k + 1, :] for k in range(CONV_WIDTH)]

    @pl.when(j == 0)
    def _():
        cx_ref[0:HALO, :] = jnp.zeros((HALO, B_WIDTH), f32)
        xc_ref[0:HALO, :] = jnp.zeros((HALO, C_WIDTH), f32)

    def proj(seg):
        return jnp.dot(h_ref[...], w_in_ref[:, seg[0]:seg[1]], preferred_element_type=f32)

    @pl.when(step == 0)
    def _():
        if "w_in_bf16_hbm" in r:
            ready_w_in = pltpu.make_async_copy(r["w_in_bf16_hbm"], w_in_ref, sem_ref.at[2])
            ready_w_in.start()
        else:
            _load_w_in(r["w_in_f32_hbm"], w_in_ref, r["stage"], sem_ref, layer)
        _load_out_weights((r["w_pa_hbm"], r["w_pb_hbm"], r["w_pc_hbm"], r["w_o_hbm"]),
                          (w_pa_ref, w_pb_ref, w_pc_ref, w_o_ref), puv_ref, sem_ref, layer)
        _prepare_small_weights(r["w_s"], r["b_s"], r["w_pool"], w_sp_ref, b_sp_ref,
                               w_pool_bd_ref)
        h_ref[...] = _rmsnorm_rows(x_ref[0], norm_g).astype(bf16)
        if "w_in_bf16_hbm" in r:
            ready_w_in.wait()
        puv_ref[...] = proj(SEG_UV)
        if next_w_in is not None:
            next_w_in.start_first_fetch()

    if next_w_in is not None:
        next_w_in.begin_step()
        next_w_in.cast()

    n_chunks = TS // CHUNK
    lane = lax.broadcasted_iota(jnp.int32, (CHUNK, LANES), 1)
    first_group = lane < A_HEAD

    def spatial_mix(vn):
        sg_blocks = []
        for jb in range(A_WIDTH // LANES):
            rhs = []
            for c in range(n_chunks):
                vb = vn[c * CHUNK:(c + 1) * CHUNK, jb * LANES:(jb + 1) * LANES]
                rhs.append(jnp.concatenate([jnp.where(first_group, vb, 0.0),
                                            jnp.where(first_group, 0.0, vb)], axis=0))
            rhs = jnp.concatenate(rhs, axis=1).astype(bf16)
            mixed = jnp.dot(w_sp_ref[jb], rhs, preferred_element_type=f32)
            sg_blocks.append(jnp.concatenate(
                [mixed[:, c * LANES:(c + 1) * LANES] for c in range(n_chunks)], axis=0))
        sg = jnp.concatenate(sg_blocks, axis=1)
        return sg + jnp.concatenate([b_sp_ref[...]] * n_chunks, axis=0)

    def layer_norm(v):
        mu = jnp.mean(v, axis=-1, keepdims=True)
        vc = v - mu
        var = jnp.mean(vc * vc, axis=-1, keepdims=True)
        return vc * lax.rsqrt(var + LN_EPS) * ln_g + ln_b

    def store_xc(e):
        xc_ref[HALO:HALO + TS, :] = proj(SEG_XC)

    def store_cx(e):
        cx_ref[HALO:HALO + TS, :] = e["cg"] * e["xb"]

    def short_conv(e):
        conv = conv_b + conv_w[CONV_WIDTH - 1] * cx_ref[HALO:HALO + TS, :]
        for k in range(CONV_WIDTH - 1):
            back = CONV_WIDTH - 1 - k
            conv = conv + conv_w[k] * cx_ref[HALO - back:HALO - back + TS, :]
        cx_ref[0:HALO, :] = cx_ref[TS:TS + HALO, :]
        return conv

    def pooled_pairs(e):
        t1 = (j * TS + 1 + lax.broadcasted_iota(jnp.int32, (TS, C_GROUP), 0)).astype(f32)
        inv_t1 = 1.0 / t1
        pooled = []
        for gi, w in enumerate(POOL_WINDOWS):
            cols = slice(gi * C_GROUP, (gi + 1) * C_GROUP)
            ext = xc_ref[:, cols]
            win = ext
            span = 1
            while span < w:
                win = win + pltpu.roll(win, span, axis=0)
                span *= 2
            pooled.append(win[HALO:, :] * jnp.maximum(inv_t1, 1.0 / w) - ext[HALO:, :])
        xc_ref[0:HALO, :] = xc_ref[TS:TS + HALO, :]
        return [jnp.concatenate(pooled[2 * half:2 * half + 2], axis=1).astype(bf16)
                for half in range(2)]

    def pool_dots(e):
        groups = [jnp.dot(lhs, w_pool_bd_ref[half], preferred_element_type=f32)
                  for half, lhs in enumerate(e["pooled"])]
        return jnp.concatenate(groups, axis=1) * pool_scale

    def next_tile_u(e):
        h_ref[...] = _rmsnorm_rows(x_next_ref[0], norm_g).astype(bf16)
        puv_ref[:, 0:A_WIDTH] = proj(SEG_U)

    def next_tile_v(e):
        puv_ref[:, A_WIDTH:] = proj(SEG_V)

    def write_out(e):
        y = x_ref[0] + e["wo"]
        if apply_final_norm:
            y = _rmsnorm_rows(y, final_g)
        out_ref[0] = y

    def bdot(lhs, w_ref):
        return jnp.dot(lhs, w_ref[...], preferred_element_type=f32)

    matmul_jobs = {
        "xc": ((), store_xc),
        "za": ((), lambda e: proj(SEG_ZA)),
        "cg": ((), lambda e: proj(SEG_CG)),
        "xb": ((), lambda e: proj(SEG_XB)),
        "bg": ((), lambda e: proj(SEG_BG)),
        "zb": ((), lambda e: proj(SEG_ZB)),
        "zc": ((), lambda e: proj(SEG_ZC)),
        "ga": ((), lambda e: proj(SEG_GA)),
        "gb": ((), lambda e: proj(SEG_GB)),
        "gc": ((), lambda e: proj(SEG_GC)),
        "sp": (("vn",), lambda e: spatial_mix(e["vn"])),
        "pa": (("ya",), lambda e: bdot(e["ya"], w_pa_ref)),
        "pb": (("yb",), lambda e: bdot(e["yb"], w_pb_ref)),
        "po": (("pooled",), pool_dots),
        "pc": (("yc",), lambda e: bdot(e["yc"], w_pc_ref)),
        "u_next": (PROJ_JOBS, next_tile_u),
        "wo": (("merged_c",), lambda e: bdot(e["merged_c"].astype(bf16), w_o_ref)),
        "v_next": (("u_next",), next_tile_v),
    }
    elementwise = (
        ("u", (), lambda e: _gelu_of_half(puv_ref[:, 0:A_WIDTH])),
        ("vn", (), lambda e: layer_norm(_gelu_of_half(puv_ref[:, A_WIDTH:]))),
        ("cx", ("cg", "xb"), store_cx),
        ("ya", ("u", "sp", "za"),
         lambda e: (e["u"] * e["sp"] * _silu_of_half(e["za"])).astype(bf16)),
        ("conv", ("cx",), short_conv),
        ("yb", ("bg", "conv", "zb"),
         lambda e: (e["bg"] * e["conv"] * _silu_of_half(e["zb"])).astype(bf16)),
        ("pooled", ("xc",), pooled_pairs),
        ("yc", ("po", "zc"), lambda e: (e["po"] * _silu_of_half(e["zc"])).astype(bf16)),
        ("merged_a", ("ga", "pa"), lambda e: _twice_sigmoid_of_half(e["ga"]) * e["pa"]),
        ("merged_b", ("merged_a", "gb", "pb"),
         lambda e: e["merged_a"] + _twice_sigmoid_of_half(e["gb"]) * e["pb"]),
        ("gate_c", ("gc",), lambda e: _twice_sigmoid_of_half(e["gc"])),
        ("merged_c", ("merged_b", "gate_c", "pc"),
         lambda e: e["merged_b"] + e["gate_c"] * e["pc"]),
        ("out", ("wo",), write_out),
    )
    env = {}

    def trace_ready_elementwise():
        progress = True
        while progress:
            progress = False
            for name, needs, fn in elementwise:
                if name not in env and all(n in env for n in needs):
                    env[name] = fn(env)
                    progress = True

    assert sorted(MXU_ORDER) == sorted(matmul_jobs)
    for name in MXU_ORDER:
        trace_ready_elementwise()
        needs, fn = matmul_jobs[name]
        assert all(n in env for n in needs), (name, needs)
        env[name] = fn(env)
    trace_ready_elementwise()
    assert len(env) == len(matmul_jobs) + len(elementwise)
    if next_w_in is not None:
        next_w_in.end_step()


def _whole_spec(array):
    zeros = (0,) * array.ndim
    return pl.BlockSpec(array.shape, lambda b, j: zeros, pipeline_mode=pl.Buffered(1))


def _layer_spec(array, layer):
    zeros = (0,) * (array.ndim - 1)
    return pl.BlockSpec((None,) + array.shape[1:], lambda b, j: (layer,) + zeros,
                        pipeline_mode=pl.Buffered(1))


VECTOR_NAMES = ("norm_g", "ln_g", "ln_b", "conv_w", "conv_b", "pool_scale", "final_g")
PER_LAYER_NAMES = ("w_s", "b_s", "w_pool")
OUT_WEIGHT_NAMES = ("w_pa", "w_pb", "w_pc", "w_o")


def _layer(x, w_in, w_in_bf16, vectors, per_layer, out_weights, layer, *, convert_next,
           apply_final_norm):
    bsz, seq, _ = x.shape
    assert seq % TS == 0 and TS % CHUNK == 0 and TS >= HALO
    assert w_in.dtype == jnp.float32 and w_in.shape[1:] == (D_MODEL, IN_TOTAL)
    assert D_MODEL % W_ROWS == 0 and TS == 2 * O_ROWS and N_STAGE >= 3
    steps_per_seq = seq // TS
    n_steps = bsz * steps_per_seq
    bf16 = jnp.bfloat16
    f32 = jnp.float32

    def next_tile(b, j):
        s = jnp.minimum(b * steps_per_seq + j + 1, n_steps - 1)
        return (s // steps_per_seq, s % steps_per_seq, 0)

    tile = (1, TS, D_MODEL)
    any_spec = pl.BlockSpec(memory_space=pl.ANY)
    inputs = [("x", x, pl.BlockSpec(tile, lambda b, j: (b, j, 0))),
              ("x_next", x, pl.BlockSpec(tile, next_tile)),
              ("w_in_f32_hbm", w_in, any_spec)]
    if w_in_bf16 is not None:
        inputs.append(("w_in_bf16_hbm", w_in_bf16, any_spec))
    inputs += [(n, a, _whole_spec(a)) for n, a in zip(VECTOR_NAMES, vectors, strict=True)]
    inputs += [(n, a, _layer_spec(a, layer))
               for n, a in zip(PER_LAYER_NAMES, per_layer, strict=True)]
    inputs += [(n + "_hbm", a, any_spec)
               for n, a in zip(OUT_WEIGHT_NAMES, out_weights, strict=True)]

    outputs = [("out", jax.ShapeDtypeStruct(x.shape, x.dtype),
                pl.BlockSpec(tile, lambda b, j: (b, j, 0)))]
    scratch = [
        ("w_in", pltpu.VMEM((D_MODEL, IN_TOTAL), bf16)),
        ("sem", pltpu.SemaphoreType.DMA((N_STAGE,))),
        ("w_pa", pltpu.VMEM((A_WIDTH, D_MODEL), bf16)),
        ("w_pb", pltpu.VMEM((B_WIDTH, D_MODEL), bf16)),
        ("w_pc", pltpu.VMEM((C_WIDTH, D_MODEL), bf16)),
        ("w_o", pltpu.VMEM((D_MODEL, D_MODEL), bf16)),
        ("w_sp", pltpu.VMEM((A_GROUPS // 2, CHUNK, 2 * CHUNK), bf16)),
        ("b_sp", pltpu.VMEM((CHUNK, A_WIDTH), f32)),
        ("w_pool_bd", pltpu.VMEM((len(POOL_WINDOWS) // 2, 2 * C_GROUP, 2 * C_GROUP), bf16)),
        ("h", pltpu.VMEM((TS, D_MODEL), bf16)),
        ("puv", pltpu.VMEM((TS, 2 * A_WIDTH), f32)),
        ("cx", pltpu.VMEM((TS + HALO, B_WIDTH), f32)),
        ("xc", pltpu.VMEM((TS + HALO, C_WIDTH), f32)),
    ]
    if w_in_bf16 is None or convert_next:
        scratch.append(("stage", pltpu.VMEM((N_STAGE, W_ROWS, IN_TOTAL), f32)))
    if convert_next:
        assert D_MODEL % n_steps == 0 and (D_MODEL // n_steps) % 16 == 0
        assert D_MODEL // n_steps <= W_ROWS and n_steps >= 2
        outputs.append(("w_in_next", jax.ShapeDtypeStruct((D_MODEL, IN_TOTAL), bf16), any_spec))
        scratch += [("cast", pltpu.VMEM((2, D_MODEL // n_steps, IN_TOTAL), bf16)),
                    ("next_sem", pltpu.SemaphoreType.DMA((4,)))]

    names = tuple(n for n, *_ in inputs + outputs + scratch)
    results = pl.pallas_call(
        functools.partial(_layer_kernel, names=names, layer=layer, steps_per_seq=steps_per_seq,
                          n_steps=n_steps, apply_final_norm=apply_final_norm),
        out_shape=[s for _, s, _ in outputs],
        grid=(bsz, steps_per_seq),
        in_specs=[spec for _, _, spec in inputs],
        out_specs=[spec for _, _, spec in outputs],
        scratch_shapes=[s for _, s in scratch],
        compiler_params=pltpu.CompilerParams(
            dimension_semantics=("arbitrary", "arbitrary"),
            vmem_limit_bytes=VMEM_LIMIT_BYTES,
        ),
        name="hybrid_layer_final" if apply_final_norm else "hybrid_layer",
    )(*[a for _, a, _ in inputs])
    return results[0], (results[1] if convert_next else None)


def kernel(x, norm_g, w_in, ln_g, ln_b, w_s, b_s, conv_w, conv_b, w_pool, pool_scale,
           w_pa, w_pb, w_pc, w_o, final_g):
    vectors = (norm_g, ln_g, ln_b, conv_w, conv_b, pool_scale, final_g[None, :])
    per_layer = (w_s, b_s, w_pool)
    out_weights = (w_pa, w_pb, w_pc, w_o)
    w_in_bf16 = None
    for layer in range(DEPTH):
        last = layer == DEPTH - 1
        x, w_in_bf16 = _layer(x, w_in, w_in_bf16, vectors, per_layer, out_weights, layer,
                              convert_next=not last, apply_final_norm=last)
    return x
```

```python
import functools
import math

import jax
import jax.numpy as jnp
from jax import lax
from jax.experimental import pallas as pl
from jax.experimental.pallas import tpu as pltpu

D_MODEL = 1024
DEPTH = 2
CHUNK = 128
A_GROUPS = 8
A_WIDTH = 512
A_HEAD = A_WIDTH // A_GROUPS
B_WIDTH = 512
CONV_WIDTH = 3
C_WIDTH = 512
POOL_WINDOWS = (2, 4, 8, 16)
C_GROUP = C_WIDTH // len(POOL_WINDOWS)
IN_TOTAL = 3 * A_WIDTH + 4 * B_WIDTH + 2 * C_WIDTH + 3 * D_MODEL
RMS_EPS = 1e-6
LN_EPS = 1e-5

_WIDTHS = [A_WIDTH] * 3 + [B_WIDTH] * 4 + [C_WIDTH] * 2 + [D_MODEL] * 3
_OFFS = [sum(_WIDTHS[:i]) for i in range(len(_WIDTHS) + 1)]
PROJ_SEGS = tuple((_OFFS[i], _OFFS[i + 1]) for i in range(len(_WIDTHS)))
(SEG_U, SEG_V, SEG_ZA, SEG_XB, SEG_BG, SEG_CG, SEG_ZB, SEG_XC, SEG_ZC,
 SEG_GA, SEG_GB, SEG_GC) = PROJ_SEGS
SEG_UV = (SEG_U[0], SEG_V[1])
GATE_JOBS = tuple(g + str(half) for g in ("ga", "gb", "gc") for half in range(2))
PROJ_JOBS = ("xc", "za", "cg", "xb", "bg", "zb", "zc") + GATE_JOBS
SPATIAL_JOBS = tuple("sp%d" % jb for jb in range(A_GROUPS // 2))
MXU_ORDER = ("xb", "sp1", "sp0", "sp3", "bg", "gb0", "za", "cg", "xc", "zb", "sp2", "pa", "po",
             "gb1", "ga0", "ga1", "pb", "gc0", "zc", "gc1", "u_next", "pc", "wo", "v_next")

LANES = 128
HALO = 16
TS = 512
W_ROWS = 64
N_STAGE = 3
O_ROWS = 256
VMEM_LIMIT_BYTES = 56 * 1024 * 1024


HALVED_SEGS = (SEG_U, SEG_V, SEG_ZA, SEG_ZB, SEG_ZC, SEG_GA, SEG_GB, SEG_GC)
_GELU_C1 = 2.0 * math.sqrt(2.0 / math.pi)
_GELU_C3 = 8.0 * 0.044715 * math.sqrt(2.0 / math.pi)


def _gelu_of_half(xh):
    t = jnp.tanh(xh * (_GELU_C1 + _GELU_C3 * (xh * xh)))
    return xh + xh * t


def _silu_of_half(zh):
    return zh + zh * jnp.tanh(zh)


def _twice_sigmoid_of_half(gh):
    return jnp.tanh(gh) + 1.0


def _rmsnorm_rows(x, gain):
    ms = jnp.mean(x * x, axis=-1, keepdims=True)
    return x * lax.rsqrt(ms + RMS_EPS) * gain


def _cast_w_in_rows(src, dst_ref, dst_rows):
    for lo, hi in PROJ_SEGS:
        w = src(lo, hi)
        if (lo, hi) in HALVED_SEGS:
            w = w * 0.5
        dst_ref[dst_rows, lo:hi] = w.astype(jnp.bfloat16)


def _w_in_copy(w_in_hbm_ref, stage_ref, sem_ref, layer, i):
    return pltpu.make_async_copy(
        w_in_hbm_ref.at[layer, pl.ds(i * W_ROWS, W_ROWS), :],
        stage_ref.at[i % N_STAGE], sem_ref.at[i % N_STAGE])


def _load_w_in(w_in_hbm_ref, w_in_ref, stage_ref, sem_ref, layer):
    n_blocks = D_MODEL // W_ROWS
    for i in range(N_STAGE):
        _w_in_copy(w_in_hbm_ref, stage_ref, sem_ref, layer, i).start()

    def convert_block(i, carry):
        _w_in_copy(w_in_hbm_ref, stage_ref, sem_ref, layer, i).wait()
        rows = pl.ds(pl.multiple_of(i * W_ROWS, W_ROWS), W_ROWS)
        _cast_w_in_rows(lambda lo, hi: stage_ref[i % N_STAGE, :, lo:hi], w_in_ref, rows)

        @pl.when(i + N_STAGE < n_blocks)
        def _():
            _w_in_copy(w_in_hbm_ref, stage_ref, sem_ref, layer, i + N_STAGE).start()

        return carry

    lax.fori_loop(0, n_blocks, convert_block, 0)


def _load_out_weights(hbm_refs, vmem_refs, stage_ref, sem_ref, layer):
    blocks = [(src, dst, r) for src, dst in zip(hbm_refs, vmem_refs)
              for r in range(0, dst.shape[0], O_ROWS)]

    def copy(i):
        src, _, r = blocks[i]
        return pltpu.make_async_copy(src.at[layer, pl.ds(r, O_ROWS), :],
                                     stage_ref.at[pl.ds((i % 2) * O_ROWS, O_ROWS), :],
                                     sem_ref.at[i % 2])

    copy(0).start()
    copy(1).start()
    for i, (_, dst, r) in enumerate(blocks):
        copy(i).wait()
        w = stage_ref[(i % 2) * O_ROWS:(i % 2 + 1) * O_ROWS, :]
        if dst is vmem_refs[-1]:
            w = w * 0.5
        dst[r:r + O_ROWS, :] = w.astype(jnp.bfloat16)
        if i + 2 < len(blocks):
            copy(i + 2).start()


class _NextLayerWIn:
    def __init__(self, w_in_hbm_ref, w_next_hbm_ref, stage_ref, cast_ref, sem_ref, layer,
                 step, n_steps):
        self.refs = (w_in_hbm_ref, w_next_hbm_ref, stage_ref, cast_ref, sem_ref)
        self.layer, self.step, self.n_steps = layer, step, n_steps
        self.rows = D_MODEL // n_steps

    def _rows_of(self, block):
        return pl.ds(pl.multiple_of(block * self.rows, self.rows), self.rows)

    def _fetch(self, block):
        w_in_hbm_ref, _, stage_ref, _, sem_ref = self.refs
        return pltpu.make_async_copy(w_in_hbm_ref.at[self.layer + 1, self._rows_of(block), :],
                                     stage_ref.at[block % 2, pl.ds(0, self.rows), :],
                                     sem_ref.at[block % 2])

    def _write_back(self, block):
        _, w_next_hbm_ref, _, cast_ref, sem_ref = self.refs
        return pltpu.make_async_copy(cast_ref.at[block % 2],
                                     w_next_hbm_ref.at[self._rows_of(block), :],
                                     sem_ref.at[2 + block % 2])

    def start_first_fetch(self):
        self._fetch(0).start()

    def begin_step(self):
        step = self.step
        self._fetch(step).wait()

        @pl.when(step + 1 < self.n_steps)
        def _():
            self._fetch(step + 1).start()

        @pl.when(step >= 2)
        def _():
            self._write_back(step - 2).wait()

    def cast(self):
        _, _, stage_ref, cast_ref, _ = self.refs
        slot = self.step % 2
        _cast_w_in_rows(lambda lo, hi: stage_ref[slot, 0:self.rows, lo:hi], cast_ref.at[slot],
                        slice(None))

    def end_step(self):
        step = self.step
        self._write_back(step).start()

        @pl.when(step == self.n_steps - 1)
        def _():
            self._write_back(step - 1).wait()
            self._write_back(step).wait()


def _prepare_small_weights(w_s_ref, b_s_ref, w_pool_ref, w_sp_ref, b_sp_ref, w_pool_bd_ref):
    f32 = jnp.float32
    bf16 = jnp.bfloat16
    row = lax.broadcasted_iota(jnp.int32, (CHUNK, CHUNK), 0)
    col = lax.broadcasted_iota(jnp.int32, (CHUNK, CHUNK), 1)
    causal = row >= col
    first_group = col < A_HEAD
    b_t = b_s_ref[...].T
    for jb in range(A_GROUPS // 2):
        w_pair = [jnp.where(causal, w_s_ref[2 * jb + k], 0.0) for k in range(2)]
        w_sp_ref[jb] = jnp.concatenate(w_pair, axis=1).astype(bf16)
        b_pair = [jnp.broadcast_to(b_t[:, 2 * jb + k:2 * jb + k + 1], (CHUNK, LANES))
                  for k in range(2)]
        b_sp_ref[:, jb * LANES:(jb + 1) * LANES] = jnp.where(first_group, b_pair[0], b_pair[1])
    zero = jnp.zeros((C_GROUP, C_GROUP), f32)
    for i in range(len(POOL_WINDOWS) // 2):
        top = jnp.concatenate([w_pool_ref[2 * i], zero], axis=1)
        bottom = jnp.concatenate([zero, w_pool_ref[2 * i + 1]], axis=1)
        w_pool_bd_ref[i] = jnp.concatenate([top, bottom], axis=0).astype(bf16)


def _layer_kernel(*refs, names, layer, steps_per_seq, n_steps, apply_final_norm):
    f32 = jnp.float32
    bf16 = jnp.bfloat16
    r = dict(zip(names, refs, strict=True))
    x_ref, x_next_ref, out_ref = r["x"], r["x_next"], r["out"]
    w_in_ref, sem_ref = r["w_in"], r["sem"]
    w_pa_ref, w_pb_ref, w_pc_ref, w_o_ref = r["w_pa"], r["w_pb"], r["w_pc"], r["w_o"]
    w_sp_ref, b_sp_ref, w_pool_bd_ref = r["w_sp"], r["b_sp"], r["w_pool_bd"]
    h_ref, puv_ref, cx_ref, xc_ref = r["h"], r["puv"], r["cx"], r["xc"]
    norm_g_ref, ln_g_ref, ln_b_ref = r["norm_g"], r["ln_g"], r["ln_b"]
    conv_w_ref, conv_b_ref, pool_scale_ref = r["conv_w"], r["conv_b"], r["pool_scale"]
    final_g_ref = r["final_g"]
    j = pl.program_id(1)
    step = pl.program_id(0) * steps_per_seq + j
    next_w_in = None
    if "w_in_next" in r:
        next_w_in = _NextLayerWIn(r["w_in_f32_hbm"], r["w_in_next"], r["stage"], r["cast"],
                                  r["next_sem"], layer, step, n_steps)

    this_layer = slice(layer, layer + 1)
    norm_g, final_g = norm_g_ref[this_layer, :], final_g_ref[...]
    ln_g, ln_b = ln_g_ref[this_layer, :], ln_b_ref[this_layer, :]
    conv_b, pool_scale = conv_b_ref[this_layer, :], pool_scale_ref[this_layer, :]
    conv_w = [conv_w_ref[layer, k:k + 1, :] for k in range(CONV_WIDTH)]

    @pl.when(j == 0)
    def _():
        cx_ref[0:HALO, :] = jnp.zeros((HALO, B_WIDTH), f32)
        xc_ref[0:HALO, :] = jnp.zeros((HALO, C_WIDTH), f32)

    def proj(seg):
        return jnp.dot(h_ref[...], w_in_ref[:, seg[0]:seg[1]], preferred_element_type=f32)

    @pl.when(step == 0)
    def _():
        if "w_in_bf16_hbm" in r:
            ready_w_in = pltpu.make_async_copy(r["w_in_bf16_hbm"], w_in_ref, sem_ref.at[2])
            ready_w_in.start()
        else:
            _load_w_in(r["w_in_f32_hbm"], w_in_ref, r["stage"], sem_ref, layer)
        _load_out_weights((r["w_pa_hbm"], r["w_pb_hbm"], r["w_pc_hbm"], r["w_o_hbm"]),
                          (w_pa_ref, w_pb_ref, w_pc_ref, w_o_ref), puv_ref, sem_ref, layer)
        _prepare_small_weights(r["w_s"], r["b_s"], r["w_pool"], w_sp_ref, b_sp_ref,
                               w_pool_bd_ref)
        h_ref[...] = _rmsnorm_rows(x_ref[0], norm_g).astype(bf16)
        if "w_in_bf16_hbm" in r:
            ready_w_in.wait()
        puv_ref[...] = proj(SEG_UV)
        if next_w_in is not None:
            next_w_in.start_first_fetch()

    if next_w_in is not None:
        next_w_in.begin_step()
        next_w_in.cast()

    n_chunks = TS // CHUNK
    lane = lax.broadcasted_iota(jnp.int32, (CHUNK, LANES), 1)
    first_group = lane < A_HEAD

    def spatial_mix(vn, jb):
        rhs = []
        for c in range(n_chunks):
            vb = vn[c * CHUNK:(c + 1) * CHUNK, jb * LANES:(jb + 1) * LANES]
            rhs.append(jnp.concatenate([jnp.where(first_group, vb, 0.0),
                                        jnp.where(first_group, 0.0, vb)], axis=0))
        rhs = jnp.concatenate(rhs, axis=1).astype(bf16)
        mixed = jnp.dot(w_sp_ref[jb], rhs, preferred_element_type=f32)
        sg = jnp.concatenate(
            [mixed[:, c * LANES:(c + 1) * LANES] for c in range(n_chunks)], axis=0)
        return sg + jnp.concatenate([b_sp_ref[:, jb * LANES:(jb + 1) * LANES]] * n_chunks, axis=0)

    def layer_norm(v):
        mu = jnp.mean(v, axis=-1, keepdims=True)
        vc = v - mu
        var = jnp.mean(vc * vc, axis=-1, keepdims=True)
        return vc * lax.rsqrt(var + LN_EPS) * ln_g + ln_b

    def store_xc(e):
        xc_ref[HALO:HALO + TS, :] = proj(SEG_XC)

    def store_cx(e):
        cx_ref[HALO:HALO + TS, :] = e["cg"] * e["xb"]

    def short_conv(e):
        conv = conv_b + conv_w[CONV_WIDTH - 1] * cx_ref[HALO:HALO + TS, :]
        for k in range(CONV_WIDTH - 1):
            back = CONV_WIDTH - 1 - k
            conv = conv + conv_w[k] * cx_ref[HALO - back:HALO - back + TS, :]
        cx_ref[0:HALO, :] = cx_ref[TS:TS + HALO, :]
        return conv

    def pooled_pairs(e):
        t1 = (j * TS + 1 + lax.broadcasted_iota(jnp.int32, (TS, C_GROUP), 0)).astype(f32)
        inv_t1 = 1.0 / t1
        pooled = []
        for gi, w in enumerate(POOL_WINDOWS):
            cols = slice(gi * C_GROUP, (gi + 1) * C_GROUP)
            ext = xc_ref[:, cols]
            win = ext
            span = 1
            while span < w:
                win = win + pltpu.roll(win, span, axis=0)
                span *= 2
            pooled.append(win[HALO:, :] * jnp.maximum(inv_t1, 1.0 / w) - ext[HALO:, :])
        xc_ref[0:HALO, :] = xc_ref[TS:TS + HALO, :]
        return [jnp.concatenate(pooled[2 * half:2 * half + 2], axis=1).astype(bf16)
                for half in range(2)]

    def pool_dots(e):
        groups = [jnp.dot(lhs, w_pool_bd_ref[half], preferred_element_type=f32)
                  for half, lhs in enumerate(e["pooled"])]
        return jnp.concatenate(groups, axis=1) * pool_scale

    def next_tile_u(e):
        h_ref[...] = _rmsnorm_rows(x_next_ref[0], norm_g).astype(bf16)
        puv_ref[:, 0:A_WIDTH] = proj(SEG_U)

    def next_tile_v(e):
        puv_ref[:, A_WIDTH:] = proj(SEG_V)

    def write_out(e):
        y = x_ref[0] + e["wo"]
        if apply_final_norm:
            y = _rmsnorm_rows(y, final_g)
        out_ref[0] = y

    def gate_half(seg, half, e):
        mid = (seg[0] + seg[1]) // 2
        return proj((seg[0], mid) if half == 0 else (mid, seg[1]))

    def bdot(lhs, w_ref):
        return jnp.dot(lhs, w_ref[...], preferred_element_type=f32)

    matmul_jobs = {
        "xc": ((), store_xc),
        "za": ((), lambda e: proj(SEG_ZA)),
        "cg": ((), lambda e: proj(SEG_CG)),
        "xb": ((), lambda e: proj(SEG_XB)),
        "bg": ((), lambda e: proj(SEG_BG)),
        "zb": ((), lambda e: proj(SEG_ZB)),
        "zc": ((), lambda e: proj(SEG_ZC)),
        **{g + str(half): ((), functools.partial(gate_half, seg, half))
           for g, seg in (("ga", SEG_GA), ("gb", SEG_GB), ("gc", SEG_GC)) for half in range(2)},
        **{"sp%d" % jb: (("vn",), functools.partial(lambda jb, e: spatial_mix(e["vn"], jb), jb))
           for jb in range(A_WIDTH // LANES)},
        "pa": (("ya",), lambda e: bdot(e["ya"], w_pa_ref)),
        "pb": (("yb",), lambda e: bdot(e["yb"], w_pb_ref)),
        "po": (("pooled",), pool_dots),
        "pc": (("yc",), lambda e: bdot(e["yc"], w_pc_ref)),
        "u_next": (PROJ_JOBS, next_tile_u),
        "wo": (("merged_c0", "merged_c1"), lambda e: bdot(
            jnp.concatenate([e["merged_c0"], e["merged_c1"]], axis=1).astype(bf16), w_o_ref)),
        "v_next": (("u_next",), next_tile_v),
    }
    elementwise = (
        ("u", (), lambda e: _gelu_of_half(puv_ref[:, 0:A_WIDTH])),
        ("vn", (), lambda e: layer_norm(_gelu_of_half(puv_ref[:, A_WIDTH:]))),
        ("cx", ("cg", "xb"), store_cx),
        ("sg", SPATIAL_JOBS, lambda e: jnp.concatenate([e[n] for n in SPATIAL_JOBS], axis=1)),
        ("ya", ("u", "sg", "za"),
         lambda e: (e["u"] * e["sg"] * _silu_of_half(e["za"])).astype(bf16)),
        ("conv", ("cx",), short_conv),
        ("yb", ("bg", "conv", "zb"),
         lambda e: (e["bg"] * e["conv"] * _silu_of_half(e["zb"])).astype(bf16)),
        ("pooled", ("xc",), pooled_pairs),
        ("yc", ("po", "zc"), lambda e: (e["po"] * _silu_of_half(e["zc"])).astype(bf16)),
        ("out", ("wo",), write_out),
    )
    for half in range(2):
        cols = slice(half * D_MODEL // 2, (half + 1) * D_MODEL // 2)
        tag = str(half)
        elementwise += (
            ("merged_a" + tag, ("ga" + tag, "pa"),
             lambda e, t=tag, c=cols: _twice_sigmoid_of_half(e["ga" + t]) * e["pa"][:, c]),
            ("merged_b" + tag, ("merged_a" + tag, "gb" + tag, "pb"),
             lambda e, t=tag, c=cols: (e["merged_a" + t]
                                       + _twice_sigmoid_of_half(e["gb" + t]) * e["pb"][:, c])),
            ("gate_c" + tag, ("gc" + tag,), lambda e, t=tag: _twice_sigmoid_of_half(e["gc" + t])),
            ("merged_c" + tag, ("merged_b" + tag, "gate_c" + tag, "pc"),
             lambda e, t=tag, c=cols: e["merged_b" + t] + e["gate_c" + t] * e["pc"][:, c]),
        )
    env = {}

    def trace_ready_elementwise():
        progress = True
        while progress:
            progress = False
            for name, needs, fn in elementwise:
                if name not in env and all(n in env for n in needs):
                    env[name] = fn(env)
                    progress = True

    assert sorted(MXU_ORDER) == sorted(matmul_jobs)
    for name in MXU_ORDER:
        trace_ready_elementwise()
        needs, fn = matmul_jobs[name]
        assert all(n in env for n in needs), (name, needs)
        env[name] = fn(env)
    trace_ready_elementwise()
    assert len(env) == len(matmul_jobs) + len(elementwise)
    if next_w_in is not None:
        next_w_in.end_step()


def _whole_spec(array):
    zeros = (0,) * array.ndim
    return pl.BlockSpec(array.shape, lambda b, j: zeros, pipeline_mode=pl.Buffered(1))


def _layer_spec(array, layer):
    zeros = (0,) * (array.ndim - 1)
    return pl.BlockSpec((None,) + array.shape[1:], lambda b, j: (layer,) + zeros,
                        pipeline_mode=pl.Buffered(1))


VECTOR_NAMES = ("norm_g", "ln_g", "ln_b", "conv_w", "conv_b", "pool_scale", "final_g")
PER_LAYER_NAMES = ("w_s", "b_s", "w_pool")
OUT_WEIGHT_NAMES = ("w_pa", "w_pb", "w_pc", "w_o")


def _layer(x, w_in, w_in_bf16, vectors, per_layer, out_weights, layer, *, convert_next,
           apply_final_norm):
    bsz, seq, _ = x.shape
    assert seq % TS == 0 and TS % CHUNK == 0 and TS >= HALO
    assert w_in.dtype == jnp.float32 and w_in.shape[1:] == (D_MODEL, IN_TOTAL)
    assert D_MODEL % W_ROWS == 0 and TS == 2 * O_ROWS and N_STAGE >= 3
    steps_per_seq = seq // TS
    n_steps = bsz * steps_per_seq
    bf16 = jnp.bfloat16
    f32 = jnp.float32

    def next_tile(b, j):
        s = jnp.minimum(b * steps_per_seq + j + 1, n_steps - 1)
        return (s // steps_per_seq, s % steps_per_seq, 0)

    tile = (1, TS, D_MODEL)
    any_spec = pl.BlockSpec(memory_space=pl.ANY)
    inputs = [("x", x, pl.BlockSpec(tile, lambda b, j: (b, j, 0))),
              ("x_next", x, pl.BlockSpec(tile, next_tile)),
              ("w_in_f32_hbm", w_in, any_spec)]
    if w_in_bf16 is not None:
        inputs.append(("w_in_bf16_hbm", w_in_bf16, any_spec))
    inputs += [(n, a, _whole_spec(a)) for n, a in zip(VECTOR_NAMES, vectors, strict=True)]
    inputs += [(n, a, _layer_spec(a, layer))
               for n, a in zip(PER_LAYER_NAMES, per_layer, strict=True)]
    inputs += [(n + "_hbm", a, any_spec)
               for n, a in zip(OUT_WEIGHT_NAMES, out_weights, strict=True)]

    outputs = [("out", jax.ShapeDtypeStruct(x.shape, x.dtype),
                pl.BlockSpec(tile, lambda b, j: (b, j, 0)))]
    scratch = [
        ("w_in", pltpu.VMEM((D_MODEL, IN_TOTAL), bf16)),
        ("sem", pltpu.SemaphoreType.DMA((N_STAGE,))),
        ("w_pa", pltpu.VMEM((A_WIDTH, D_MODEL), bf16)),
        ("w_pb", pltpu.VMEM((B_WIDTH, D_MODEL), bf16)),
        ("w_pc", pltpu.VMEM((C_WIDTH, D_MODEL), bf16)),
        ("w_o", pltpu.VMEM((D_MODEL, D_MODEL), bf16)),
        ("w_sp", pltpu.VMEM((A_GROUPS // 2, CHUNK, 2 * CHUNK), bf16)),
        ("b_sp", pltpu.VMEM((CHUNK, A_WIDTH), f32)),
        ("w_pool_bd", pltpu.VMEM((len(POOL_WINDOWS) // 2, 2 * C_GROUP, 2 * C_GROUP), bf16)),
        ("h", pltpu.VMEM((TS, D_MODEL), bf16)),
        ("puv", pltpu.VMEM((TS, 2 * A_WIDTH), f32)),
        ("cx", pltpu.VMEM((TS + HALO, B_WIDTH), f32)),
        ("xc", pltpu.VMEM((TS + HALO, C_WIDTH), f32)),
    ]
    if w_in_bf16 is None or convert_next:
        scratch.append(("stage", pltpu.VMEM((N_STAGE, W_ROWS, IN_TOTAL), f32)))
    if convert_next:
        assert D_MODEL % n_steps == 0 and (D_MODEL // n_steps) % 16 == 0
        assert D_MODEL // n_steps <= W_ROWS and n_steps >= 2
        outputs.append(("w_in_next", jax.ShapeDtypeStruct((D_MODEL, IN_TOTAL), bf16), any_spec))
        scratch += [("cast", pltpu.VMEM((2, D_MODEL // n_steps, IN_TOTAL), bf16)),
                    ("next_sem", pltpu.SemaphoreType.DMA((4,)))]

    names = tuple(n for n, *_ in inputs + outputs + scratch)
    results = pl.pallas_call(
        functools.partial(_layer_kernel, names=names, layer=layer, steps_per_seq=steps_per_seq,
                          n_steps=n_steps, apply_final_norm=apply_final_norm),
        out_shape=[s for _, s, _ in outputs],
        grid=(bsz, steps_per_seq),
        in_specs=[spec for _, _, spec in inputs],
        out_specs=[spec for _, _, spec in outputs],
        scratch_shapes=[s for _, s in scratch],
        compiler_params=pltpu.CompilerParams(
            dimension_semantics=("arbitrary", "arbitrary"),
            vmem_limit_bytes=VMEM_LIMIT_BYTES,
        ),
        name="hybrid_layer_final" if apply_final_norm else "hybrid_layer",
    )(*[a for _, a, _ in inputs])
    return results[0], (results[1] if convert_next else None)


def kernel(x, norm_g, w_in, ln_g, ln_b, w_s, b_s, conv_w, conv_b, w_pool, pool_scale,
           w_pa, w_pb, w_pc, w_o, final_g):
    vectors = (norm_g, ln_g, ln_b, conv_w, conv_b, pool_scale, final_g[None, :])
    per_layer = (w_s, b_s, w_pool)
    out_weights = (w_pa, w_pb, w_pc, w_o)
    w_in_bf16 = None
    for layer in range(DEPTH):
        last = layer == DEPTH - 1
        x, w_in_bf16 = _layer(x, w_in, w_in_bf16, vectors, per_layer, out_weights, layer,
                              convert_next=not last, apply_final_norm=last)
    return x
```

```python
import functools
import math

import jax
import jax.numpy as jnp
from jax import lax
from jax.experimental import pallas as pl
from jax.experimental.pallas import tpu as pltpu

D_MODEL = 1024
DEPTH = 2
CHUNK = 128
A_GROUPS = 8
A_WIDTH = 512
A_HEAD = A_WIDTH // A_GROUPS
B_WIDTH = 512
CONV_WIDTH = 3
C_WIDTH = 512
POOL_WINDOWS = (2, 4, 8, 16)
C_GROUP = C_WIDTH // len(POOL_WINDOWS)
IN_TOTAL = 3 * A_WIDTH + 4 * B_WIDTH + 2 * C_WIDTH + 3 * D_MODEL
RMS_EPS = 1e-6
LN_EPS = 1e-5

_WIDTHS = [A_WIDTH] * 3 + [B_WIDTH] * 4 + [C_WIDTH] * 2 + [D_MODEL] * 3
_OFFS = [sum(_WIDTHS[:i]) for i in range(len(_WIDTHS) + 1)]
PROJ_SEGS = tuple((_OFFS[i], _OFFS[i + 1]) for i in range(len(_WIDTHS)))
(SEG_U, SEG_V, SEG_ZA, SEG_XB, SEG_BG, SEG_CG, SEG_ZB, SEG_XC, SEG_ZC,
 SEG_GA, SEG_GB, SEG_GC) = PROJ_SEGS
SEG_UV = (SEG_U[0], SEG_V[1])
PROJ_JOBS = ("xc", "za", "cg", "xb", "bg", "zb", "zc", "ga", "gb", "gc")
MXU_ORDER = ("za", "cg", "xb", "sp", "bg", "gb", "xc", "zb", "pa", "zc", "po", "ga", "pb", "gc",
             "u_next", "pc", "wo", "v_next")

LANES = 128
BF16_SUBLANES = 16
HALO = 16
TS = 512
W_ROWS = 64
N_STAGE = 3
O_ROWS = 256
VMEM_LIMIT_BYTES = 56 * 1024 * 1024


HALVED_SEGS = (SEG_U, SEG_V, SEG_ZA, SEG_ZB, SEG_ZC, SEG_GA, SEG_GB, SEG_GC)
_GELU_C1 = 2.0 * math.sqrt(2.0 / math.pi)
_GELU_C3 = 8.0 * 0.044715 * math.sqrt(2.0 / math.pi)


def _gelu_of_half(xh):
    t = jnp.tanh(xh * (_GELU_C1 + _GELU_C3 * (xh * xh)))
    return xh + xh * t


def _silu_of_half(zh):
    return zh + zh * jnp.tanh(zh)


def _twice_sigmoid_of_half(gh):
    return jnp.tanh(gh) + 1.0


def _rmsnorm_rows(x, gain):
    ms = jnp.mean(x * x, axis=-1, keepdims=True)
    return x * lax.rsqrt(ms + RMS_EPS) * gain


def _cast_w_in_rows(src, dst_ref, dst_rows):
    for lo, hi in PROJ_SEGS:
        w = src(lo, hi)
        if (lo, hi) in HALVED_SEGS:
            w = w * 0.5
        dst_ref[dst_rows, lo:hi] = w.astype(jnp.bfloat16)


def _w_in_copy(w_in_hbm_ref, stage_ref, sem_ref, layer, i):
    return pltpu.make_async_copy(
        w_in_hbm_ref.at[layer, pl.ds(i * W_ROWS, W_ROWS), :],
        stage_ref.at[i % N_STAGE], sem_ref.at[i % N_STAGE])


def _load_w_in(w_in_hbm_ref, w_in_ref, stage_ref, sem_ref, layer):
    n_blocks = D_MODEL // W_ROWS
    for i in range(N_STAGE):
        _w_in_copy(w_in_hbm_ref, stage_ref, sem_ref, layer, i).start()

    def convert_block(i, carry):
        _w_in_copy(w_in_hbm_ref, stage_ref, sem_ref, layer, i).wait()
        rows = pl.ds(pl.multiple_of(i * W_ROWS, W_ROWS), W_ROWS)
        _cast_w_in_rows(lambda lo, hi: stage_ref[i % N_STAGE, :, lo:hi], w_in_ref, rows)

        @pl.when(i + N_STAGE < n_blocks)
        def _():
            _w_in_copy(w_in_hbm_ref, stage_ref, sem_ref, layer, i + N_STAGE).start()

        return carry

    lax.fori_loop(0, n_blocks, convert_block, 0)


def _load_out_weights(hbm_refs, vmem_refs, stage_ref, sem_ref, layer):
    blocks = [(src, dst, r) for src, dst in zip(hbm_refs, vmem_refs)
              for r in range(0, dst.shape[0], O_ROWS)]

    def copy(i):
        src, _, r = blocks[i]
        return pltpu.make_async_copy(src.at[layer, pl.ds(r, O_ROWS), :],
                                     stage_ref.at[pl.ds((i % 2) * O_ROWS, O_ROWS), :],
                                     sem_ref.at[i % 2])

    copy(0).start()
    copy(1).start()
    for i, (_, dst, r) in enumerate(blocks):
        copy(i).wait()
        w = stage_ref[(i % 2) * O_ROWS:(i % 2 + 1) * O_ROWS, :]
        if dst is vmem_refs[-1]:
            w = w * 0.5
        dst[r:r + O_ROWS, :] = w.astype(jnp.bfloat16)
        if i + 2 < len(blocks):
            copy(i + 2).start()


class _NextLayerWIn:
    def __init__(self, w_in_hbm_ref, w_next_hbm_ref, stage_ref, cast_ref, sem_ref, layer,
                 step, n_steps):
        self.refs = (w_in_hbm_ref, w_next_hbm_ref, stage_ref, cast_ref, sem_ref)
        self.layer, self.step, self.n_steps = layer, step, n_steps
        self.rows = D_MODEL // n_steps

    def _rows_of(self, block):
        return pl.ds(pl.multiple_of(block * self.rows, self.rows), self.rows)

    def _fetch(self, block):
        w_in_hbm_ref, _, stage_ref, _, sem_ref = self.refs
        return pltpu.make_async_copy(w_in_hbm_ref.at[self.layer + 1, self._rows_of(block), :],
                                     stage_ref.at[block % 2, pl.ds(0, self.rows), :],
                                     sem_ref.at[block % 2])

    def _write_back(self, block):
        _, w_next_hbm_ref, _, cast_ref, sem_ref = self.refs
        return pltpu.make_async_copy(cast_ref.at[block % 2],
                                     w_next_hbm_ref.at[self._rows_of(block), :],
                                     sem_ref.at[2 + block % 2])

    def start_first_fetch(self):
        self._fetch(0).start()

    def begin_step(self):
        step = self.step
        self._fetch(step).wait()

        @pl.when(step + 1 < self.n_steps)
        def _():
            self._fetch(step + 1).start()

        @pl.when(step >= 2)
        def _():
            self._write_back(step - 2).wait()

    def cast(self):
        _, _, stage_ref, cast_ref, _ = self.refs
        slot = self.step % 2
        _cast_w_in_rows(lambda lo, hi: stage_ref[slot, 0:self.rows, lo:hi], cast_ref.at[slot],
                        slice(None))

    def end_step(self):
        step = self.step
        self._write_back(step).start()

        @pl.when(step == self.n_steps - 1)
        def _():
            self._write_back(step - 1).wait()
            self._write_back(step).wait()


def _prepare_small_weights(w_s_ref, b_s_ref, w_pool_ref, w_sp_ref, b_sp_ref, w_pool_bd_ref):
    f32 = jnp.float32
    bf16 = jnp.bfloat16
    row = lax.broadcasted_iota(jnp.int32, (CHUNK, CHUNK), 0)
    col = lax.broadcasted_iota(jnp.int32, (CHUNK, CHUNK), 1)
    causal = row >= col
    first_group = col < A_HEAD
    b_t = b_s_ref[...].T
    for jb in range(A_GROUPS // 2):
        w_pair = [jnp.where(causal, w_s_ref[2 * jb + k], 0.0) for k in range(2)]
        w_sp_ref[jb] = jnp.concatenate(w_pair, axis=1).astype(bf16)
        b_pair = [jnp.broadcast_to(b_t[:, 2 * jb + k:2 * jb + k + 1], (CHUNK, LANES))
                  for k in range(2)]
        b_sp_ref[:, jb * LANES:(jb + 1) * LANES] = jnp.where(first_group, b_pair[0], b_pair[1])
    zero = jnp.zeros((C_GROUP, C_GROUP), f32)
    for i in range(len(POOL_WINDOWS) // 2):
        top = jnp.concatenate([w_pool_ref[2 * i], zero], axis=1)
        bottom = jnp.concatenate([zero, w_pool_ref[2 * i + 1]], axis=1)
        w_pool_bd_ref[i] = jnp.concatenate([top, bottom], axis=0).astype(bf16)


def _layer_kernel(*refs, names, layer, steps_per_seq, n_steps, apply_final_norm):
    f32 = jnp.float32
    bf16 = jnp.bfloat16
    r = dict(zip(names, refs, strict=True))
    x_ref, x_next_ref, out_ref = r["x"], r["x_next"], r["out"]
    w_in_ref, sem_ref = r["w_in"], r["sem"]
    w_pa_ref, w_pb_ref, w_pc_ref, w_o_ref = r["w_pa"], r["w_pb"], r["w_pc"], r["w_o"]
    w_sp_ref, b_sp_ref, w_pool_bd_ref = r["w_sp"], r["b_sp"], r["w_pool_bd"]
    h_ref, puv_ref, cx_ref, xc_ref = r["h"], r["puv"], r["cx"], r["xc"]
    norm_g_ref, ln_g_ref, ln_b_ref = r["norm_g"], r["ln_g"], r["ln_b"]
    conv_w_ref, conv_b_ref, pool_scale_ref = r["conv_w"], r["conv_b"], r["pool_scale"]
    final_g_ref = r["final_g"]
    j = pl.program_id(1)
    step = pl.program_id(0) * steps_per_seq + j
    next_w_in = None
    if "w_in_next" in r:
        next_w_in = _NextLayerWIn(r["w_in_f32_hbm"], r["w_in_next"], r["stage"], r["cast"],
                                  r["next_sem"], layer, step, n_steps)

    this_layer = slice(layer, layer + 1)
    norm_g, final_g = norm_g_ref[this_layer, :], final_g_ref[...]
    ln_g, ln_b = ln_g_ref[this_layer, :], ln_b_ref[this_layer, :]
    conv_b, pool_scale = conv_b_ref[this_layer, :], pool_scale_ref[this_layer, :]
    conv_w = [conv_w_ref[layer, k:k + 1, :] for k in range(CONV_WIDTH)]

    @pl.when(j == 0)
    def _():
        cx_ref[0:HALO, :] = jnp.zeros((HALO, B_WIDTH), f32)
        xc_ref[0:HALO, :] = jnp.zeros((HALO, C_WIDTH), f32)

    def proj(seg):
        return jnp.dot(h_ref[...], w_in_ref[:, seg[0]:seg[1]], preferred_element_type=f32)

    @pl.when(step == 0)
    def _():
        if "w_in_bf16_hbm" in r:
            ready_w_in = pltpu.make_async_copy(r["w_in_bf16_hbm"], w_in_ref, sem_ref.at[2])
            ready_w_in.start()
        else:
            _load_w_in(r["w_in_f32_hbm"], w_in_ref, r["stage"], sem_ref, layer)
        _load_out_weights((r["w_pa_hbm"], r["w_pb_hbm"], r["w_pc_hbm"], r["w_o_hbm"]),
                          (w_pa_ref, w_pb_ref, w_pc_ref, w_o_ref), puv_ref, sem_ref, layer)
        _prepare_small_weights(r["w_s"], r["b_s"], r["w_pool"], w_sp_ref, b_sp_ref,
                               w_pool_bd_ref)
        h_ref[...] = _rmsnorm_rows(x_ref[0], norm_g).astype(bf16)
        if "w_in_bf16_hbm" in r:
            ready_w_in.wait()
        puv_ref[...] = proj(SEG_UV)
        if next_w_in is not None:
            next_w_in.start_first_fetch()

    if next_w_in is not None:
        next_w_in.begin_step()
        next_w_in.cast()

    n_chunks = TS // CHUNK
    lane = lax.broadcasted_iota(jnp.int32, (CHUNK, LANES), 1)
    first_group = lane < A_HEAD

    def spatial_mix(vn):
        sg_blocks = []
        for jb in range(A_WIDTH // LANES):
            rhs = []
            for c in range(n_chunks):
                vb = vn[c * CHUNK:(c + 1) * CHUNK, jb * LANES:(jb + 1) * LANES]
                rhs.append(jnp.concatenate([jnp.where(first_group, vb, 0.0),
                                            jnp.where(first_group, 0.0, vb)], axis=0))
            rhs = jnp.concatenate(rhs, axis=1).astype(bf16)
            mixed = jnp.dot(w_sp_ref[jb], rhs, preferred_element_type=f32)
            sg_blocks.append(jnp.concatenate(
                [mixed[:, c * LANES:(c + 1) * LANES] for c in range(n_chunks)], axis=0))
        sg = jnp.concatenate(sg_blocks, axis=1)
        return sg + jnp.concatenate([b_sp_ref[...]] * n_chunks, axis=0)

    def layer_norm(v):
        mu = jnp.mean(v, axis=-1, keepdims=True)
        vc = v - mu
        var = jnp.mean(vc * vc, axis=-1, keepdims=True)
        return vc * lax.rsqrt(var + LN_EPS) * ln_g + ln_b

    def store_xc(e):
        xc_ref[HALO:HALO + TS, :] = proj(SEG_XC)

    def store_cx(e):
        cx_ref[HALO:HALO + TS, :] = e["cg"] * e["xb"]

    def short_conv(e):
        conv = conv_b + conv_w[CONV_WIDTH - 1] * cx_ref[HALO:HALO + TS, :]
        for k in range(CONV_WIDTH - 1):
            back = CONV_WIDTH - 1 - k
            conv = conv + conv_w[k] * cx_ref[HALO - back:HALO - back + TS, :]
        cx_ref[0:HALO, :] = cx_ref[TS:TS + HALO, :]
        return conv

    def pooled_pairs(e):
        t1 = (j * TS + 1 + lax.broadcasted_iota(jnp.int32, (TS, C_GROUP), 0)).astype(f32)
        inv_t1 = 1.0 / t1
        pooled = []
        for gi, w in enumerate(POOL_WINDOWS):
            cols = slice(gi * C_GROUP, (gi + 1) * C_GROUP)
            ext = xc_ref[:, cols]
            win = ext
            span = 1
            while span < w:
                win = win + pltpu.roll(win, span, axis=0)
                span *= 2
            pooled.append(win[HALO:, :] * jnp.maximum(inv_t1, 1.0 / w) - ext[HALO:, :])
        xc_ref[0:HALO, :] = xc_ref[TS:TS + HALO, :]
        return [jnp.concatenate(pooled[2 * half:2 * half + 2], axis=1).astype(bf16)
                for half in range(2)]

    def pool_dots(e):
        groups = [jnp.dot(lhs, w_pool_bd_ref[half], preferred_element_type=f32)
                  for half, lhs in enumerate(e["pooled"])]
        return jnp.concatenate(groups, axis=1) * pool_scale

    def next_tile_u(e):
        h_ref[...] = _rmsnorm_rows(x_next_ref[0], norm_g).astype(bf16)
        puv_ref[:, 0:A_WIDTH] = proj(SEG_U)

    def next_tile_v(e):
        puv_ref[:, A_WIDTH:] = proj(SEG_V)

    def write_out(e):
        y = x_ref[0] + e["wo"]
        if apply_final_norm:
            y = _rmsnorm_rows(y, final_g)
        out_ref[0] = y

    def bdot(lhs, w_ref):
        return jnp.dot(lhs, w_ref[...], preferred_element_type=f32)

    matmul_jobs = {
        "xc": ((), store_xc),
        "za": ((), lambda e: proj(SEG_ZA)),
        "cg": ((), lambda e: proj(SEG_CG)),
        "xb": ((), lambda e: proj(SEG_XB)),
        "bg": ((), lambda e: proj(SEG_BG)),
        "zb": ((), lambda e: proj(SEG_ZB)),
        "zc": ((), lambda e: proj(SEG_ZC)),
        "ga": ((), lambda e: proj(SEG_GA)),
        "gb": ((), lambda e: proj(SEG_GB)),
        "gc": ((), lambda e: proj(SEG_GC)),
        "sp": (("vn",), lambda e: spatial_mix(e["vn"])),
        "pa": (("ya",), lambda e: bdot(e["ya"], w_pa_ref)),
        "pb": (("yb",), lambda e: bdot(e["yb"], w_pb_ref)),
        "po": (("pooled",), pool_dots),
        "pc": (("yc",), lambda e: bdot(e["yc"], w_pc_ref)),
        "u_next": (PROJ_JOBS, next_tile_u),
        "wo": (("merged_c",), lambda e: bdot(e["merged_c"].astype(bf16), w_o_ref)),
        "v_next": (("u_next",), next_tile_v),
    }
    elementwise = (
        ("u", (), lambda e: _gelu_of_half(puv_ref[:, 0:A_WIDTH])),
        ("vn", (), lambda e: layer_norm(_gelu_of_half(puv_ref[:, A_WIDTH:]))),
        ("cx", ("cg", "xb"), store_cx),
        ("ya", ("u", "sp", "za"),
         lambda e: (e["u"] * e["sp"] * _silu_of_half(e["za"])).astype(bf16)),
        ("conv", ("cx",), short_conv),
        ("yb", ("bg", "conv", "zb"),
         lambda e: (e["bg"] * e["conv"] * _silu_of_half(e["zb"])).astype(bf16)),
        ("pooled", ("xc",), pooled_pairs),
        ("yc", ("po", "zc"), lambda e: (e["po"] * _silu_of_half(e["zc"])).astype(bf16)),
        ("merged_a", ("ga", "pa"), lambda e: _twice_sigmoid_of_half(e["ga"]) * e["pa"]),
        ("merged_b", ("merged_a", "gb", "pb"),
         lambda e: e["merged_a"] + _twice_sigmoid_of_half(e["gb"]) * e["pb"]),
        ("gate_c", ("gc",), lambda e: _twice_sigmoid_of_half(e["gc"])),
        ("merged_c", ("merged_b", "gate_c", "pc"),
         lambda e: e["merged_b"] + e["gate_c"] * e["pc"]),
        ("out", ("wo",), write_out),
    )
    env = {}

    def trace_ready_elementwise():
        progress = True
        while progress:
            progress = False
            for name, needs, fn in elementwise:
                if name not in env and all(n in env for n in needs):
                    env[name] = fn(env)
                    progress = True

    assert sorted(MXU_ORDER) == sorted(matmul_jobs)
    for name in MXU_ORDER:
        trace_ready_elementwise()
        needs, fn = matmul_jobs[name]
        assert all(n in env for n in needs), (name, needs)
        env[name] = fn(env)
    trace_ready_elementwise()
    assert len(env) == len(matmul_jobs) + len(elementwise)
    if next_w_in is not None:
        next_w_in.end_step()


def _whole_spec(array):
    zeros = (0,) * array.ndim
    return pl.BlockSpec(array.shape, lambda b, j: zeros, pipeline_mode=pl.Buffered(1))


def _layer_spec(array, layer):
    zeros = (0,) * (array.ndim - 1)
    return pl.BlockSpec((None,) + array.shape[1:], lambda b, j: (layer,) + zeros,
                        pipeline_mode=pl.Buffered(1))


VECTOR_NAMES = ("norm_g", "ln_g", "ln_b", "conv_w", "conv_b", "pool_scale", "final_g")
PER_LAYER_NAMES = ("w_s", "b_s", "w_pool")
OUT_WEIGHT_NAMES = ("w_pa", "w_pb", "w_pc", "w_o")


def _layer(x, w_in, w_in_bf16, vectors, per_layer, out_weights, layer, *, convert_next,
           apply_final_norm):
    bsz, seq, _ = x.shape
    assert seq % TS == 0 and TS % CHUNK == 0 and TS >= HALO
    assert w_in.dtype == jnp.float32 and w_in.shape[1:] == (D_MODEL, IN_TOTAL)
    assert D_MODEL % W_ROWS == 0 and TS == 2 * O_ROWS and N_STAGE >= 3
    steps_per_seq = seq // TS
    n_steps = bsz * steps_per_seq
    bf16 = jnp.bfloat16
    f32 = jnp.float32

    def next_tile(b, j):
        s = jnp.minimum(b * steps_per_seq + j + 1, n_steps - 1)
        return (s // steps_per_seq, s % steps_per_seq, 0)

    tile = (1, TS, D_MODEL)
    any_spec = pl.BlockSpec(memory_space=pl.ANY)
    inputs = [("x", x, pl.BlockSpec(tile, lambda b, j: (b, j, 0))),
              ("x_next", x, pl.BlockSpec(tile, next_tile)),
              ("w_in_f32_hbm", w_in, any_spec)]
    if w_in_bf16 is not None:
        inputs.append(("w_in_bf16_hbm", w_in_bf16, any_spec))
    inputs += [(n, a, _whole_spec(a)) for n, a in zip(VECTOR_NAMES, vectors, strict=True)]
    inputs += [(n, a, _layer_spec(a, layer))
               for n, a in zip(PER_LAYER_NAMES, per_layer, strict=True)]
    inputs += [(n + "_hbm", a, any_spec)
               for n, a in zip(OUT_WEIGHT_NAMES, out_weights, strict=True)]

    outputs = [("out", jax.ShapeDtypeStruct(x.shape, x.dtype),
                pl.BlockSpec(tile, lambda b, j: (b, j, 0)))]
    scratch = [
        ("w_in", pltpu.VMEM((D_MODEL, IN_TOTAL), bf16)),
        ("sem", pltpu.SemaphoreType.DMA((N_STAGE,))),
        ("w_pa", pltpu.VMEM((A_WIDTH, D_MODEL), bf16)),
        ("w_pb", pltpu.VMEM((B_WIDTH, D_MODEL), bf16)),
        ("w_pc", pltpu.VMEM((C_WIDTH, D_MODEL), bf16)),
        ("w_o", pltpu.VMEM((D_MODEL, D_MODEL), bf16)),
        ("w_sp", pltpu.VMEM((A_GROUPS // 2, CHUNK, 2 * CHUNK), bf16)),
        ("b_sp", pltpu.VMEM((CHUNK, A_WIDTH), f32)),
        ("w_pool_bd", pltpu.VMEM((len(POOL_WINDOWS) // 2, 2 * C_GROUP, 2 * C_GROUP), bf16)),
        ("h", pltpu.VMEM((TS, D_MODEL), bf16)),
        ("puv", pltpu.VMEM((TS, 2 * A_WIDTH), f32)),
        ("cx", pltpu.VMEM((TS + HALO, B_WIDTH), f32)),
        ("xc", pltpu.VMEM((TS + HALO, C_WIDTH), f32)),
    ]
    if w_in_bf16 is None or convert_next:
        scratch.append(("stage", pltpu.VMEM((N_STAGE, W_ROWS, IN_TOTAL), f32)))
    if convert_next:
        assert D_MODEL % n_steps == 0 and (D_MODEL // n_steps) % BF16_SUBLANES == 0
        assert D_MODEL // n_steps <= W_ROWS and n_steps >= 2
        outputs.append(("w_in_next", jax.ShapeDtypeStruct((D_MODEL, IN_TOTAL), bf16), any_spec))
        scratch += [("cast", pltpu.VMEM((2, D_MODEL // n_steps, IN_TOTAL), bf16)),
                    ("next_sem", pltpu.SemaphoreType.DMA((4,)))]

    names = tuple(n for n, *_ in inputs + outputs + scratch)
    results = pl.pallas_call(
        functools.partial(_layer_kernel, names=names, layer=layer, steps_per_seq=steps_per_seq,
                          n_steps=n_steps, apply_final_norm=apply_final_norm),
        out_shape=[s for _, s, _ in outputs],
        grid=(bsz, steps_per_seq),
        in_specs=[spec for _, _, spec in inputs],
        out_specs=[spec for _, _, spec in outputs],
        scratch_shapes=[s for _, s in scratch],
        compiler_params=pltpu.CompilerParams(
            dimension_semantics=("arbitrary", "arbitrary"),
            vmem_limit_bytes=VMEM_LIMIT_BYTES,
        ),
        name="hybrid_layer_final" if apply_final_norm else "hybrid_layer",
    )(*[a for _, a, _ in inputs])
    return results[0], (results[1] if convert_next else None)


def kernel(x, norm_g, w_in, ln_g, ln_b, w_s, b_s, conv_w, conv_b, w_pool, pool_scale,
           w_pa, w_pb, w_pc, w_o, final_g):
    vectors = (norm_g, ln_g, ln_b, conv_w, conv_b, pool_scale, final_g[None, :])
    per_layer = (w_s, b_s, w_pool)
    out_weights = (w_pa, w_pb, w_pc, w_o)
    w_in_bf16 = None
    for layer in range(DEPTH):
        last = layer == DEPTH - 1
        x, w_in_bf16 = _layer(x, w_in, w_in_bf16, vectors, per_layer, out_weights, layer,
                              convert_next=not last, apply_final_norm=last)
    return x
```

```python
import functools
import math

import jax
import jax.numpy as jnp
from jax import lax
from jax.experimental import pallas as pl
from jax.experimental.pallas import tpu as pltpu

D_MODEL = 1024
DEPTH = 2
CHUNK = 128
A_GROUPS = 8
A_WIDTH = 512
A_HEAD = A_WIDTH // A_GROUPS
B_WIDTH = 512
CONV_WIDTH = 3
C_WIDTH = 512
POOL_WINDOWS = (2, 4, 8, 16)
C_GROUP = C_WIDTH // len(POOL_WINDOWS)
IN_TOTAL = 3 * A_WIDTH + 4 * B_WIDTH + 2 * C_WIDTH + 3 * D_MODEL
RMS_EPS = 1e-6
LN_EPS = 1e-5

_WIDTHS = [A_WIDTH] * 3 + [B_WIDTH] * 4 + [C_WIDTH] * 2 + [D_MODEL] * 3
_OFFS = [sum(_WIDTHS[:i]) for i in range(len(_WIDTHS) + 1)]
PROJ_SEGS = tuple((_OFFS[i], _OFFS[i + 1]) for i in range(len(_WIDTHS)))
(SEG_U, SEG_V, SEG_ZA, SEG_XB, SEG_BG, SEG_CG, SEG_ZB, SEG_XC, SEG_ZC,
 SEG_GA, SEG_GB, SEG_GC) = PROJ_SEGS
SEG_UV = (SEG_U[0], SEG_V[1])
PROJ_JOBS = ("xc", "za", "cg", "xb", "bg", "zb", "zc", "ga", "gb", "gc")
MXU_ORDER = ("za", "cg", "xb", "sp", "bg", "gb", "xc", "zb", "pa", "zc", "po", "ga", "pb", "gc",
             "u_next", "pc", "wo", "v_next")

LANES = 128
BF16_SUBLANES = 16
HALO = 16
TS = 512
W_ROWS = 64
N_STAGE = 3
O_ROWS = 256
VMEM_LIMIT_BYTES = 56 * 1024 * 1024


HALVED_SEGS = (SEG_U, SEG_V, SEG_ZA, SEG_ZB, SEG_ZC, SEG_GA, SEG_GB, SEG_GC)
_GELU_C1 = 2.0 * math.sqrt(2.0 / math.pi)
_GELU_C3 = 8.0 * 0.044715 * math.sqrt(2.0 / math.pi)


def _gelu_of_half(xh):
    t = jnp.tanh(xh * (_GELU_C1 + _GELU_C3 * (xh * xh)))
    return xh + xh * t


def _silu_of_half(zh):
    return zh + zh * jnp.tanh(zh)


def _twice_sigmoid_of_half(gh):
    return jnp.tanh(gh) + 1.0


def _rmsnorm_rows(x, gain):
    ms = jnp.mean(x * x, axis=-1, keepdims=True)
    return x * lax.rsqrt(ms + RMS_EPS) * gain


def _cast_w_in_rows(src, dst_ref, dst_rows):
    for lo, hi in PROJ_SEGS:
        w = src(lo, hi)
        if (lo, hi) in HALVED_SEGS:
            w = w * 0.5
        dst_ref[dst_rows, lo:hi] = w.astype(jnp.bfloat16)


def _w_in_copy(w_in_hbm_ref, stage_ref, sem_ref, layer, i):
    return pltpu.make_async_copy(
        w_in_hbm_ref.at[layer, pl.ds(i * W_ROWS, W_ROWS), :],
        stage_ref.at[i % N_STAGE], sem_ref.at[i % N_STAGE])


def _load_w_in(w_in_hbm_ref, w_in_ref, stage_ref, sem_ref, layer):
    n_blocks = D_MODEL // W_ROWS
    for i in range(N_STAGE):
        _w_in_copy(w_in_hbm_ref, stage_ref, sem_ref, layer, i).start()

    def convert_block(i, carry):
        _w_in_copy(w_in_hbm_ref, stage_ref, sem_ref, layer, i).wait()
        rows = pl.ds(pl.multiple_of(i * W_ROWS, W_ROWS), W_ROWS)
        _cast_w_in_rows(lambda lo, hi: stage_ref[i % N_STAGE, :, lo:hi], w_in_ref, rows)

        @pl.when(i + N_STAGE < n_blocks)
        def _():
            _w_in_copy(w_in_hbm_ref, stage_ref, sem_ref, layer, i + N_STAGE).start()

        return carry

    lax.fori_loop(0, n_blocks, convert_block, 0)


def _load_out_weights(hbm_refs, vmem_refs, stage_ref, sem_ref, layer):
    blocks = [(src, dst, r) for src, dst in zip(hbm_refs, vmem_refs)
              for r in range(0, dst.shape[0], O_ROWS)]

    def copy(i):
        src, _, r = blocks[i]
        return pltpu.make_async_copy(src.at[layer, pl.ds(r, O_ROWS), :],
                                     stage_ref.at[pl.ds((i % 2) * O_ROWS, O_ROWS), :],
                                     sem_ref.at[i % 2])

    copy(0).start()
    copy(1).start()
    for i, (_, dst, r) in enumerate(blocks):
        copy(i).wait()
        w = stage_ref[(i % 2) * O_ROWS:(i % 2 + 1) * O_ROWS, :]
        if dst is vmem_refs[-1]:
            w = w * 0.5
        dst[r:r + O_ROWS, :] = w.astype(jnp.bfloat16)
        if i + 2 < len(blocks):
            copy(i + 2).start()


class _NextLayerWIn:
    def __init__(self, w_in_hbm_ref, w_next_hbm_ref, stage_ref, cast_ref, sem_ref, layer,
                 step, n_steps):
        self.refs = (w_in_hbm_ref, w_next_hbm_ref, stage_ref, cast_ref, sem_ref)
        self.layer, self.step, self.n_steps = layer, step, n_steps
        self.rows = D_MODEL // n_steps

    def _rows_of(self, block):
        return pl.ds(pl.multiple_of(block * self.rows, self.rows), self.rows)

    def _fetch(self, block):
        w_in_hbm_ref, _, stage_ref, _, sem_ref = self.refs
        return pltpu.make_async_copy(w_in_hbm_ref.at[self.layer + 1, self._rows_of(block), :],
                                     stage_ref.at[block % 2, pl.ds(0, self.rows), :],
                                     sem_ref.at[block % 2])

    def _write_back(self, block):
        _, w_next_hbm_ref, _, cast_ref, sem_ref = self.refs
        return pltpu.make_async_copy(cast_ref.at[block % 2],
                                     w_next_hbm_ref.at[self._rows_of(block), :],
                                     sem_ref.at[2 + block % 2])

    def start_first_fetch(self):
        self._fetch(0).start()

    def begin_step(self):
        step = self.step
        self._fetch(step).wait()

        @pl.when(step + 1 < self.n_steps)
        def _():
            self._fetch(step + 1).start()

        @pl.when(step >= 2)
        def _():
            self._write_back(step - 2).wait()

    def cast(self):
        _, _, stage_ref, cast_ref, _ = self.refs
        slot = self.step % 2
        _cast_w_in_rows(lambda lo, hi: stage_ref[slot, 0:self.rows, lo:hi], cast_ref.at[slot],
                        slice(None))

    def end_step(self):
        step = self.step
        self._write_back(step).start()

        @pl.when(step == self.n_steps - 1)
        def _():
            self._write_back(step - 1).wait()
            self._write_back(step).wait()


def _prepare_small_weights(w_s_ref, b_s_ref, w_pool_ref, w_sp_ref, b_sp_ref, w_pool_bd_ref):
    f32 = jnp.float32
    bf16 = jnp.bfloat16
    row = lax.broadcasted_iota(jnp.int32, (CHUNK, CHUNK), 0)
    col = lax.broadcasted_iota(jnp.int32, (CHUNK, CHUNK), 1)
    causal = row >= col
    first_group = col < A_HEAD
    b_t = b_s_ref[...].T
    for jb in range(A_GROUPS // 2):
        w_pair = [jnp.where(causal, w_s_ref[2 * jb + k], 0.0) for k in range(2)]
        w_sp_ref[jb] = jnp.concatenate(w_pair, axis=1).astype(bf16)
        b_pair = [jnp.broadcast_to(b_t[:, 2 * jb + k:2 * jb + k + 1], (CHUNK, LANES))
                  for k in range(2)]
        b_sp_ref[:, jb * LANES:(jb + 1) * LANES] = jnp.where(first_group, b_pair[0], b_pair[1])
    zero = jnp.zeros((C_GROUP, C_GROUP), f32)
    for i in range(len(POOL_WINDOWS) // 2):
        top = jnp.concatenate([w_pool_ref[2 * i], zero], axis=1)
        bottom = jnp.concatenate([zero, w_pool_ref[2 * i + 1]], axis=1)
        w_pool_bd_ref[i] = jnp.concatenate([top, bottom], axis=0).astype(bf16)


def _layer_kernel(*refs, names, layer, steps_per_seq, n_steps, apply_final_norm):
    f32 = jnp.float32
    bf16 = jnp.bfloat16
    r = dict(zip(names, refs, strict=True))
    x_ref, x_next_ref, out_ref = r["x_tile"], r["x_next"], r["out"]
    w_in_ref, sem_ref = r["w_in"], r["sem"]
    w_pa_ref, w_pb_ref, w_pc_ref, w_o_ref = r["w_pa"], r["w_pb"], r["w_pc"], r["w_o"]
    w_sp_ref, b_sp_ref, w_pool_bd_ref = r["w_sp"], r["b_sp"], r["w_pool_bd"]
    h_ref, puv_ref, cx_ref, xc_ref = r["h"], r["puv"], r["cx"], r["xc"]
    norm_g_ref, ln_g_ref, ln_b_ref = r["norm_g"], r["ln_g"], r["ln_b"]
    conv_w_ref, conv_b_ref, pool_scale_ref = r["conv_w"], r["conv_b"], r["pool_scale"]
    final_g_ref = r["final_g"]
    j = pl.program_id(1)
    step = pl.program_id(0) * steps_per_seq + j
    next_w_in = None
    if "w_in_next" in r:
        next_w_in = _NextLayerWIn(r["w_in_f32_hbm"], r["w_in_next"], r["stage"], r["cast"],
                                  r["next_sem"], layer, step, n_steps)

    this_layer = slice(layer, layer + 1)
    norm_g, final_g = norm_g_ref[this_layer, :], final_g_ref[...]
    ln_g, ln_b = ln_g_ref[this_layer, :], ln_b_ref[this_layer, :]
    conv_b, pool_scale = conv_b_ref[this_layer, :], pool_scale_ref[this_layer, :]
    conv_w = [conv_w_ref[layer, k:k + 1, :] for k in range(CONV_WIDTH)]

    @pl.when(j == 0)
    def _():
        cx_ref[0:HALO, :] = jnp.zeros((HALO, B_WIDTH), f32)
        xc_ref[0:HALO, :] = jnp.zeros((HALO, C_WIDTH), f32)

    def proj(seg):
        return jnp.dot(h_ref[...], w_in_ref[:, seg[0]:seg[1]], preferred_element_type=f32)

    @pl.when(step == 0)
    def _():
        first_tile = pltpu.make_async_copy(r["x_hbm"].at[0, pl.ds(0, TS), :], x_ref,
                                           sem_ref.at[N_STAGE])
        first_tile.start()
        if "w_in_bf16_hbm" in r:
            ready_w_in = pltpu.make_async_copy(r["w_in_bf16_hbm"], w_in_ref, sem_ref.at[2])
            ready_w_in.start()
        else:
            _load_w_in(r["w_in_f32_hbm"], w_in_ref, r["stage"], sem_ref, layer)
        _load_out_weights((r["w_pa_hbm"], r["w_pb_hbm"], r["w_pc_hbm"], r["w_o_hbm"]),
                          (w_pa_ref, w_pb_ref, w_pc_ref, w_o_ref), puv_ref, sem_ref, layer)
        _prepare_small_weights(r["w_s"], r["b_s"], r["w_pool"], w_sp_ref, b_sp_ref,
                               w_pool_bd_ref)
        first_tile.wait()
        h_ref[...] = _rmsnorm_rows(x_ref[...], norm_g).astype(bf16)
        if "w_in_bf16_hbm" in r:
            ready_w_in.wait()
        puv_ref[...] = proj(SEG_UV)
        if next_w_in is not None:
            next_w_in.start_first_fetch()

    if next_w_in is not None:
        next_w_in.begin_step()
        next_w_in.cast()

    n_chunks = TS // CHUNK
    lane = lax.broadcasted_iota(jnp.int32, (CHUNK, LANES), 1)
    first_group = lane < A_HEAD

    def spatial_mix(vn):
        sg_blocks = []
        for jb in range(A_WIDTH // LANES):
            rhs = []
            for c in range(n_chunks):
                vb = vn[c * CHUNK:(c + 1) * CHUNK, jb * LANES:(jb + 1) * LANES]
                rhs.append(jnp.concatenate([jnp.where(first_group, vb, 0.0),
                                            jnp.where(first_group, 0.0, vb)], axis=0))
            rhs = jnp.concatenate(rhs, axis=1).astype(bf16)
            mixed = jnp.dot(w_sp_ref[jb], rhs, preferred_element_type=f32)
            sg_blocks.append(jnp.concatenate(
                [mixed[:, c * LANES:(c + 1) * LANES] for c in range(n_chunks)], axis=0))
        sg = jnp.concatenate(sg_blocks, axis=1)
        return sg + jnp.concatenate([b_sp_ref[...]] * n_chunks, axis=0)

    def layer_norm(v):
        mu = jnp.mean(v, axis=-1, keepdims=True)
        vc = v - mu
        var = jnp.mean(vc * vc, axis=-1, keepdims=True)
        return vc * lax.rsqrt(var + LN_EPS) * ln_g + ln_b

    def store_xc(e):
        xc_ref[HALO:HALO + TS, :] = proj(SEG_XC)

    def store_cx(e):
        cx_ref[HALO:HALO + TS, :] = e["cg"] * e["xb"]

    def short_conv(e):
        conv = conv_b + conv_w[CONV_WIDTH - 1] * cx_ref[HALO:HALO + TS, :]
        for k in range(CONV_WIDTH - 1):
            back = CONV_WIDTH - 1 - k
            conv = conv + conv_w[k] * cx_ref[HALO - back:HALO - back + TS, :]
        cx_ref[0:HALO, :] = cx_ref[TS:TS + HALO, :]
        return conv

    def pooled_pairs(e):
        t1 = (j * TS + 1 + lax.broadcasted_iota(jnp.int32, (TS, C_GROUP), 0)).astype(f32)
        inv_t1 = 1.0 / t1
        pooled = []
        for gi, w in enumerate(POOL_WINDOWS):
            cols = slice(gi * C_GROUP, (gi + 1) * C_GROUP)
            ext = xc_ref[:, cols]
            win = ext
            span = 1
            while span < w:
                win = win + pltpu.roll(win, span, axis=0)
                span *= 2
            pooled.append(win[HALO:, :] * jnp.maximum(inv_t1, 1.0 / w) - ext[HALO:, :])
        xc_ref[0:HALO, :] = xc_ref[TS:TS + HALO, :]
        return [jnp.concatenate(pooled[2 * half:2 * half + 2], axis=1).astype(bf16)
                for half in range(2)]

    def pool_dots(e):
        groups = [jnp.dot(lhs, w_pool_bd_ref[half], preferred_element_type=f32)
                  for half, lhs in enumerate(e["pooled"])]
        return jnp.concatenate(groups, axis=1) * pool_scale

    def next_tile_u(e):
        h_ref[...] = _rmsnorm_rows(x_next_ref[0], norm_g).astype(bf16)
        puv_ref[:, 0:A_WIDTH] = proj(SEG_U)

    def next_tile_v(e):
        puv_ref[:, A_WIDTH:] = proj(SEG_V)

    def write_out(e):
        y = x_ref[...] + e["wo"]
        if apply_final_norm:
            y = _rmsnorm_rows(y, final_g)
        out_ref[0] = y
        x_ref[...] = x_next_ref[0]

    def bdot(lhs, w_ref):
        return jnp.dot(lhs, w_ref[...], preferred_element_type=f32)

    matmul_jobs = {
        "xc": ((), store_xc),
        "za": ((), lambda e: proj(SEG_ZA)),
        "cg": ((), lambda e: proj(SEG_CG)),
        "xb": ((), lambda e: proj(SEG_XB)),
        "bg": ((), lambda e: proj(SEG_BG)),
        "zb": ((), lambda e: proj(SEG_ZB)),
        "zc": ((), lambda e: proj(SEG_ZC)),
        "ga": ((), lambda e: proj(SEG_GA)),
        "gb": ((), lambda e: proj(SEG_GB)),
        "gc": ((), lambda e: proj(SEG_GC)),
        "sp": (("vn",), lambda e: spatial_mix(e["vn"])),
        "pa": (("ya",), lambda e: bdot(e["ya"], w_pa_ref)),
        "pb": (("yb",), lambda e: bdot(e["yb"], w_pb_ref)),
        "po": (("pooled",), pool_dots),
        "pc": (("yc",), lambda e: bdot(e["yc"], w_pc_ref)),
        "u_next": (PROJ_JOBS, next_tile_u),
        "wo": (("merged_c",), lambda e: bdot(e["merged_c"].astype(bf16), w_o_ref)),
        "v_next": (("u_next",), next_tile_v),
    }
    elementwise = (
        ("u", (), lambda e: _gelu_of_half(puv_ref[:, 0:A_WIDTH])),
        ("vn", (), lambda e: layer_norm(_gelu_of_half(puv_ref[:, A_WIDTH:]))),
        ("cx", ("cg", "xb"), store_cx),
        ("ya", ("u", "sp", "za"),
         lambda e: (e["u"] * e["sp"] * _silu_of_half(e["za"])).astype(bf16)),
        ("conv", ("cx",), short_conv),
        ("yb", ("bg", "conv", "zb"),
         lambda e: (e["bg"] * e["conv"] * _silu_of_half(e["zb"])).astype(bf16)),
        ("pooled", ("xc",), pooled_pairs),
        ("yc", ("po", "zc"), lambda e: (e["po"] * _silu_of_half(e["zc"])).astype(bf16)),
        ("merged_a", ("ga", "pa"), lambda e: _twice_sigmoid_of_half(e["ga"]) * e["pa"]),
        ("merged_b", ("merged_a", "gb", "pb"),
         lambda e: e["merged_a"] + _twice_sigmoid_of_half(e["gb"]) * e["pb"]),
        ("gate_c", ("gc",), lambda e: _twice_sigmoid_of_half(e["gc"])),
        ("merged_c", ("merged_b", "gate_c", "pc"),
         lambda e: e["merged_b"] + e["gate_c"] * e["pc"]),
        ("out", ("wo",), write_out),
    )
    env = {}

    def trace_ready_elementwise():
        progress = True
        while progress:
            progress = False
            for name, needs, fn in elementwise:
                if name not in env and all(n in env for n in needs):
                    env[name] = fn(env)
                    progress = True

    assert sorted(MXU_ORDER) == sorted(matmul_jobs)
    for name in MXU_ORDER:
        trace_ready_elementwise()
        needs, fn = matmul_jobs[name]
        assert all(n in env for n in needs), (name, needs)
        env[name] = fn(env)
    trace_ready_elementwise()
    assert len(env) == len(matmul_jobs) + len(elementwise)
    if next_w_in is not None:
        next_w_in.end_step()


def _whole_spec(array):
    zeros = (0,) * array.ndim
    return pl.BlockSpec(array.shape, lambda b, j: zeros, pipeline_mode=pl.Buffered(1))


def _layer_spec(array, layer):
    zeros = (0,) * (array.ndim - 1)
    return pl.BlockSpec((None,) + array.shape[1:], lambda b, j: (layer,) + zeros,
                        pipeline_mode=pl.Buffered(1))


VECTOR_NAMES = ("norm_g", "ln_g", "ln_b", "conv_w", "conv_b", "pool_scale", "final_g")
PER_LAYER_NAMES = ("w_s", "b_s", "w_pool")
OUT_WEIGHT_NAMES = ("w_pa", "w_pb", "w_pc", "w_o")


def _layer(x, w_in, w_in_bf16, vectors, per_layer, out_weights, layer, *, convert_next,
           apply_final_norm):
    bsz, seq, _ = x.shape
    assert seq % TS == 0 and TS % CHUNK == 0 and TS >= HALO
    assert w_in.dtype == jnp.float32 and w_in.shape[1:] == (D_MODEL, IN_TOTAL)
    assert D_MODEL % W_ROWS == 0 and TS == 2 * O_ROWS and N_STAGE >= 3
    steps_per_seq = seq // TS
    n_steps = bsz * steps_per_seq
    bf16 = jnp.bfloat16
    f32 = jnp.float32

    def next_tile(b, j):
        s = jnp.minimum(b * steps_per_seq + j + 1, n_steps - 1)
        return (s // steps_per_seq, s % steps_per_seq, 0)

    tile = (1, TS, D_MODEL)
    any_spec = pl.BlockSpec(memory_space=pl.ANY)
    inputs = [("x_hbm", x, any_spec),
              ("x_next", x, pl.BlockSpec(tile, next_tile)),
              ("w_in_f32_hbm", w_in, any_spec)]
    if w_in_bf16 is not None:
        inputs.append(("w_in_bf16_hbm", w_in_bf16, any_spec))
    inputs += [(n, a, _whole_spec(a)) for n, a in zip(VECTOR_NAMES, vectors, strict=True)]
    inputs += [(n, a, _layer_spec(a, layer))
               for n, a in zip(PER_LAYER_NAMES, per_layer, strict=True)]
    inputs += [(n + "_hbm", a, any_spec)
               for n, a in zip(OUT_WEIGHT_NAMES, out_weights, strict=True)]

    outputs = [("out", jax.ShapeDtypeStruct(x.shape, x.dtype),
                pl.BlockSpec(tile, lambda b, j: (b, j, 0)))]
    scratch = [
        ("w_in", pltpu.VMEM((D_MODEL, IN_TOTAL), bf16)),
        ("sem", pltpu.SemaphoreType.DMA((N_STAGE + 1,))),
        ("x_tile", pltpu.VMEM((TS, D_MODEL), f32)),
        ("w_pa", pltpu.VMEM((A_WIDTH, D_MODEL), bf16)),
        ("w_pb", pltpu.VMEM((B_WIDTH, D_MODEL), bf16)),
        ("w_pc", pltpu.VMEM((C_WIDTH, D_MODEL), bf16)),
        ("w_o", pltpu.VMEM((D_MODEL, D_MODEL), bf16)),
        ("w_sp", pltpu.VMEM((A_GROUPS // 2, CHUNK, 2 * CHUNK), bf16)),
        ("b_sp", pltpu.VMEM((CHUNK, A_WIDTH), f32)),
        ("w_pool_bd", pltpu.VMEM((len(POOL_WINDOWS) // 2, 2 * C_GROUP, 2 * C_GROUP), bf16)),
        ("h", pltpu.VMEM((TS, D_MODEL), bf16)),
        ("puv", pltpu.VMEM((TS, 2 * A_WIDTH), f32)),
        ("cx", pltpu.VMEM((TS + HALO, B_WIDTH), f32)),
        ("xc", pltpu.VMEM((TS + HALO, C_WIDTH), f32)),
    ]
    if w_in_bf16 is None or convert_next:
        scratch.append(("stage", pltpu.VMEM((N_STAGE, W_ROWS, IN_TOTAL), f32)))
    if convert_next:
        assert D_MODEL % n_steps == 0 and (D_MODEL // n_steps) % BF16_SUBLANES == 0
        assert D_MODEL // n_steps <= W_ROWS and n_steps >= 2
        outputs.append(("w_in_next", jax.ShapeDtypeStruct((D_MODEL, IN_TOTAL), bf16), any_spec))
        scratch += [("cast", pltpu.VMEM((2, D_MODEL // n_steps, IN_TOTAL), bf16)),
                    ("next_sem", pltpu.SemaphoreType.DMA((4,)))]

    names = tuple(n for n, *_ in inputs + outputs + scratch)
    results = pl.pallas_call(
        functools.partial(_layer_kernel, names=names, layer=layer, steps_per_seq=steps_per_seq,
                          n_steps=n_steps, apply_final_norm=apply_final_norm),
        out_shape=[s for _, s, _ in outputs],
        grid=(bsz, steps_per_seq),
        in_specs=[spec for _, _, spec in inputs],
        out_specs=[spec for _, _, spec in outputs],
        scratch_shapes=[s for _, s in scratch],
        compiler_params=pltpu.CompilerParams(
            dimension_semantics=("arbitrary", "arbitrary"),
            vmem_limit_bytes=VMEM_LIMIT_BYTES,
        ),
        name="hybrid_layer_final" if apply_final_norm else "hybrid_layer",
    )(*[a for _, a, _ in inputs])
    return results[0], (results[1] if convert_next else None)


def kernel(x, norm_g, w_in, ln_g, ln_b, w_s, b_s, conv_w, conv_b, w_pool, pool_scale,
           w_pa, w_pb, w_pc, w_o, final_g):
    vectors = (norm_g, ln_g, ln_b, conv_w, conv_b, pool_scale, final_g[None, :])
    per_layer = (w_s, b_s, w_pool)
    out_weights = (w_pa, w_pb, w_pc, w_o)
    w_in_bf16 = None
    for layer in range(DEPTH):
        last = layer == DEPTH - 1
        x, w_in_bf16 = _layer(x, w_in, w_in_bf16, vectors, per_layer, out_weights, layer,
                              convert_next=not last, apply_final_norm=last)
    return x
```

```python
import functools
import math

import jax
import jax.numpy as jnp
from jax import lax
from jax.experimental import pallas as pl
from jax.experimental.pallas import tpu as pltpu

D_MODEL = 1024
DEPTH = 2
CHUNK = 128
A_GROUPS = 8
A_WIDTH = 512
A_HEAD = A_WIDTH // A_GROUPS
B_WIDTH = 512
CONV_WIDTH = 3
C_WIDTH = 512
POOL_WINDOWS = (2, 4, 8, 16)
C_GROUP = C_WIDTH // len(POOL_WINDOWS)
IN_TOTAL = 3 * A_WIDTH + 4 * B_WIDTH + 2 * C_WIDTH + 3 * D_MODEL
RMS_EPS = 1e-6
LN_EPS = 1e-5

_WIDTHS = [A_WIDTH] * 3 + [B_WIDTH] * 4 + [C_WIDTH] * 2 + [D_MODEL] * 3
_OFFS = [sum(_WIDTHS[:i]) for i in range(len(_WIDTHS) + 1)]
PROJ_SEGS = tuple((_OFFS[i], _OFFS[i + 1]) for i in range(len(_WIDTHS)))
(SEG_U, SEG_V, SEG_ZA, SEG_XB, SEG_BG, SEG_CG, SEG_ZB, SEG_XC, SEG_ZC,
 SEG_GA, SEG_GB, SEG_GC) = PROJ_SEGS
SEG_UV = (SEG_U[0], SEG_V[1])
PROJ_JOBS = ("xc", "za", "cg", "xb", "bg", "zb", "zc", "ga", "gb", "gc")
MXU_ORDER = ("za", "cg", "xb", "sp", "bg", "gb", "xc", "zb", "pa", "zc", "po", "ga", "pb", "gc",
             "u_next", "pc", "wo", "v_next")

LANES = 128
BF16_SUBLANES = 16
HALO = 16
TS = 512
W_ROWS = 32
N_STAGE = 6
O_ROWS = 256
VMEM_LIMIT_BYTES = 56 * 1024 * 1024


HALVED_SEGS = (SEG_U, SEG_V, SEG_ZA, SEG_ZB, SEG_ZC, SEG_GA, SEG_GB, SEG_GC)
_GELU_C1 = 2.0 * math.sqrt(2.0 / math.pi)
_GELU_C3 = 8.0 * 0.044715 * math.sqrt(2.0 / math.pi)


def _gelu_of_half(xh):
    t = jnp.tanh(xh * (_GELU_C1 + _GELU_C3 * (xh * xh)))
    return xh + xh * t


def _silu_of_half(zh):
    return zh + zh * jnp.tanh(zh)


def _twice_sigmoid_of_half(gh):
    return jnp.tanh(gh) + 1.0


def _rmsnorm_rows(x, gain):
    ms = jnp.mean(x * x, axis=-1, keepdims=True)
    return x * lax.rsqrt(ms + RMS_EPS) * gain


def _cast_w_in_rows(src, dst_ref, dst_rows):
    for lo, hi in PROJ_SEGS:
        w = src(lo, hi)
        if (lo, hi) in HALVED_SEGS:
            w = w * 0.5
        dst_ref[dst_rows, lo:hi] = w.astype(jnp.bfloat16)


def _w_in_copy(w_in_hbm_ref, stage_ref, sem_ref, layer, i):
    return pltpu.make_async_copy(
        w_in_hbm_ref.at[layer, pl.ds(i * W_ROWS, W_ROWS), :],
        stage_ref.at[i % N_STAGE], sem_ref.at[i % N_STAGE])


def _load_w_in(w_in_hbm_ref, w_in_ref, stage_ref, sem_ref, layer):
    n_blocks = D_MODEL // W_ROWS
    for i in range(N_STAGE):
        _w_in_copy(w_in_hbm_ref, stage_ref, sem_ref, layer, i).start()

    def convert_block(i, carry):
        _w_in_copy(w_in_hbm_ref, stage_ref, sem_ref, layer, i).wait()
        rows = pl.ds(pl.multiple_of(i * W_ROWS, W_ROWS), W_ROWS)
        _cast_w_in_rows(lambda lo, hi: stage_ref[i % N_STAGE, :, lo:hi], w_in_ref, rows)

        @pl.when(i + N_STAGE < n_blocks)
        def _():
            _w_in_copy(w_in_hbm_ref, stage_ref, sem_ref, layer, i + N_STAGE).start()

        return carry

    lax.fori_loop(0, n_blocks, convert_block, 0)


def _load_out_weights(hbm_refs, vmem_refs, stage_ref, sem_ref, layer):
    blocks = [(src, dst, r) for src, dst in zip(hbm_refs, vmem_refs)
              for r in range(0, dst.shape[0], O_ROWS)]

    def copy(i):
        src, _, r = blocks[i]
        return pltpu.make_async_copy(src.at[layer, pl.ds(r, O_ROWS), :],
                                     stage_ref.at[pl.ds((i % 2) * O_ROWS, O_ROWS), :],
                                     sem_ref.at[i % 2])

    copy(0).start()
    copy(1).start()
    for i, (_, dst, r) in enumerate(blocks):
        copy(i).wait()
        w = stage_ref[(i % 2) * O_ROWS:(i % 2 + 1) * O_ROWS, :]
        if dst is vmem_refs[-1]:
            w = w * 0.5
        dst[r:r + O_ROWS, :] = w.astype(jnp.bfloat16)
        if i + 2 < len(blocks):
            copy(i + 2).start()


class _NextLayerWIn:
    def __init__(self, w_in_hbm_ref, w_next_hbm_ref, stage_ref, cast_ref, sem_ref, layer,
                 step, n_steps):
        self.refs = (w_in_hbm_ref, w_next_hbm_ref, stage_ref, cast_ref, sem_ref)
        self.layer, self.step, self.n_steps = layer, step, n_steps
        self.rows = D_MODEL // n_steps

    def _rows_of(self, block):
        return pl.ds(pl.multiple_of(block * self.rows, self.rows), self.rows)

    def _fetch(self, block):
        w_in_hbm_ref, _, stage_ref, _, sem_ref = self.refs
        return pltpu.make_async_copy(w_in_hbm_ref.at[self.layer + 1, self._rows_of(block), :],
                                     stage_ref.at[block % 2, pl.ds(0, self.rows), :],
                                     sem_ref.at[block % 2])

    def _write_back(self, block):
        _, w_next_hbm_ref, _, cast_ref, sem_ref = self.refs
        return pltpu.make_async_copy(cast_ref.at[block % 2],
                                     w_next_hbm_ref.at[self._rows_of(block), :],
                                     sem_ref.at[2 + block % 2])

    def start_first_fetch(self):
        self._fetch(0).start()

    def begin_step(self):
        step = self.step
        self._fetch(step).wait()

        @pl.when(step + 1 < self.n_steps)
        def _():
            self._fetch(step + 1).start()

        @pl.when(step >= 2)
        def _():
            self._write_back(step - 2).wait()

    def cast(self):
        _, _, stage_ref, cast_ref, _ = self.refs
        slot = self.step % 2
        _cast_w_in_rows(lambda lo, hi: stage_ref[slot, 0:self.rows, lo:hi], cast_ref.at[slot],
                        slice(None))

    def end_step(self):
        step = self.step
        self._write_back(step).start()

        @pl.when(step == self.n_steps - 1)
        def _():
            self._write_back(step - 1).wait()
            self._write_back(step).wait()


def _prepare_small_weights(w_s_ref, b_s_ref, w_pool_ref, w_sp_ref, b_sp_ref, w_pool_bd_ref):
    f32 = jnp.float32
    bf16 = jnp.bfloat16
    row = lax.broadcasted_iota(jnp.int32, (CHUNK, CHUNK), 0)
    col = lax.broadcasted_iota(jnp.int32, (CHUNK, CHUNK), 1)
    causal = row >= col
    first_group = col < A_HEAD
    b_t = b_s_ref[...].T
    for jb in range(A_GROUPS // 2):
        w_pair = [jnp.where(causal, w_s_ref[2 * jb + k], 0.0) for k in range(2)]
        w_sp_ref[jb] = jnp.concatenate(w_pair, axis=1).astype(bf16)
        b_pair = [jnp.broadcast_to(b_t[:, 2 * jb + k:2 * jb + k + 1], (CHUNK, LANES))
                  for k in range(2)]
        b_sp_ref[:, jb * LANES:(jb + 1) * LANES] = jnp.where(first_group, b_pair[0], b_pair[1])
    zero = jnp.zeros((C_GROUP, C_GROUP), f32)
    for i in range(len(POOL_WINDOWS) // 2):
        top = jnp.concatenate([w_pool_ref[2 * i], zero], axis=1)
        bottom = jnp.concatenate([zero, w_pool_ref[2 * i + 1]], axis=1)
        w_pool_bd_ref[i] = jnp.concatenate([top, bottom], axis=0).astype(bf16)


def _layer_kernel(*refs, names, layer, steps_per_seq, n_steps, apply_final_norm):
    f32 = jnp.float32
    bf16 = jnp.bfloat16
    r = dict(zip(names, refs, strict=True))
    x_ref, x_next_ref, out_ref = r["x"], r["x_next"], r["out"]
    w_in_ref, sem_ref = r["w_in"], r["sem"]
    w_pa_ref, w_pb_ref, w_pc_ref, w_o_ref = r["w_pa"], r["w_pb"], r["w_pc"], r["w_o"]
    w_sp_ref, b_sp_ref, w_pool_bd_ref = r["w_sp"], r["b_sp"], r["w_pool_bd"]
    h_ref, puv_ref, cx_ref, xc_ref = r["h"], r["puv"], r["cx"], r["xc"]
    norm_g_ref, ln_g_ref, ln_b_ref = r["norm_g"], r["ln_g"], r["ln_b"]
    conv_w_ref, conv_b_ref, pool_scale_ref = r["conv_w"], r["conv_b"], r["pool_scale"]
    final_g_ref = r["final_g"]
    j = pl.program_id(1)
    step = pl.program_id(0) * steps_per_seq + j
    next_w_in = None
    if "w_in_next" in r:
        next_w_in = _NextLayerWIn(r["w_in_f32_hbm"], r["w_in_next"], r["stage"], r["cast"],
                                  r["next_sem"], layer, step, n_steps)

    this_layer = slice(layer, layer + 1)
    norm_g, final_g = norm_g_ref[this_layer, :], final_g_ref[...]
    ln_g, ln_b = ln_g_ref[this_layer, :], ln_b_ref[this_layer, :]
    conv_b, pool_scale = conv_b_ref[this_layer, :], pool_scale_ref[this_layer, :]
    conv_w = [conv_w_ref[layer, k:k + 1, :] for k in range(CONV_WIDTH)]

    @pl.when(j == 0)
    def _():
        cx_ref[0:HALO, :] = jnp.zeros((HALO, B_WIDTH), f32)
        xc_ref[0:HALO, :] = jnp.zeros((HALO, C_WIDTH), f32)

    def proj(seg):
        return jnp.dot(h_ref[...], w_in_ref[:, seg[0]:seg[1]], preferred_element_type=f32)

    @pl.when(step == 0)
    def _():
        if "w_in_bf16_hbm" in r:
            ready_w_in = pltpu.make_async_copy(r["w_in_bf16_hbm"], w_in_ref, sem_ref.at[2])
            ready_w_in.start()
        else:
            _load_w_in(r["w_in_f32_hbm"], w_in_ref, r["stage"], sem_ref, layer)
        _load_out_weights((r["w_pa_hbm"], r["w_pb_hbm"], r["w_pc_hbm"], r["w_o_hbm"]),
                          (w_pa_ref, w_pb_ref, w_pc_ref, w_o_ref), puv_ref, sem_ref, layer)
        _prepare_small_weights(r["w_s"], r["b_s"], r["w_pool"], w_sp_ref, b_sp_ref,
                               w_pool_bd_ref)
        h_ref[...] = _rmsnorm_rows(x_ref[0], norm_g).astype(bf16)
        if "w_in_bf16_hbm" in r:
            ready_w_in.wait()
        puv_ref[...] = proj(SEG_UV)
        if next_w_in is not None:
            next_w_in.start_first_fetch()

    if next_w_in is not None:
        next_w_in.begin_step()
        next_w_in.cast()

    n_chunks = TS // CHUNK
    lane = lax.broadcasted_iota(jnp.int32, (CHUNK, LANES), 1)
    first_group = lane < A_HEAD

    def spatial_mix(vn):
        sg_blocks = []
        for jb in range(A_WIDTH // LANES):
            rhs = []
            for c in range(n_chunks):
                vb = vn[c * CHUNK:(c + 1) * CHUNK, jb * LANES:(jb + 1) * LANES]
                rhs.append(jnp.concatenate([jnp.where(first_group, vb, 0.0),
                                            jnp.where(first_group, 0.0, vb)], axis=0))
            rhs = jnp.concatenate(rhs, axis=1).astype(bf16)
            mixed = jnp.dot(w_sp_ref[jb], rhs, preferred_element_type=f32)
            sg_blocks.append(jnp.concatenate(
                [mixed[:, c * LANES:(c + 1) * LANES] for c in range(n_chunks)], axis=0))
        sg = jnp.concatenate(sg_blocks, axis=1)
        return sg + jnp.concatenate([b_sp_ref[...]] * n_chunks, axis=0)

    def layer_norm(v):
        mu = jnp.mean(v, axis=-1, keepdims=True)
        vc = v - mu
        var = jnp.mean(vc * vc, axis=-1, keepdims=True)
        return vc * lax.rsqrt(var + LN_EPS) * ln_g + ln_b

    def store_xc(e):
        xc_ref[HALO:HALO + TS, :] = proj(SEG_XC)

    def store_cx(e):
        cx_ref[HALO:HALO + TS, :] = e["cg"] * e["xb"]

    def short_conv(e):
        conv = conv_b + conv_w[CONV_WIDTH - 1] * cx_ref[HALO:HALO + TS, :]
        for k in range(CONV_WIDTH - 1):
            back = CONV_WIDTH - 1 - k
            conv = conv + conv_w[k] * cx_ref[HALO - back:HALO - back + TS, :]
        cx_ref[0:HALO, :] = cx_ref[TS:TS + HALO, :]
        return conv

    def pooled_pairs(e):
        t1 = (j * TS + 1 + lax.broadcasted_iota(jnp.int32, (TS, C_GROUP), 0)).astype(f32)
        inv_t1 = 1.0 / t1
        pooled = []
        for gi, w in enumerate(POOL_WINDOWS):
            cols = slice(gi * C_GROUP, (gi + 1) * C_GROUP)
            ext = xc_ref[:, cols]
            win = ext
            span = 1
            while span < w:
                win = win + pltpu.roll(win, span, axis=0)
                span *= 2
            pooled.append(win[HALO:, :] * jnp.maximum(inv_t1, 1.0 / w) - ext[HALO:, :])
        xc_ref[0:HALO, :] = xc_ref[TS:TS + HALO, :]
        return [jnp.concatenate(pooled[2 * half:2 * half + 2], axis=1).astype(bf16)
                for half in range(2)]

    def pool_dots(e):
        groups = [jnp.dot(lhs, w_pool_bd_ref[half], preferred_element_type=f32)
                  for half, lhs in enumerate(e["pooled"])]
        return jnp.concatenate(groups, axis=1) * pool_scale

    def next_tile_u(e):
        h_ref[...] = _rmsnorm_rows(x_next_ref[0], norm_g).astype(bf16)
        puv_ref[:, 0:A_WIDTH] = proj(SEG_U)

    def next_tile_v(e):
        puv_ref[:, A_WIDTH:] = proj(SEG_V)

    def write_out(e):
        y = x_ref[0] + e["wo"]
        if apply_final_norm:
            y = _rmsnorm_rows(y, final_g)
        out_ref[0] = y

    def bdot(lhs, w_ref):
        return jnp.dot(lhs, w_ref[...], preferred_element_type=f32)

    matmul_jobs = {
        "xc": ((), store_xc),
        "za": ((), lambda e: proj(SEG_ZA)),
        "cg": ((), lambda e: proj(SEG_CG)),
        "xb": ((), lambda e: proj(SEG_XB)),
        "bg": ((), lambda e: proj(SEG_BG)),
        "zb": ((), lambda e: proj(SEG_ZB)),
        "zc": ((), lambda e: proj(SEG_ZC)),
        "ga": ((), lambda e: proj(SEG_GA)),
        "gb": ((), lambda e: proj(SEG_GB)),
        "gc": ((), lambda e: proj(SEG_GC)),
        "sp": (("vn",), lambda e: spatial_mix(e["vn"])),
        "pa": (("ya",), lambda e: bdot(e["ya"], w_pa_ref)),
        "pb": (("yb",), lambda e: bdot(e["yb"], w_pb_ref)),
        "po": (("pooled",), pool_dots),
        "pc": (("yc",), lambda e: bdot(e["yc"], w_pc_ref)),
        "u_next": (PROJ_JOBS, next_tile_u),
        "wo": (("merged_c",), lambda e: bdot(e["merged_c"].astype(bf16), w_o_ref)),
        "v_next": (("u_next",), next_tile_v),
    }
    elementwise = (
        ("u", (), lambda e: _gelu_of_half(puv_ref[:, 0:A_WIDTH])),
        ("vn", (), lambda e: layer_norm(_gelu_of_half(puv_ref[:, A_WIDTH:]))),
        ("cx", ("cg", "xb"), store_cx),
        ("ya", ("u", "sp", "za"),
         lambda e: (e["u"] * e["sp"] * _silu_of_half(e["za"])).astype(bf16)),
        ("conv", ("cx",), short_conv),
        ("yb", ("bg", "conv", "zb"),
         lambda e: (e["bg"] * e["conv"] * _silu_of_half(e["zb"])).astype(bf16)),
        ("pooled", ("xc",), pooled_pairs),
        ("yc", ("po", "zc"), lambda e: (e["po"] * _silu_of_half(e["zc"])).astype(bf16)),
        ("merged_a", ("ga", "pa"), lambda e: _twice_sigmoid_of_half(e["ga"]) * e["pa"]),
        ("merged_b", ("merged_a", "gb", "pb"),
         lambda e: e["merged_a"] + _twice_sigmoid_of_half(e["gb"]) * e["pb"]),
        ("gate_c", ("gc",), lambda e: _twice_sigmoid_of_half(e["gc"])),
        ("merged_c", ("merged_b", "gate_c", "pc"),
         lambda e: e["merged_b"] + e["gate_c"] * e["pc"]),
        ("out", ("wo",), write_out),
    )
    env = {}

    def trace_ready_elementwise():
        progress = True
        while progress:
            progress = False
            for name, needs, fn in elementwise:
                if name not in env and all(n in env for n in needs):
                    env[name] = fn(env)
                    progress = True

    assert sorted(MXU_ORDER) == sorted(matmul_jobs)
    for name in MXU_ORDER:
        trace_ready_elementwise()
        needs, fn = matmul_jobs[name]
        assert all(n in env for n in needs), (name, needs)
        env[name] = fn(env)
    trace_ready_elementwise()
    assert len(env) == len(matmul_jobs) + len(elementwise)
    if next_w_in is not None:
        next_w_in.end_step()


def _whole_spec(array):
    zeros = (0,) * array.ndim
    return pl.BlockSpec(array.shape, lambda b, j: zeros, pipeline_mode=pl.Buffered(1))


def _layer_spec(array, layer):
    zeros = (0,) * (array.ndim - 1)
    return pl.BlockSpec((None,) + array.shape[1:], lambda b, j: (layer,) + zeros,
                        pipeline_mode=pl.Buffered(1))


VECTOR_NAMES = ("norm_g", "ln_g", "ln_b", "conv_w", "conv_b", "pool_scale", "final_g")
PER_LAYER_NAMES = ("w_s", "b_s", "w_pool")
OUT_WEIGHT_NAMES = ("w_pa", "w_pb", "w_pc", "w_o")


def _layer(x, w_in, w_in_bf16, vectors, per_layer, out_weights, layer, *, convert_next,
           apply_final_norm):
    bsz, seq, _ = x.shape
    assert seq % TS == 0 and TS % CHUNK == 0 and TS >= HALO
    assert w_in.dtype == jnp.float32 and w_in.shape[1:] == (D_MODEL, IN_TOTAL)
    assert D_MODEL % W_ROWS == 0 and TS == 2 * O_ROWS and N_STAGE >= 3
    steps_per_seq = seq // TS
    n_steps = bsz * steps_per_seq
    bf16 = jnp.bfloat16
    f32 = jnp.float32

    def next_tile(b, j):
        s = jnp.minimum(b * steps_per_seq + j + 1, n_steps - 1)
        return (s // steps_per_seq, s % steps_per_seq, 0)

    tile = (1, TS, D_MODEL)
    any_spec = pl.BlockSpec(memory_space=pl.ANY)
    inputs = [("x", x, pl.BlockSpec(tile, lambda b, j: (b, j, 0))),
              ("x_next", x, pl.BlockSpec(tile, next_tile)),
              ("w_in_f32_hbm", w_in, any_spec)]
    if w_in_bf16 is not None:
        inputs.append(("w_in_bf16_hbm", w_in_bf16, any_spec))
    inputs += [(n, a, _whole_spec(a)) for n, a in zip(VECTOR_NAMES, vectors, strict=True)]
    inputs += [(n, a, _layer_spec(a, layer))
               for n, a in zip(PER_LAYER_NAMES, per_layer, strict=True)]
    inputs += [(n + "_hbm", a, any_spec)
               for n, a in zip(OUT_WEIGHT_NAMES, out_weights, strict=True)]

    outputs = [("out", jax.ShapeDtypeStruct(x.shape, x.dtype),
                pl.BlockSpec(tile, lambda b, j: (b, j, 0)))]
    scratch = [
        ("w_in", pltpu.VMEM((D_MODEL, IN_TOTAL), bf16)),
        ("sem", pltpu.SemaphoreType.DMA((N_STAGE,))),
        ("w_pa", pltpu.VMEM((A_WIDTH, D_MODEL), bf16)),
        ("w_pb", pltpu.VMEM((B_WIDTH, D_MODEL), bf16)),
        ("w_pc", pltpu.VMEM((C_WIDTH, D_MODEL), bf16)),
        ("w_o", pltpu.VMEM((D_MODEL, D_MODEL), bf16)),
        ("w_sp", pltpu.VMEM((A_GROUPS // 2, CHUNK, 2 * CHUNK), bf16)),
        ("b_sp", pltpu.VMEM((CHUNK, A_WIDTH), f32)),
        ("w_pool_bd", pltpu.VMEM((len(POOL_WINDOWS) // 2, 2 * C_GROUP, 2 * C_GROUP), bf16)),
        ("h", pltpu.VMEM((TS, D_MODEL), bf16)),
        ("puv", pltpu.VMEM((TS, 2 * A_WIDTH), f32)),
        ("cx", pltpu.VMEM((TS + HALO, B_WIDTH), f32)),
        ("xc", pltpu.VMEM((TS + HALO, C_WIDTH), f32)),
    ]
    if w_in_bf16 is None or convert_next:
        scratch.append(("stage", pltpu.VMEM((N_STAGE, W_ROWS, IN_TOTAL), f32)))
    if convert_next:
        assert D_MODEL % n_steps == 0 and (D_MODEL // n_steps) % BF16_SUBLANES == 0
        assert D_MODEL // n_steps <= W_ROWS and n_steps >= 2
        outputs.append(("w_in_next", jax.ShapeDtypeStruct((D_MODEL, IN_TOTAL), bf16), any_spec))
        scratch += [("cast", pltpu.VMEM((2, D_MODEL // n_steps, IN_TOTAL), bf16)),
                    ("next_sem", pltpu.SemaphoreType.DMA((4,)))]

    names = tuple(n for n, *_ in inputs + outputs + scratch)
    results = pl.pallas_call(
        functools.partial(_layer_kernel, names=names, layer=layer, steps_per_seq=steps_per_seq,
                          n_steps=n_steps, apply_final_norm=apply_final_norm),
        out_shape=[s for _, s, _ in outputs],
        grid=(bsz, steps_per_seq),
        in_specs=[spec for _, _, spec in inputs],
        out_specs=[spec for _, _, spec in outputs],
        scratch_shapes=[s for _, s in scratch],
        compiler_params=pltpu.CompilerParams(
            dimension_semantics=("arbitrary", "arbitrary"),
            vmem_limit_bytes=VMEM_LIMIT_BYTES,
        ),
        name="hybrid_layer_final" if apply_final_norm else "hybrid_layer",
    )(*[a for _, a, _ in inputs])
    return results[0], (results[1] if convert_next else None)


def kernel(x, norm_g, w_in, ln_g, ln_b, w_s, b_s, conv_w, conv_b, w_pool, pool_scale,
           w_pa, w_pb, w_pc, w_o, final_g):
    vectors = (norm_g, ln_g, ln_b, conv_w, conv_b, pool_scale, final_g[None, :])
    per_layer = (w_s, b_s, w_pool)
    out_weights = (w_pa, w_pb, w_pc, w_o)
    w_in_bf16 = None
    for layer in range(DEPTH):
        last = layer == DEPTH - 1
        x, w_in_bf16 = _layer(x, w_in, w_in_bf16, vectors, per_layer, out_weights, layer,
                              convert_next=not last, apply_final_norm=last)
    return x
```

```python
import functools
import math

import jax
import jax.numpy as jnp
from jax import lax
from jax.experimental import pallas as pl
from jax.experimental.pallas import tpu as pltpu

D_MODEL = 1024
DEPTH = 2
CHUNK = 128
A_GROUPS = 8
A_WIDTH = 512
A_HEAD = A_WIDTH // A_GROUPS
B_WIDTH = 512
CONV_WIDTH = 3
C_WIDTH = 512
POOL_WINDOWS = (2, 4, 8, 16)
C_GROUP = C_WIDTH // len(POOL_WINDOWS)
IN_TOTAL = 3 * A_WIDTH + 4 * B_WIDTH + 2 * C_WIDTH + 3 * D_MODEL
RMS_EPS = 1e-6
LN_EPS = 1e-5

_WIDTHS = [A_WIDTH] * 3 + [B_WIDTH] * 4 + [C_WIDTH] * 2 + [D_MODEL] * 3
_OFFS = [sum(_WIDTHS[:i]) for i in range(len(_WIDTHS) + 1)]
PROJ_SEGS = tuple((_OFFS[i], _OFFS[i + 1]) for i in range(len(_WIDTHS)))
(SEG_U, SEG_V, SEG_ZA, SEG_XB, SEG_BG, SEG_CG, SEG_ZB, SEG_XC, SEG_ZC,
 SEG_GA, SEG_GB, SEG_GC) = PROJ_SEGS
SEG_UV = (SEG_U[0], SEG_V[1])
PROJ_JOBS = ("p_za_cg", "p_zb_zc", "ga", "gb", "gc")
SEG_ZA_CG = (SEG_ZA[0], SEG_CG[1])
SEG_ZB_ZC = (SEG_ZB[0], SEG_ZC[1])
MXU_ORDER = ("p_za_cg", "sp", "gb", "p_zb_zc", "pa", "po", "ga", "pb", "gc",
             "u_next", "pc", "wo", "v_next")

LANES = 128
BF16_SUBLANES = 16
HALO = 16
TS = 512
W_ROWS = 64
N_STAGE = 3
O_ROWS = 256
VMEM_LIMIT_BYTES = 56 * 1024 * 1024


HALVED_SEGS = (SEG_U, SEG_V, SEG_ZA, SEG_ZB, SEG_ZC, SEG_GA, SEG_GB, SEG_GC)
_GELU_C1 = 2.0 * math.sqrt(2.0 / math.pi)
_GELU_C3 = 8.0 * 0.044715 * math.sqrt(2.0 / math.pi)


def _gelu_of_half(xh):
    t = jnp.tanh(xh * (_GELU_C1 + _GELU_C3 * (xh * xh)))
    return xh + xh * t


def _silu_of_half(zh):
    return zh + zh * jnp.tanh(zh)


def _twice_sigmoid_of_half(gh):
    return jnp.tanh(gh) + 1.0


def _rmsnorm_rows(x, gain):
    ms = jnp.mean(x * x, axis=-1, keepdims=True)
    return x * lax.rsqrt(ms + RMS_EPS) * gain


def _cast_w_in_rows(src, dst_ref, dst_rows):
    for lo, hi in PROJ_SEGS:
        w = src(lo, hi)
        if (lo, hi) in HALVED_SEGS:
            w = w * 0.5
        dst_ref[dst_rows, lo:hi] = w.astype(jnp.bfloat16)


def _w_in_copy(w_in_hbm_ref, stage_ref, sem_ref, layer, i):
    return pltpu.make_async_copy(
        w_in_hbm_ref.at[layer, pl.ds(i * W_ROWS, W_ROWS), :],
        stage_ref.at[i % N_STAGE], sem_ref.at[i % N_STAGE])


def _load_w_in(w_in_hbm_ref, w_in_ref, stage_ref, sem_ref, layer):
    n_blocks = D_MODEL // W_ROWS
    for i in range(N_STAGE):
        _w_in_copy(w_in_hbm_ref, stage_ref, sem_ref, layer, i).start()

    def convert_block(i, carry):
        _w_in_copy(w_in_hbm_ref, stage_ref, sem_ref, layer, i).wait()
        rows = pl.ds(pl.multiple_of(i * W_ROWS, W_ROWS), W_ROWS)
        _cast_w_in_rows(lambda lo, hi: stage_ref[i % N_STAGE, :, lo:hi], w_in_ref, rows)

        @pl.when(i + N_STAGE < n_blocks)
        def _():
            _w_in_copy(w_in_hbm_ref, stage_ref, sem_ref, layer, i + N_STAGE).start()

        return carry

    lax.fori_loop(0, n_blocks, convert_block, 0)


def _load_out_weights(hbm_refs, vmem_refs, stage_ref, sem_ref, layer):
    blocks = [(src, dst, r) for src, dst in zip(hbm_refs, vmem_refs)
              for r in range(0, dst.shape[0], O_ROWS)]

    def copy(i):
        src, _, r = blocks[i]
        return pltpu.make_async_copy(src.at[layer, pl.ds(r, O_ROWS), :],
                                     stage_ref.at[pl.ds((i % 2) * O_ROWS, O_ROWS), :],
                                     sem_ref.at[i % 2])

    copy(0).start()
    copy(1).start()
    for i, (_, dst, r) in enumerate(blocks):
        copy(i).wait()
        w = stage_ref[(i % 2) * O_ROWS:(i % 2 + 1) * O_ROWS, :]
        if dst is vmem_refs[-1]:
            w = w * 0.5
        dst[r:r + O_ROWS, :] = w.astype(jnp.bfloat16)
        if i + 2 < len(blocks):
            copy(i + 2).start()


class _NextLayerWIn:
    def __init__(self, w_in_hbm_ref, w_next_hbm_ref, stage_ref, cast_ref, sem_ref, layer,
                 step, n_steps):
        self.refs = (w_in_hbm_ref, w_next_hbm_ref, stage_ref, cast_ref, sem_ref)
        self.layer, self.step, self.n_steps = layer, step, n_steps
        self.rows = D_MODEL // n_steps

    def _rows_of(self, block):
        return pl.ds(pl.multiple_of(block * self.rows, self.rows), self.rows)

    def _fetch(self, block):
        w_in_hbm_ref, _, stage_ref, _, sem_ref = self.refs
        return pltpu.make_async_copy(w_in_hbm_ref.at[self.layer + 1, self._rows_of(block), :],
                                     stage_ref.at[block % 2, pl.ds(0, self.rows), :],
                                     sem_ref.at[block % 2])

    def _write_back(self, block):
        _, w_next_hbm_ref, _, cast_ref, sem_ref = self.refs
        return pltpu.make_async_copy(cast_ref.at[block % 2],
                                     w_next_hbm_ref.at[self._rows_of(block), :],
                                     sem_ref.at[2 + block % 2])

    def start_first_fetch(self):
        self._fetch(0).start()

    def begin_step(self):
        step = self.step
        self._fetch(step).wait()

        @pl.when(step + 1 < self.n_steps)
        def _():
            self._fetch(step + 1).start()

        @pl.when(step >= 2)
        def _():
            self._write_back(step - 2).wait()

    def cast(self):
        _, _, stage_ref, cast_ref, _ = self.refs
        slot = self.step % 2
        _cast_w_in_rows(lambda lo, hi: stage_ref[slot, 0:self.rows, lo:hi], cast_ref.at[slot],
                        slice(None))

    def end_step(self):
        step = self.step
        self._write_back(step).start()

        @pl.when(step == self.n_steps - 1)
        def _():
            self._write_back(step - 1).wait()
            self._write_back(step).wait()


def _prepare_small_weights(w_s_ref, b_s_ref, w_pool_ref, w_sp_ref, b_sp_ref, w_pool_bd_ref):
    f32 = jnp.float32
    bf16 = jnp.bfloat16
    row = lax.broadcasted_iota(jnp.int32, (CHUNK, CHUNK), 0)
    col = lax.broadcasted_iota(jnp.int32, (CHUNK, CHUNK), 1)
    causal = row >= col
    first_group = col < A_HEAD
    b_t = b_s_ref[...].T
    for jb in range(A_GROUPS // 2):
        w_pair = [jnp.where(causal, w_s_ref[2 * jb + k], 0.0) for k in range(2)]
        w_sp_ref[jb] = jnp.concatenate(w_pair, axis=1).astype(bf16)
        b_pair = [jnp.broadcast_to(b_t[:, 2 * jb + k:2 * jb + k + 1], (CHUNK, LANES))
                  for k in range(2)]
        b_sp_ref[:, jb * LANES:(jb + 1) * LANES] = jnp.where(first_group, b_pair[0], b_pair[1])
    zero = jnp.zeros((C_GROUP, C_GROUP), f32)
    for i in range(len(POOL_WINDOWS) // 2):
        top = jnp.concatenate([w_pool_ref[2 * i], zero], axis=1)
        bottom = jnp.concatenate([zero, w_pool_ref[2 * i + 1]], axis=1)
        w_pool_bd_ref[i] = jnp.concatenate([top, bottom], axis=0).astype(bf16)


def _layer_kernel(*refs, names, layer, steps_per_seq, n_steps, apply_final_norm):
    f32 = jnp.float32
    bf16 = jnp.bfloat16
    r = dict(zip(names, refs, strict=True))
    x_ref, x_next_ref, out_ref = r["x"], r["x_next"], r["out"]
    w_in_ref, sem_ref = r["w_in"], r["sem"]
    w_pa_ref, w_pb_ref, w_pc_ref, w_o_ref = r["w_pa"], r["w_pb"], r["w_pc"], r["w_o"]
    w_sp_ref, b_sp_ref, w_pool_bd_ref = r["w_sp"], r["b_sp"], r["w_pool_bd"]
    h_ref, puv_ref, cx_ref, xc_ref = r["h"], r["puv"], r["cx"], r["xc"]
    norm_g_ref, ln_g_ref, ln_b_ref = r["norm_g"], r["ln_g"], r["ln_b"]
    conv_w_ref, conv_b_ref, pool_scale_ref = r["conv_w"], r["conv_b"], r["pool_scale"]
    final_g_ref = r["final_g"]
    j = pl.program_id(1)
    step = pl.program_id(0) * steps_per_seq + j
    next_w_in = None
    if "w_in_next" in r:
        next_w_in = _NextLayerWIn(r["w_in_f32_hbm"], r["w_in_next"], r["stage"], r["cast"],
                                  r["next_sem"], layer, step, n_steps)

    this_layer = slice(layer, layer + 1)
    norm_g, final_g = norm_g_ref[this_layer, :], final_g_ref[...]
    ln_g, ln_b = ln_g_ref[this_layer, :], ln_b_ref[this_layer, :]
    conv_b, pool_scale = conv_b_ref[this_layer, :], pool_scale_ref[this_layer, :]
    conv_w = [conv_w_ref[layer, k:k + 1, :] for k in range(CONV_WIDTH)]

    @pl.when(j == 0)
    def _():
        cx_ref[0:HALO, :] = jnp.zeros((HALO, B_WIDTH), f32)
        xc_ref[0:HALO, :] = jnp.zeros((HALO, C_WIDTH), f32)

    def proj(seg):
        return jnp.dot(h_ref[...], w_in_ref[:, seg[0]:seg[1]], preferred_element_type=f32)

    @pl.when(step == 0)
    def _():
        if "w_in_bf16_hbm" in r:
            ready_w_in = pltpu.make_async_copy(r["w_in_bf16_hbm"], w_in_ref, sem_ref.at[2])
            ready_w_in.start()
        else:
            _load_w_in(r["w_in_f32_hbm"], w_in_ref, r["stage"], sem_ref, layer)
        _load_out_weights((r["w_pa_hbm"], r["w_pb_hbm"], r["w_pc_hbm"], r["w_o_hbm"]),
                          (w_pa_ref, w_pb_ref, w_pc_ref, w_o_ref), puv_ref, sem_ref, layer)
        _prepare_small_weights(r["w_s"], r["b_s"], r["w_pool"], w_sp_ref, b_sp_ref,
                               w_pool_bd_ref)
        h_ref[...] = _rmsnorm_rows(x_ref[0], norm_g).astype(bf16)
        if "w_in_bf16_hbm" in r:
            ready_w_in.wait()
        puv_ref[...] = proj(SEG_UV)
        if next_w_in is not None:
            next_w_in.start_first_fetch()

    if next_w_in is not None:
        next_w_in.begin_step()
        next_w_in.cast()

    n_chunks = TS // CHUNK
    lane = lax.broadcasted_iota(jnp.int32, (CHUNK, LANES), 1)
    first_group = lane < A_HEAD

    def spatial_mix(vn):
        sg_blocks = []
        for jb in range(A_WIDTH // LANES):
            rhs = []
            for c in range(n_chunks):
                vb = vn[c * CHUNK:(c + 1) * CHUNK, jb * LANES:(jb + 1) * LANES]
                rhs.append(jnp.concatenate([jnp.where(first_group, vb, 0.0),
                                            jnp.where(first_group, 0.0, vb)], axis=0))
            rhs = jnp.concatenate(rhs, axis=1).astype(bf16)
            mixed = jnp.dot(w_sp_ref[jb], rhs, preferred_element_type=f32)
            sg_blocks.append(jnp.concatenate(
                [mixed[:, c * LANES:(c + 1) * LANES] for c in range(n_chunks)], axis=0))
        sg = jnp.concatenate(sg_blocks, axis=1)
        return sg + jnp.concatenate([b_sp_ref[...]] * n_chunks, axis=0)

    def layer_norm(v):
        mu = jnp.mean(v, axis=-1, keepdims=True)
        vc = v - mu
        var = jnp.mean(vc * vc, axis=-1, keepdims=True)
        return vc * lax.rsqrt(var + LN_EPS) * ln_g + ln_b

    def part(e, job, job_seg, seg):
        lo = seg[0] - job_seg[0]
        return e[job][:, lo:lo + seg[1] - seg[0]]

    def store_xc(e):
        xc_ref[HALO:HALO + TS, :] = part(e, "p_zb_zc", SEG_ZB_ZC, SEG_XC)

    def store_cx(e):
        cx_ref[HALO:HALO + TS, :] = e["cg"] * e["xb"]

    def short_conv(e):
        conv = conv_b + conv_w[CONV_WIDTH - 1] * cx_ref[HALO:HALO + TS, :]
        for k in range(CONV_WIDTH - 1):
            back = CONV_WIDTH - 1 - k
            conv = conv + conv_w[k] * cx_ref[HALO - back:HALO - back + TS, :]
        cx_ref[0:HALO, :] = cx_ref[TS:TS + HALO, :]
        return conv

    def pooled_pairs(e):
        t1 = (j * TS + 1 + lax.broadcasted_iota(jnp.int32, (TS, C_GROUP), 0)).astype(f32)
        inv_t1 = 1.0 / t1
        pooled = []
        for gi, w in enumerate(POOL_WINDOWS):
            cols = slice(gi * C_GROUP, (gi + 1) * C_GROUP)
            ext = xc_ref[:, cols]
            win = ext
            span = 1
            while span < w:
                win = win + pltpu.roll(win, span, axis=0)
                span *= 2
            pooled.append(win[HALO:, :] * jnp.maximum(inv_t1, 1.0 / w) - ext[HALO:, :])
        xc_ref[0:HALO, :] = xc_ref[TS:TS + HALO, :]
        return [jnp.concatenate(pooled[2 * half:2 * half + 2], axis=1).astype(bf16)
                for half in range(2)]

    def pool_dots(e):
        groups = [jnp.dot(lhs, w_pool_bd_ref[half], preferred_element_type=f32)
                  for half, lhs in enumerate(e["pooled"])]
        return jnp.concatenate(groups, axis=1) * pool_scale

    def next_tile_u(e):
        h_ref[...] = _rmsnorm_rows(x_next_ref[0], norm_g).astype(bf16)
        puv_ref[:, 0:A_WIDTH] = proj(SEG_U)

    def next_tile_v(e):
        puv_ref[:, A_WIDTH:] = proj(SEG_V)

    def write_out(e):
        y = x_ref[0] + e["wo"]
        if apply_final_norm:
            y = _rmsnorm_rows(y, final_g)
        out_ref[0] = y

    def bdot(lhs, w_ref):
        return jnp.dot(lhs, w_ref[...], preferred_element_type=f32)

    matmul_jobs = {
        "p_za_cg": ((), lambda e: proj(SEG_ZA_CG)),
        "p_zb_zc": ((), lambda e: proj(SEG_ZB_ZC)),
        "ga": ((), lambda e: proj(SEG_GA)),
        "gb": ((), lambda e: proj(SEG_GB)),
        "gc": ((), lambda e: proj(SEG_GC)),
        "sp": (("vn",), lambda e: spatial_mix(e["vn"])),
        "pa": (("ya",), lambda e: bdot(e["ya"], w_pa_ref)),
        "pb": (("yb",), lambda e: bdot(e["yb"], w_pb_ref)),
        "po": (("pooled",), pool_dots),
        "pc": (("yc",), lambda e: bdot(e["yc"], w_pc_ref)),
        "u_next": (PROJ_JOBS, next_tile_u),
        "wo": (("merged_c",), lambda e: bdot(e["merged_c"].astype(bf16), w_o_ref)),
        "v_next": (("u_next",), next_tile_v),
    }
    elementwise = (
        ("za", ("p_za_cg",), lambda e: part(e, "p_za_cg", SEG_ZA_CG, SEG_ZA)),
        ("xb", ("p_za_cg",), lambda e: part(e, "p_za_cg", SEG_ZA_CG, SEG_XB)),
        ("bg", ("p_za_cg",), lambda e: part(e, "p_za_cg", SEG_ZA_CG, SEG_BG)),
        ("cg", ("p_za_cg",), lambda e: part(e, "p_za_cg", SEG_ZA_CG, SEG_CG)),
        ("zb", ("p_zb_zc",), lambda e: part(e, "p_zb_zc", SEG_ZB_ZC, SEG_ZB)),
        ("xc", ("p_zb_zc",), store_xc),
        ("zc", ("p_zb_zc",), lambda e: part(e, "p_zb_zc", SEG_ZB_ZC, SEG_ZC)),
        ("u", (), lambda e: _gelu_of_half(puv_ref[:, 0:A_WIDTH])),
        ("vn", (), lambda e: layer_norm(_gelu_of_half(puv_ref[:, A_WIDTH:]))),
        ("cx", ("cg", "xb"), store_cx),
        ("ya", ("u", "sp", "za"),
         lambda e: (e["u"] * e["sp"] * _silu_of_half(e["za"])).astype(bf16)),
        ("conv", ("cx",), short_conv),
        ("yb", ("bg", "conv", "zb"),
         lambda e: (e["bg"] * e["conv"] * _silu_of_half(e["zb"])).astype(bf16)),
        ("pooled", ("xc",), pooled_pairs),
        ("yc", ("po", "zc"), lambda e: (e["po"] * _silu_of_half(e["zc"])).astype(bf16)),
        ("merged_a", ("ga", "pa"), lambda e: _twice_sigmoid_of_half(e["ga"]) * e["pa"]),
        ("merged_b", ("merged_a", "gb", "pb"),
         lambda e: e["merged_a"] + _twice_sigmoid_of_half(e["gb"]) * e["pb"]),
        ("gate_c", ("gc",), lambda e: _twice_sigmoid_of_half(e["gc"])),
        ("merged_c", ("merged_b", "gate_c", "pc"),
         lambda e: e["merged_b"] + e["gate_c"] * e["pc"]),
        ("out", ("wo",), write_out),
    )
    env = {}

    def trace_ready_elementwise():
        progress = True
        while progress:
            progress = False
            for name, needs, fn in elementwise:
                if name not in env and all(n in env for n in needs):
                    env[name] = fn(env)
                    progress = True

    assert sorted(MXU_ORDER) == sorted(matmul_jobs)
    for name in MXU_ORDER:
        trace_ready_elementwise()
        needs, fn = matmul_jobs[name]
        assert all(n in env for n in needs), (name, needs)
        env[name] = fn(env)
    trace_ready_elementwise()
    assert len(env) == len(matmul_jobs) + len(elementwise)
    if next_w_in is not None:
        next_w_in.end_step()


def _whole_spec(array):
    zeros = (0,) * array.ndim
    return pl.BlockSpec(array.shape, lambda b, j: zeros, pipeline_mode=pl.Buffered(1))


def _layer_spec(array, layer):
    zeros = (0,) * (array.ndim - 1)
    return pl.BlockSpec((None,) + array.shape[1:], lambda b, j: (layer,) + zeros,
                        pipeline_mode=pl.Buffered(1))


VECTOR_NAMES = ("norm_g", "ln_g", "ln_b", "conv_w", "conv_b", "pool_scale", "final_g")
PER_LAYER_NAMES = ("w_s", "b_s", "w_pool")
OUT_WEIGHT_NAMES = ("w_pa", "w_pb", "w_pc", "w_o")


def _layer(x, w_in, w_in_bf16, vectors, per_layer, out_weights, layer, *, convert_next,
           apply_final_norm):
    bsz, seq, _ = x.shape
    assert seq % TS == 0 and TS % CHUNK == 0 and TS >= HALO
    assert w_in.dtype == jnp.float32 and w_in.shape[1:] == (D_MODEL, IN_TOTAL)
    assert D_MODEL % W_ROWS == 0 and TS == 2 * O_ROWS and N_STAGE >= 3
    steps_per_seq = seq // TS
    n_steps = bsz * steps_per_seq
    bf16 = jnp.bfloat16
    f32 = jnp.float32

    def next_tile(b, j):
        s = jnp.minimum(b * steps_per_seq + j + 1, n_steps - 1)
        return (s // steps_per_seq, s % steps_per_seq, 0)

    tile = (1, TS, D_MODEL)
    any_spec = pl.BlockSpec(memory_space=pl.ANY)
    inputs = [("x", x, pl.BlockSpec(tile, lambda b, j: (b, j, 0))),
              ("x_next", x, pl.BlockSpec(tile, next_tile)),
              ("w_in_f32_hbm", w_in, any_spec)]
    if w_in_bf16 is not None:
        inputs.append(("w_in_bf16_hbm", w_in_bf16, any_spec))
    inputs += [(n, a, _whole_spec(a)) for n, a in zip(VECTOR_NAMES, vectors, strict=True)]
    inputs += [(n, a, _layer_spec(a, layer))
               for n, a in zip(PER_LAYER_NAMES, per_layer, strict=True)]
    inputs += [(n + "_hbm", a, any_spec)
               for n, a in zip(OUT_WEIGHT_NAMES, out_weights, strict=True)]

    outputs = [("out", jax.ShapeDtypeStruct(x.shape, x.dtype),
                pl.BlockSpec(tile, lambda b, j: (b, j, 0)))]
    scratch = [
        ("w_in", pltpu.VMEM((D_MODEL, IN_TOTAL), bf16)),
        ("sem", pltpu.SemaphoreType.DMA((N_STAGE,))),
        ("w_pa", pltpu.VMEM((A_WIDTH, D_MODEL), bf16)),
        ("w_pb", pltpu.VMEM((B_WIDTH, D_MODEL), bf16)),
        ("w_pc", pltpu.VMEM((C_WIDTH, D_MODEL), bf16)),
        ("w_o", pltpu.VMEM((D_MODEL, D_MODEL), bf16)),
        ("w_sp", pltpu.VMEM((A_GROUPS // 2, CHUNK, 2 * CHUNK), bf16)),
        ("b_sp", pltpu.VMEM((CHUNK, A_WIDTH), f32)),
        ("w_pool_bd", pltpu.VMEM((len(POOL_WINDOWS) // 2, 2 * C_GROUP, 2 * C_GROUP), bf16)),
        ("h", pltpu.VMEM((TS, D_MODEL), bf16)),
        ("puv", pltpu.VMEM((TS, 2 * A_WIDTH), f32)),
        ("cx", pltpu.VMEM((TS + HALO, B_WIDTH), f32)),
        ("xc", pltpu.VMEM((TS + HALO, C_WIDTH), f32)),
    ]
    if w_in_bf16 is None or convert_next:
        scratch.append(("stage", pltpu.VMEM((N_STAGE, W_ROWS, IN_TOTAL), f32)))
    if convert_next:
        assert D_MODEL % n_steps == 0 and (D_MODEL // n_steps) % BF16_SUBLANES == 0
        assert D_MODEL // n_steps <= W_ROWS and n_steps >= 2
        outputs.append(("w_in_next", jax.ShapeDtypeStruct((D_MODEL, IN_TOTAL), bf16), any_spec))
        scratch += [("cast", pltpu.VMEM((2, D_MODEL // n_steps, IN_TOTAL), bf16)),
                    ("next_sem", pltpu.SemaphoreType.DMA((4,)))]

    names = tuple(n for n, *_ in inputs + outputs + scratch)
    results = pl.pallas_call(
        functools.partial(_layer_kernel, names=names, layer=layer, steps_per_seq=steps_per_seq,
                          n_steps=n_steps, apply_final_norm=apply_final_norm),
        out_shape=[s for _, s, _ in outputs],
        grid=(bsz, steps_per_seq),
        in_specs=[spec for _, _, spec in inputs],
        out_specs=[spec for _, _, spec in outputs],
        scratch_shapes=[s for _, s in scratch],
        compiler_params=pltpu.CompilerParams(
            dimension_semantics=("arbitrary", "arbitrary"),
            vmem_limit_bytes=VMEM_LIMIT_BYTES,
        ),
        name="hybrid_layer_final" if apply_final_norm else "hybrid_layer",
    )(*[a for _, a, _ in inputs])
    return results[0], (results[1] if convert_next else None)


def kernel(x, norm_g, w_in, ln_g, ln_b, w_s, b_s, conv_w, conv_b, w_pool, pool_scale,
           w_pa, w_pb, w_pc, w_o, final_g):
    vectors = (norm_g, ln_g, ln_b, conv_w, conv_b, pool_scale, final_g[None, :])
    per_layer = (w_s, b_s, w_pool)
    out_weights = (w_pa, w_pb, w_pc, w_o)
    w_in_bf16 = None
    for layer in range(DEPTH):
        last = layer == DEPTH - 1
        x, w_in_bf16 = _layer(x, w_in, w_in_bf16, vectors, per_layer, out_weights, layer,
                              convert_next=not last, apply_final_norm=last)
    return x
```

```python
import functools
import math

import jax
import jax.numpy as jnp
from jax import lax
from jax.experimental import pallas as pl
from jax.experimental.pallas import tpu as pltpu

D_MODEL = 1024
DEPTH = 2
CHUNK = 128
A_GROUPS = 8
A_WIDTH = 512
A_HEAD = A_WIDTH // A_GROUPS
B_WIDTH = 512
CONV_WIDTH = 3
C_WIDTH = 512
POOL_WINDOWS = (2, 4, 8, 16)
C_GROUP = C_WIDTH // len(POOL_WINDOWS)
IN_TOTAL = 3 * A_WIDTH + 4 * B_WIDTH + 2 * C_WIDTH + 3 * D_MODEL
RMS_EPS = 1e-6
LN_EPS = 1e-5

_WIDTHS = [A_WIDTH] * 3 + [B_WIDTH] * 4 + [C_WIDTH] * 2 + [D_MODEL] * 3
_OFFS = [sum(_WIDTHS[:i]) for i in range(len(_WIDTHS) + 1)]
PROJ_SEGS = tuple((_OFFS[i], _OFFS[i + 1]) for i in range(len(_WIDTHS)))
(SEG_U, SEG_V, SEG_ZA, SEG_XB, SEG_BG, SEG_CG, SEG_ZB, SEG_XC, SEG_ZC,
 SEG_GA, SEG_GB, SEG_GC) = PROJ_SEGS
SEG_UV = (SEG_U[0], SEG_V[1])
PROJ_JOBS = ("xc", "za", "cg", "xb", "bg", "zb", "zc", "ga", "gb", "gc")
MXU_ORDER = ("gb", "za", "cg", "xb", "sp", "bg", "xc", "zb", "pa", "zc", "po", "ga", "pb", "gc",
             "u_next", "pc", "wo", "v_next")

LANES = 128
BF16_SUBLANES = 16
HALO = 16
TS = 512
W_ROWS = 64
N_STAGE = 3
O_ROWS = 256
VMEM_LIMIT_BYTES = 56 * 1024 * 1024


HALVED_SEGS = (SEG_U, SEG_V, SEG_ZA, SEG_ZB, SEG_ZC, SEG_GA, SEG_GB, SEG_GC)
_GELU_C1 = 2.0 * math.sqrt(2.0 / math.pi)
_GELU_C3 = 8.0 * 0.044715 * math.sqrt(2.0 / math.pi)


def _gelu_of_half(xh):
    t = jnp.tanh(xh * (_GELU_C1 + _GELU_C3 * (xh * xh)))
    return xh + xh * t


def _silu_of_half(zh):
    return zh + zh * jnp.tanh(zh)


def _twice_sigmoid_of_half(gh):
    return jnp.tanh(gh) + 1.0


def _rmsnorm_rows(x, gain):
    ms = jnp.mean(x * x, axis=-1, keepdims=True)
    return x * lax.rsqrt(ms + RMS_EPS) * gain


def _cast_w_in_rows(src, dst_ref, dst_rows):
    for lo, hi in PROJ_SEGS:
        w = src(lo, hi)
        if (lo, hi) in HALVED_SEGS:
            w = w * 0.5
        dst_ref[dst_rows, lo:hi] = w.astype(jnp.bfloat16)


def _w_in_copy(w_in_hbm_ref, stage_ref, sem_ref, layer, i):
    return pltpu.make_async_copy(
        w_in_hbm_ref.at[layer, pl.ds(i * W_ROWS, W_ROWS), :],
        stage_ref.at[i % N_STAGE], sem_ref.at[i % N_STAGE])


def _load_w_in(w_in_hbm_ref, w_in_ref, stage_ref, sem_ref, layer):
    n_blocks = D_MODEL // W_ROWS
    for i in range(N_STAGE):
        _w_in_copy(w_in_hbm_ref, stage_ref, sem_ref, layer, i).start()

    def convert_block(i, carry):
        _w_in_copy(w_in_hbm_ref, stage_ref, sem_ref, layer, i).wait()
        rows = pl.ds(pl.multiple_of(i * W_ROWS, W_ROWS), W_ROWS)
        _cast_w_in_rows(lambda lo, hi: stage_ref[i % N_STAGE, :, lo:hi], w_in_ref, rows)

        @pl.when(i + N_STAGE < n_blocks)
        def _():
            _w_in_copy(w_in_hbm_ref, stage_ref, sem_ref, layer, i + N_STAGE).start()

        return carry

    lax.fori_loop(0, n_blocks, convert_block, 0)


def _load_out_weights(hbm_refs, vmem_refs, stage_ref, sem_ref, layer):
    blocks = [(src, dst, r) for src, dst in zip(hbm_refs, vmem_refs)
              for r in range(0, dst.shape[0], O_ROWS)]

    def copy(i):
        src, _, r = blocks[i]
        return pltpu.make_async_copy(src.at[layer, pl.ds(r, O_ROWS), :],
                                     stage_ref.at[pl.ds((i % 2) * O_ROWS, O_ROWS), :],
                                     sem_ref.at[i % 2])

    copy(0).start()
    copy(1).start()
    for i, (_, dst, r) in enumerate(blocks):
        copy(i).wait()
        w = stage_ref[(i % 2) * O_ROWS:(i % 2 + 1) * O_ROWS, :]
        if dst is vmem_refs[-1]:
            w = w * 0.5
        dst[r:r + O_ROWS, :] = w.astype(jnp.bfloat16)
        if i + 2 < len(blocks):
            copy(i + 2).start()


class _NextLayerWIn:
    def __init__(self, w_in_hbm_ref, w_next_hbm_ref, stage_ref, cast_ref, sem_ref, layer,
                 step, n_steps):
        self.refs = (w_in_hbm_ref, w_next_hbm_ref, stage_ref, cast_ref, sem_ref)
        self.layer, self.step, self.n_steps = layer, step, n_steps
        self.rows = D_MODEL // n_steps

    def _rows_of(self, block):
        return pl.ds(pl.multiple_of(block * self.rows, self.rows), self.rows)

    def _fetch(self, block):
        w_in_hbm_ref, _, stage_ref, _, sem_ref = self.refs
        return pltpu.make_async_copy(w_in_hbm_ref.at[self.layer + 1, self._rows_of(block), :],
                                     stage_ref.at[block % 2, pl.ds(0, self.rows), :],
                                     sem_ref.at[block % 2])

    def _write_back(self, block):
        _, w_next_hbm_ref, _, cast_ref, sem_ref = self.refs
        return pltpu.make_async_copy(cast_ref.at[block % 2],
                                     w_next_hbm_ref.at[self._rows_of(block), :],
                                     sem_ref.at[2 + block % 2])

    def start_first_fetch(self):
        self._fetch(0).start()

    def begin_step(self):
        step = self.step
        self._fetch(step).wait()

        @pl.when(step + 1 < self.n_steps)
        def _():
            self._fetch(step + 1).start()

        @pl.when(step >= 2)
        def _():
            self._write_back(step - 2).wait()

    def cast(self):
        _, _, stage_ref, cast_ref, _ = self.refs
        slot = self.step % 2
        _cast_w_in_rows(lambda lo, hi: stage_ref[slot, 0:self.rows, lo:hi], cast_ref.at[slot],
                        slice(None))

    def end_step(self):
        step = self.step
        self._write_back(step).start()

        @pl.when(step == self.n_steps - 1)
        def _():
            self._write_back(step - 1).wait()
            self._write_back(step).wait()


def _prepare_small_weights(w_s_ref, b_s_ref, w_pool_ref, w_sp_ref, b_sp_ref, w_pool_bd_ref):
    f32 = jnp.float32
    bf16 = jnp.bfloat16
    row = lax.broadcasted_iota(jnp.int32, (CHUNK, CHUNK), 0)
    col = lax.broadcasted_iota(jnp.int32, (CHUNK, CHUNK), 1)
    causal = row >= col
    first_group = col < A_HEAD
    b_t = b_s_ref[...].T
    for jb in range(A_GROUPS // 2):
        w_pair = [jnp.where(causal, w_s_ref[2 * jb + k], 0.0) for k in range(2)]
        w_sp_ref[jb] = jnp.concatenate(w_pair, axis=1).astype(bf16)
        b_pair = [jnp.broadcast_to(b_t[:, 2 * jb + k:2 * jb + k + 1], (CHUNK, LANES))
                  for k in range(2)]
        b_sp_ref[:, jb * LANES:(jb + 1) * LANES] = jnp.where(first_group, b_pair[0], b_pair[1])
    zero = jnp.zeros((C_GROUP, C_GROUP), f32)
    for i in range(len(POOL_WINDOWS) // 2):
        top = jnp.concatenate([w_pool_ref[2 * i], zero], axis=1)
        bottom = jnp.concatenate([zero, w_pool_ref[2 * i + 1]], axis=1)
        w_pool_bd_ref[i] = jnp.concatenate([top, bottom], axis=0).astype(bf16)


def _layer_kernel(*refs, names, layer, steps_per_seq, n_steps, apply_final_norm):
    f32 = jnp.float32
    bf16 = jnp.bfloat16
    r = dict(zip(names, refs, strict=True))
    x_ref, x_next_ref, out_ref = r["x"], r["x_next"], r["out"]
    w_in_ref, sem_ref = r["w_in"], r["sem"]
    w_pa_ref, w_pb_ref, w_pc_ref, w_o_ref = r["w_pa"], r["w_pb"], r["w_pc"], r["w_o"]
    w_sp_ref, b_sp_ref, w_pool_bd_ref = r["w_sp"], r["b_sp"], r["w_pool_bd"]
    h_ref, puv_ref, cx_ref, xc_ref = r["h"], r["puv"], r["cx"], r["xc"]
    norm_g_ref, ln_g_ref, ln_b_ref = r["norm_g"], r["ln_g"], r["ln_b"]
    conv_w_ref, conv_b_ref, pool_scale_ref = r["conv_w"], r["conv_b"], r["pool_scale"]
    final_g_ref = r["final_g"]
    j = pl.program_id(1)
    step = pl.program_id(0) * steps_per_seq + j
    next_w_in = None
    if "w_in_next" in r:
        next_w_in = _NextLayerWIn(r["w_in_f32_hbm"], r["w_in_next"], r["stage"], r["cast"],
                                  r["next_sem"], layer, step, n_steps)

    this_layer = slice(layer, layer + 1)
    norm_g, final_g = norm_g_ref[this_layer, :], final_g_ref[...]
    ln_g, ln_b = ln_g_ref[this_layer, :], ln_b_ref[this_layer, :]
    conv_b, pool_scale = conv_b_ref[this_layer, :], pool_scale_ref[this_layer, :]
    conv_w = [conv_w_ref[layer, k:k + 1, :] for k in range(CONV_WIDTH)]

    @pl.when(j == 0)
    def _():
        cx_ref[0:HALO, :] = jnp.zeros((HALO, B_WIDTH), f32)
        xc_ref[0:HALO, :] = jnp.zeros((HALO, C_WIDTH), f32)

    def proj(seg):
        return jnp.dot(h_ref[...], w_in_ref[:, seg[0]:seg[1]], preferred_element_type=f32)

    @pl.when(step == 0)
    def _():
        if "w_in_bf16_hbm" in r:
            ready_w_in = pltpu.make_async_copy(r["w_in_bf16_hbm"], w_in_ref, sem_ref.at[2])
            ready_w_in.start()
        else:
            _load_w_in(r["w_in_f32_hbm"], w_in_ref, r["stage"], sem_ref, layer)
        _load_out_weights((r["w_pa_hbm"], r["w_pb_hbm"], r["w_pc_hbm"], r["w_o_hbm"]),
                          (w_pa_ref, w_pb_ref, w_pc_ref, w_o_ref), puv_ref, sem_ref, layer)
        _prepare_small_weights(r["w_s"], r["b_s"], r["w_pool"], w_sp_ref, b_sp_ref,
                               w_pool_bd_ref)
        h_ref[...] = _rmsnorm_rows(x_ref[0], norm_g).astype(bf16)
        if "w_in_bf16_hbm" in r:
            ready_w_in.wait()
        puv_ref[...] = proj(SEG_UV)
        if next_w_in is not None:
            next_w_in.start_first_fetch()

    if next_w_in is not None:
        next_w_in.begin_step()
        next_w_in.cast()

    n_chunks = TS // CHUNK
    lane = lax.broadcasted_iota(jnp.int32, (CHUNK, LANES), 1)
    first_group = lane < A_HEAD

    def spatial_mix(vn):
        sg_blocks = []
        for jb in range(A_WIDTH // LANES):
            rhs = []
            for c in range(n_chunks):
                vb = vn[c * CHUNK:(c + 1) * CHUNK, jb * LANES:(jb + 1) * LANES]
                rhs.append(jnp.concatenate([jnp.where(first_group, vb, 0.0),
                                            jnp.where(first_group, 0.0, vb)], axis=0))
            rhs = jnp.concatenate(rhs, axis=1).astype(bf16)
            mixed = jnp.dot(w_sp_ref[jb], rhs, preferred_element_type=f32)
            sg_blocks.append(jnp.concatenate(
                [mixed[:, c * LANES:(c + 1) * LANES] for c in range(n_chunks)], axis=0))
        sg = jnp.concatenate(sg_blocks, axis=1)
        return sg + jnp.concatenate([b_sp_ref[...]] * n_chunks, axis=0)

    def layer_norm(v):
        mu = jnp.mean(v, axis=-1, keepdims=True)
        vc = v - mu
        var = jnp.mean(vc * vc, axis=-1, keepdims=True)
        return vc * lax.rsqrt(var + LN_EPS) * ln_g + ln_b

    def store_xc(e):
        xc_ref[HALO:HALO + TS, :] = proj(SEG_XC)

    def store_cx(e):
        cx_ref[HALO:HALO + TS, :] = e["cg"] * e["xb"]

    def short_conv(e):
        conv = conv_b + conv_w[CONV_WIDTH - 1] * cx_ref[HALO:HALO + TS, :]
        for k in range(CONV_WIDTH - 1):
            back = CONV_WIDTH - 1 - k
            conv = conv + conv_w[k] * cx_ref[HALO - back:HALO - back + TS, :]
        cx_ref[0:HALO, :] = cx_ref[TS:TS + HALO, :]
        return conv

    def pooled_pairs(e):
        t1 = (j * TS + 1 + lax.broadcasted_iota(jnp.int32, (TS, C_GROUP), 0)).astype(f32)
        inv_t1 = 1.0 / t1
        pooled = []
        for gi, w in enumerate(POOL_WINDOWS):
            cols = slice(gi * C_GROUP, (gi + 1) * C_GROUP)
            ext = xc_ref[:, cols]
            win = ext
            span = 1
            while span < w:
                win = win + pltpu.roll(win, span, axis=0)
                span *= 2
            pooled.append(win[HALO:, :] * jnp.maximum(inv_t1, 1.0 / w) - ext[HALO:, :])
        xc_ref[0:HALO, :] = xc_ref[TS:TS + HALO, :]
        return [jnp.concatenate(pooled[2 * half:2 * half + 2], axis=1).astype(bf16)
                for half in range(2)]

    def pool_dots(e):
        groups = [jnp.dot(lhs, w_pool_bd_ref[half], preferred_element_type=f32)
                  for half, lhs in enumerate(e["pooled"])]
        return jnp.concatenate(groups, axis=1) * pool_scale

    def next_tile_u(e):
        h_ref[...] = _rmsnorm_rows(x_next_ref[0], norm_g).astype(bf16)
        puv_ref[:, 0:A_WIDTH] = proj(SEG_U)

    def next_tile_v(e):
        puv_ref[:, A_WIDTH:] = proj(SEG_V)

    def write_out(e):
        y = x_ref[0] + e["wo"]
        if apply_final_norm:
            y = _rmsnorm_rows(y, final_g)
        out_ref[0] = y

    def bdot(lhs, w_ref):
        return jnp.dot(lhs, w_ref[...], preferred_element_type=f32)

    matmul_jobs = {
        "xc": ((), store_xc),
        "za": ((), lambda e: proj(SEG_ZA)),
        "cg": ((), lambda e: proj(SEG_CG)),
        "xb": ((), lambda e: proj(SEG_XB)),
        "bg": ((), lambda e: proj(SEG_BG)),
        "zb": ((), lambda e: proj(SEG_ZB)),
        "zc": ((), lambda e: proj(SEG_ZC)),
        "ga": ((), lambda e: proj(SEG_GA)),
        "gb": ((), lambda e: proj(SEG_GB)),
        "gc": ((), lambda e: proj(SEG_GC)),
        "sp": (("vn",), lambda e: spatial_mix(e["vn"])),
        "pa": (("ya",), lambda e: bdot(e["ya"], w_pa_ref)),
        "pb": (("yb",), lambda e: bdot(e["yb"], w_pb_ref)),
        "po": (("pooled",), pool_dots),
        "pc": (("yc",), lambda e: bdot(e["yc"], w_pc_ref)),
        "u_next": (PROJ_JOBS, next_tile_u),
        "wo": (("merged_c",), lambda e: bdot(e["merged_c"].astype(bf16), w_o_ref)),
        "v_next": (("u_next",), next_tile_v),
    }
    elementwise = (
        ("u", (), lambda e: _gelu_of_half(puv_ref[:, 0:A_WIDTH])),
        ("vn", (), lambda e: layer_norm(_gelu_of_half(puv_ref[:, A_WIDTH:]))),
        ("cx", ("cg", "xb"), store_cx),
        ("ya", ("u", "sp", "za"),
         lambda e: (e["u"] * e["sp"] * _silu_of_half(e["za"])).astype(bf16)),
        ("conv", ("cx",), short_conv),
        ("yb", ("bg", "conv", "zb"),
         lambda e: (e["bg"] * e["conv"] * _silu_of_half(e["zb"])).astype(bf16)),
        ("pooled", ("xc",), pooled_pairs),
        ("yc", ("po", "zc"), lambda e: (e["po"] * _silu_of_half(e["zc"])).astype(bf16)),
        ("merged_a", ("ga", "pa"), lambda e: _twice_sigmoid_of_half(e["ga"]) * e["pa"]),
        ("merged_b", ("merged_a", "gb", "pb"),
         lambda e: e["merged_a"] + _twice_sigmoid_of_half(e["gb"]) * e["pb"]),
        ("gate_c", ("gc",), lambda e: _twice_sigmoid_of_half(e["gc"])),
        ("merged_c", ("merged_b", "gate_c", "pc"),
         lambda e: e["merged_b"] + e["gate_c"] * e["pc"]),
        ("out", ("wo",), write_out),
    )
    env = {}

    def trace_ready_elementwise():
        progress = True
        while progress:
            progress = False
            for name, needs, fn in elementwise:
                if name not in env and all(n in env for n in needs):
                    env[name] = fn(env)
                    progress = True

    assert sorted(MXU_ORDER) == sorted(matmul_jobs)
    for name in MXU_ORDER:
        trace_ready_elementwise()
        needs, fn = matmul_jobs[name]
        assert all(n in env for n in needs), (name, needs)
        env[name] = fn(env)
    trace_ready_elementwise()
    assert len(env) == len(matmul_jobs) + len(elementwise)
    if next_w_in is not None:
        next_w_in.end_step()


def _whole_spec(array):
    zeros = (0,) * array.ndim
    return pl.BlockSpec(array.shape, lambda b, j: zeros, pipeline_mode=pl.Buffered(1))


def _layer_spec(array, layer):
    zeros = (0,) * (array.ndim - 1)
    return pl.BlockSpec((None,) + array.shape[1:], lambda b, j: (layer,) + zeros,
                        pipeline_mode=pl.Buffered(1))


VECTOR_NAMES = ("norm_g", "ln_g", "ln_b", "conv_w", "conv_b", "pool_scale", "final_g")
PER_LAYER_NAMES = ("w_s", "b_s", "w_pool")
OUT_WEIGHT_NAMES = ("w_pa", "w_pb", "w_pc", "w_o")


def _layer(x, w_in, w_in_bf16, vectors, per_layer, out_weights, layer, *, convert_next,
           apply_final_norm):
    bsz, seq, _ = x.shape
    assert seq % TS == 0 and TS % CHUNK == 0 and TS >= HALO
    assert w_in.dtype == jnp.float32 and w_in.shape[1:] == (D_MODEL, IN_TOTAL)
    assert D_MODEL % W_ROWS == 0 and TS == 2 * O_ROWS and N_STAGE >= 3
    steps_per_seq = seq // TS
    n_steps = bsz * steps_per_seq
    bf16 = jnp.bfloat16
    f32 = jnp.float32

    def next_tile(b, j):
        s = jnp.minimum(b * steps_per_seq + j + 1, n_steps - 1)
        return (s // steps_per_seq, s % steps_per_seq, 0)

    tile = (1, TS, D_MODEL)
    any_spec = pl.BlockSpec(memory_space=pl.ANY)
    inputs = [("x", x, pl.BlockSpec(tile, lambda b, j: (b, j, 0))),
              ("x_next", x, pl.BlockSpec(tile, next_tile)),
              ("w_in_f32_hbm", w_in, any_spec)]
    if w_in_bf16 is not None:
        inputs.append(("w_in_bf16_hbm", w_in_bf16, any_spec))
    inputs += [(n, a, _whole_spec(a)) for n, a in zip(VECTOR_NAMES, vectors, strict=True)]
    inputs += [(n, a, _layer_spec(a, layer))
               for n, a in zip(PER_LAYER_NAMES, per_layer, strict=True)]
    inputs += [(n + "_hbm", a, any_spec)
               for n, a in zip(OUT_WEIGHT_NAMES, out_weights, strict=True)]

    outputs = [("out", jax.ShapeDtypeStruct(x.shape, x.dtype),
                pl.BlockSpec(tile, lambda b, j: (b, j, 0)))]
    scratch = [
        ("w_in", pltpu.VMEM((D_MODEL, IN_TOTAL), bf16)),
        ("sem", pltpu.SemaphoreType.DMA((N_STAGE,))),
        ("w_pa", pltpu.VMEM((A_WIDTH, D_MODEL), bf16)),
        ("w_pb", pltpu.VMEM((B_WIDTH, D_MODEL), bf16)),
        ("w_pc", pltpu.VMEM((C_WIDTH, D_MODEL), bf16)),
        ("w_o", pltpu.VMEM((D_MODEL, D_MODEL), bf16)),
        ("w_sp", pltpu.VMEM((A_GROUPS // 2, CHUNK, 2 * CHUNK), bf16)),
        ("b_sp", pltpu.VMEM((CHUNK, A_WIDTH), f32)),
        ("w_pool_bd", pltpu.VMEM((len(POOL_WINDOWS) // 2, 2 * C_GROUP, 2 * C_GROUP), bf16)),
        ("h", pltpu.VMEM((TS, D_MODEL), bf16)),
        ("puv", pltpu.VMEM((TS, 2 * A_WIDTH), f32)),
        ("cx", pltpu.VMEM((TS + HALO, B_WIDTH), f32)),
        ("xc", pltpu.VMEM((TS + HALO, C_WIDTH), f32)),
    ]
    if w_in_bf16 is None or convert_next:
        scratch.append(("stage", pltpu.VMEM((N_STAGE, W_ROWS, IN_TOTAL), f32)))
    if convert_next:
        assert D_MODEL % n_steps == 0 and (D_MODEL // n_steps) % BF16_SUBLANES == 0
        assert D_MODEL // n_steps <= W_ROWS and n_steps >= 2
        outputs.append(("w_in_next", jax.ShapeDtypeStruct((D_MODEL, IN_TOTAL), bf16), any_spec))
        scratch += [("cast", pltpu.VMEM((2, D_MODEL // n_steps, IN_TOTAL), bf16)),
                    ("next_sem", pltpu.SemaphoreType.DMA((4,)))]

    names = tuple(n for n, *_ in inputs + outputs + scratch)
    results = pl.pallas_call(
        functools.partial(_layer_kernel, names=names, layer=layer, steps_per_seq=steps_per_seq,
                          n_steps=n_steps, apply_final_norm=apply_final_norm),
        out_shape=[s for _, s, _ in outputs],
        grid=(bsz, steps_per_seq),
        in_specs=[spec for _, _, spec in inputs],
        out_specs=[spec for _, _, spec in outputs],
        scratch_shapes=[s for _, s in scratch],
        compiler_params=pltpu.CompilerParams(
            dimension_semantics=("arbitrary", "arbitrary"),
            vmem_limit_bytes=VMEM_LIMIT_BYTES,
        ),
        name="hybrid_layer_final" if apply_final_norm else "hybrid_layer",
    )(*[a for _, a, _ in inputs])
    return results[0], (results[1] if convert_next else None)


def kernel(x, norm_g, w_in, ln_g, ln_b, w_s, b_s, conv_w, conv_b, w_pool, pool_scale,
           w_pa, w_pb, w_pc, w_o, final_g):
    vectors = (norm_g, ln_g, ln_b, conv_w, conv_b, pool_scale, final_g[None, :])
    per_layer = (w_s, b_s, w_pool)
    out_weights = (w_pa, w_pb, w_pc, w_o)
    w_in_bf16 = None
    for layer in range(DEPTH):
        last = layer == DEPTH - 1
        x, w_in_bf16 = _layer(x, w_in, w_in_bf16, vectors, per_layer, out_weights, layer,
                              convert_next=not last, apply_final_norm=last)
    return x
```

```python
import functools
import math

import jax
import jax.numpy as jnp
from jax import lax
from jax.experimental import pallas as pl
from jax.experimental.pallas import tpu as pltpu

D_MODEL = 1024
DEPTH = 2
CHUNK = 128
A_GROUPS = 8
A_WIDTH = 512
A_HEAD = A_WIDTH // A_GROUPS
B_WIDTH = 512
CONV_WIDTH = 3
C_WIDTH = 512
POOL_WINDOWS = (2, 4, 8, 16)
C_GROUP = C_WIDTH // len(POOL_WINDOWS)
IN_TOTAL = 3 * A_WIDTH + 4 * B_WIDTH + 2 * C_WIDTH + 3 * D_MODEL
RMS_EPS = 1e-6
LN_EPS = 1e-5

_WIDTHS = [A_WIDTH] * 3 + [B_WIDTH] * 4 + [C_WIDTH] * 2 + [D_MODEL] * 3
_OFFS = [sum(_WIDTHS[:i]) for i in range(len(_WIDTHS) + 1)]
PROJ_SEGS = tuple((_OFFS[i], _OFFS[i + 1]) for i in range(len(_WIDTHS)))
(SEG_U, SEG_V, SEG_ZA, SEG_XB, SEG_BG, SEG_CG, SEG_ZB, SEG_XC, SEG_ZC,
 SEG_GA, SEG_GB, SEG_GC) = PROJ_SEGS
SEG_UV = (SEG_U[0], SEG_V[1])
PROJ_JOBS = ("xc", "za", "cg", "xb", "bg", "zb", "zc", "ga", "gb", "gc")
MXU_ORDER = ("za", "cg", "xb", "sp", "gb", "bg", "xc", "zb", "pa", "zc", "po", "ga", "pb", "gc",
             "u_next", "pc", "wo", "v_next")

LANES = 128
BF16_SUBLANES = 16
HALO = 16
TS = 512
W_ROWS = 64
N_STAGE = 3
O_ROWS = 256
VMEM_LIMIT_BYTES = 56 * 1024 * 1024


HALVED_SEGS = (SEG_U, SEG_V, SEG_ZA, SEG_ZB, SEG_ZC, SEG_GA, SEG_GB, SEG_GC)
_GELU_C1 = 2.0 * math.sqrt(2.0 / math.pi)
_GELU_C3 = 8.0 * 0.044715 * math.sqrt(2.0 / math.pi)


def _gelu_of_half(xh):
    t = jnp.tanh(xh * (_GELU_C1 + _GELU_C3 * (xh * xh)))
    return xh + xh * t


def _silu_of_half(zh):
    return zh + zh * jnp.tanh(zh)


def _twice_sigmoid_of_half(gh):
    return jnp.tanh(gh) + 1.0


def _rmsnorm_rows(x, gain):
    ms = jnp.mean(x * x, axis=-1, keepdims=True)
    return x * lax.rsqrt(ms + RMS_EPS) * gain


def _cast_w_in_rows(src, dst_ref, dst_rows):
    for lo, hi in PROJ_SEGS:
        w = src(lo, hi)
        if (lo, hi) in HALVED_SEGS:
            w = w * 0.5
        dst_ref[dst_rows, lo:hi] = w.astype(jnp.bfloat16)


def _w_in_copy(w_in_hbm_ref, stage_ref, sem_ref, layer, i):
    return pltpu.make_async_copy(
        w_in_hbm_ref.at[layer, pl.ds(i * W_ROWS, W_ROWS), :],
        stage_ref.at[i % N_STAGE], sem_ref.at[i % N_STAGE])


def _load_w_in(w_in_hbm_ref, w_in_ref, stage_ref, sem_ref, layer):
    n_blocks = D_MODEL // W_ROWS
    for i in range(N_STAGE):
        _w_in_copy(w_in_hbm_ref, stage_ref, sem_ref, layer, i).start()

    def convert_block(i, carry):
        _w_in_copy(w_in_hbm_ref, stage_ref, sem_ref, layer, i).wait()
        rows = pl.ds(pl.multiple_of(i * W_ROWS, W_ROWS), W_ROWS)
        _cast_w_in_rows(lambda lo, hi: stage_ref[i % N_STAGE, :, lo:hi], w_in_ref, rows)

        @pl.when(i + N_STAGE < n_blocks)
        def _():
            _w_in_copy(w_in_hbm_ref, stage_ref, sem_ref, layer, i + N_STAGE).start()

        return carry

    lax.fori_loop(0, n_blocks, convert_block, 0)


def _load_out_weights(hbm_refs, vmem_refs, stage_ref, sem_ref, layer):
    blocks = [(src, dst, r) for src, dst in zip(hbm_refs, vmem_refs)
              for r in range(0, dst.shape[0], O_ROWS)]

    def copy(i):
        src, _, r = blocks[i]
        return pltpu.make_async_copy(src.at[layer, pl.ds(r, O_ROWS), :],
                                     stage_ref.at[pl.ds((i % 2) * O_ROWS, O_ROWS), :],
                                     sem_ref.at[i % 2])

    copy(0).start()
    copy(1).start()
    for i, (_, dst, r) in enumerate(blocks):
        copy(i).wait()
        w = stage_ref[(i % 2) * O_ROWS:(i % 2 + 1) * O_ROWS, :]
        if dst is vmem_refs[-1]:
            w = w * 0.5
        dst[r:r + O_ROWS, :] = w.astype(jnp.bfloat16)
        if i + 2 < len(blocks):
            copy(i + 2).start()


class _NextLayerWIn:
    def __init__(self, w_in_hbm_ref, w_next_hbm_ref, stage_ref, cast_ref, sem_ref, layer,
                 step, n_steps):
        self.refs = (w_in_hbm_ref, w_next_hbm_ref, stage_ref, cast_ref, sem_ref)
        self.layer, self.step, self.n_steps = layer, step, n_steps
        self.rows = D_MODEL // n_steps

    def _rows_of(self, block):
        return pl.ds(pl.multiple_of(block * self.rows, self.rows), self.rows)

    def _fetch(self, block):
        w_in_hbm_ref, _, stage_ref, _, sem_ref = self.refs
        return pltpu.make_async_copy(w_in_hbm_ref.at[self.layer + 1, self._rows_of(block), :],
                                     stage_ref.at[block % 2, pl.ds(0, self.rows), :],
                                     sem_ref.at[block % 2])

    def _write_back(self, block):
        _, w_next_hbm_ref, _, cast_ref, sem_ref = self.refs
        return pltpu.make_async_copy(cast_ref.at[block % 2],
                                     w_next_hbm_ref.at[self._rows_of(block), :],
                                     sem_ref.at[2 + block % 2])

    def start_first_fetch(self):
        self._fetch(0).start()

    def begin_step(self):
        step = self.step
        self._fetch(step).wait()

        @pl.when(step + 1 < self.n_steps)
        def _():
            self._fetch(step + 1).start()

        @pl.when(step >= 2)
        def _():
            self._write_back(step - 2).wait()

    def cast(self):
        _, _, stage_ref, cast_ref, _ = self.refs
        slot = self.step % 2
        _cast_w_in_rows(lambda lo, hi: stage_ref[slot, 0:self.rows, lo:hi], cast_ref.at[slot],
                        slice(None))

    def end_step(self):
        step = self.step
        self._write_back(step).start()

        @pl.when(step == self.n_steps - 1)
        def _():
            self._write_back(step - 1).wait()
            self._write_back(step).wait()


def _prepare_small_weights(w_s_ref, b_s_ref, w_pool_ref, w_sp_ref, b_sp_ref, w_pool_bd_ref):
    f32 = jnp.float32
    bf16 = jnp.bfloat16
    row = lax.broadcasted_iota(jnp.int32, (CHUNK, CHUNK), 0)
    col = lax.broadcasted_iota(jnp.int32, (CHUNK, CHUNK), 1)
    causal = row >= col
    first_group = col < A_HEAD
    b_t = b_s_ref[...].T
    for jb in range(A_GROUPS // 2):
        w_pair = [jnp.where(causal, w_s_ref[2 * jb + k], 0.0) for k in range(2)]
        w_sp_ref[jb] = jnp.concatenate(w_pair, axis=1).astype(bf16)
        b_pair = [jnp.broadcast_to(b_t[:, 2 * jb + k:2 * jb + k + 1], (CHUNK, LANES))
                  for k in range(2)]
        b_sp_ref[:, jb * LANES:(jb + 1) * LANES] = jnp.where(first_group, b_pair[0], b_pair[1])
    zero = jnp.zeros((C_GROUP, C_GROUP), f32)
    for i in range(len(POOL_WINDOWS) // 2):
        top = jnp.concatenate([w_pool_ref[2 * i], zero], axis=1)
        bottom = jnp.concatenate([zero, w_pool_ref[2 * i + 1]], axis=1)
        w_pool_bd_ref[i] = jnp.concatenate([top, bottom], axis=0).astype(bf16)


def _layer_kernel(*refs, names, layer, steps_per_seq, n_steps, apply_final_norm):
    f32 = jnp.float32
    bf16 = jnp.bfloat16
    r = dict(zip(names, refs, strict=True))
    x_ref, x_next_ref, out_ref = r["x"], r["x_next"], r["out"]
    w_in_ref, sem_ref = r["w_in"], r["sem"]
    w_pa_ref, w_pb_ref, w_pc_ref, w_o_ref = r["w_pa"], r["w_pb"], r["w_pc"], r["w_o"]
    w_sp_ref, b_sp_ref, w_pool_bd_ref = r["w_sp"], r["b_sp"], r["w_pool_bd"]
    h_ref, puv_ref, cx_ref, xc_ref = r["h"], r["puv"], r["cx"], r["xc"]
    norm_g_ref, ln_g_ref, ln_b_ref = r["norm_g"], r["ln_g"], r["ln_b"]
    conv_w_ref, conv_b_ref, pool_scale_ref = r["conv_w"], r["conv_b"], r["pool_scale"]
    final_g_ref = r["final_g"]
    j = pl.program_id(1)
    step = pl.program_id(0) * steps_per_seq + j
    next_w_in = None
    if "w_in_next" in r:
        next_w_in = _NextLayerWIn(r["w_in_f32_hbm"], r["w_in_next"], r["stage"], r["cast"],
                                  r["next_sem"], layer, step, n_steps)

    this_layer = slice(layer, layer + 1)
    norm_g, final_g = norm_g_ref[this_layer, :], final_g_ref[...]
    ln_g, ln_b = ln_g_ref[this_layer, :], ln_b_ref[this_layer, :]
    conv_b, pool_scale = conv_b_ref[this_layer, :], pool_scale_ref[this_layer, :]
    conv_w = [conv_w_ref[layer, k:k + 1, :] for k in range(CONV_WIDTH)]

    @pl.when(j == 0)
    def _():
        cx_ref[0:HALO, :] = jnp.zeros((HALO, B_WIDTH), f32)
        xc_ref[0:HALO, :] = jnp.zeros((HALO, C_WIDTH), f32)

    def proj(seg):
        return jnp.dot(h_ref[...], w_in_ref[:, seg[0]:seg[1]], preferred_element_type=f32)

    @pl.when(step == 0)
    def _():
        if "w_in_bf16_hbm" in r:
            ready_w_in = pltpu.make_async_copy(r["w_in_bf16_hbm"], w_in_ref, sem_ref.at[2])
            ready_w_in.start()
        else:
            _load_w_in(r["w_in_f32_hbm"], w_in_ref, r["stage"], sem_ref, layer)
        _load_out_weights((r["w_pa_hbm"], r["w_pb_hbm"], r["w_pc_hbm"], r["w_o_hbm"]),
                          (w_pa_ref, w_pb_ref, w_pc_ref, w_o_ref), puv_ref, sem_ref, layer)
        _prepare_small_weights(r["w_s"], r["b_s"], r["w_pool"], w_sp_ref, b_sp_ref,
                               w_pool_bd_ref)
        h_ref[...] = _rmsnorm_rows(x_ref[0], norm_g).astype(bf16)
        if "w_in_bf16_hbm" in r:
            ready_w_in.wait()
        puv_ref[...] = proj(SEG_UV)
        if next_w_in is not None:
            next_w_in.start_first_fetch()

    if next_w_in is not None:
        next_w_in.begin_step()
        next_w_in.cast()

    n_chunks = TS // CHUNK
    lane = lax.broadcasted_iota(jnp.int32, (CHUNK, LANES), 1)
    first_group = lane < A_HEAD

    def spatial_mix(vn):
        sg_blocks = []
        for jb in range(A_WIDTH // LANES):
            rhs = []
            for c in range(n_chunks):
                vb = vn[c * CHUNK:(c + 1) * CHUNK, jb * LANES:(jb + 1) * LANES]
                rhs.append(jnp.concatenate([jnp.where(first_group, vb, 0.0),
                                            jnp.where(first_group, 0.0, vb)], axis=0))
            rhs = jnp.concatenate(rhs, axis=1).astype(bf16)
            mixed = jnp.dot(w_sp_ref[jb], rhs, preferred_element_type=f32)
            sg_blocks.append(jnp.concatenate(
                [mixed[:, c * LANES:(c + 1) * LANES] for c in range(n_chunks)], axis=0))
        sg = jnp.concatenate(sg_blocks, axis=1)
        return sg + jnp.concatenate([b_sp_ref[...]] * n_chunks, axis=0)

    def layer_norm(v):
        mu = jnp.mean(v, axis=-1, keepdims=True)
        vc = v - mu
        var = jnp.mean(vc * vc, axis=-1, keepdims=True)
        return vc * lax.rsqrt(var + LN_EPS) * ln_g + ln_b

    def store_xc(e):
        xc_ref[HALO:HALO + TS, :] = proj(SEG_XC)

    def store_cx(e):
        cx_ref[HALO:HALO + TS, :] = e["cg"] * e["xb"]

    def short_conv(e):
        conv = conv_b + conv_w[CONV_WIDTH - 1] * cx_ref[HALO:HALO + TS, :]
        for k in range(CONV_WIDTH - 1):
            back = CONV_WIDTH - 1 - k
            conv = conv + conv_w[k] * cx_ref[HALO - back:HALO - back + TS, :]
        cx_ref[0:HALO, :] = cx_ref[TS:TS + HALO, :]
        return conv

    def pooled_pairs(e):
        t1 = (j * TS + 1 + lax.broadcasted_iota(jnp.int32, (TS, C_GROUP), 0)).astype(f32)
        inv_t1 = 1.0 / t1
        pooled = []
        for gi, w in enumerate(POOL_WINDOWS):
            cols = slice(gi * C_GROUP, (gi + 1) * C_GROUP)
            ext = xc_ref[:, cols]
            win = ext
            span = 1
            while span < w:
                win = win + pltpu.roll(win, span, axis=0)
                span *= 2
            pooled.append(win[HALO:, :] * jnp.maximum(inv_t1, 1.0 / w) - ext[HALO:, :])
        xc_ref[0:HALO, :] = xc_ref[TS:TS + HALO, :]
        return [jnp.concatenate(pooled[2 * half:2 * half + 2], axis=1).astype(bf16)
                for half in range(2)]

    def pool_dots(e):
        groups = [jnp.dot(lhs, w_pool_bd_ref[half], preferred_element_type=f32)
                  for half, lhs in enumerate(e["pooled"])]
        return jnp.concatenate(groups, axis=1) * pool_scale

    def next_tile_u(e):
        h_ref[...] = _rmsnorm_rows(x_next_ref[0], norm_g).astype(bf16)
        puv_ref[:, 0:A_WIDTH] = proj(SEG_U)

    def next_tile_v(e):
        puv_ref[:, A_WIDTH:] = proj(SEG_V)

    def write_out(e):
        y = x_ref[0] + e["wo"]
        if apply_final_norm:
            y = _rmsnorm_rows(y, final_g)
        out_ref[0] = y

    def bdot(lhs, w_ref):
        return jnp.dot(lhs, w_ref[...], preferred_element_type=f32)

    matmul_jobs = {
        "xc": ((), store_xc),
        "za": ((), lambda e: proj(SEG_ZA)),
        "cg": ((), lambda e: proj(SEG_CG)),
        "xb": ((), lambda e: proj(SEG_XB)),
        "bg": ((), lambda e: proj(SEG_BG)),
        "zb": ((), lambda e: proj(SEG_ZB)),
        "zc": ((), lambda e: proj(SEG_ZC)),
        "ga": ((), lambda e: proj(SEG_GA)),
        "gb": ((), lambda e: proj(SEG_GB)),
        "gc": ((), lambda e: proj(SEG_GC)),
        "sp": (("vn",), lambda e: spatial_mix(e["vn"])),
        "pa": (("ya",), lambda e: bdot(e["ya"], w_pa_ref)),
        "pb": (("yb",), lambda e: bdot(e["yb"], w_pb_ref)),
        "po": (("pooled",), pool_dots),
        "pc": (("yc",), lambda e: bdot(e["yc"], w_pc_ref)),
        "u_next": (PROJ_JOBS, next_tile_u),
        "wo": (("merged_c",), lambda e: bdot(e["merged_c"].astype(bf16), w_o_ref)),
        "v_next": (("u_next",), next_tile_v),
    }
    elementwise = (
        ("u", (), lambda e: _gelu_of_half(puv_ref[:, 0:A_WIDTH])),
        ("vn", (), lambda e: layer_norm(_gelu_of_half(puv_ref[:, A_WIDTH:]))),
        ("cx", ("cg", "xb"), store_cx),
        ("ya", ("u", "sp", "za"),
         lambda e: (e["u"] * e["sp"] * _silu_of_half(e["za"])).astype(bf16)),
        ("conv", ("cx",), short_conv),
        ("yb", ("bg", "conv", "zb"),
         lambda e: (e["bg"] * e["conv"] * _silu_of_half(e["zb"])).astype(bf16)),
        ("pooled", ("xc",), pooled_pairs),
        ("yc", ("po", "zc"), lambda e: (e["po"] * _silu_of_half(e["zc"])).astype(bf16)),
        ("merged_a", ("ga", "pa"), lambda e: _twice_sigmoid_of_half(e["ga"]) * e["pa"]),
        ("merged_b", ("merged_a", "gb", "pb"),
         lambda e: e["merged_a"] + _twice_sigmoid_of_half(e["gb"]) * e["pb"]),
        ("gate_c", ("gc",), lambda e: _twice_sigmoid_of_half(e["gc"])),
        ("merged_c", ("merged_b", "gate_c", "pc"),
         lambda e: e["merged_b"] + e["gate_c"] * e["pc"]),
        ("out", ("wo",), write_out),
    )
    env = {}

    def trace_ready_elementwise():
        progress = True
        while progress:
            progress = False
            for name, needs, fn in elementwise:
                if name not in env and all(n in env for n in needs):
                    env[name] = fn(env)
                    progress = True

    assert sorted(MXU_ORDER) == sorted(matmul_jobs)
    for name in MXU_ORDER:
        trace_ready_elementwise()
        needs, fn = matmul_jobs[name]
        assert all(n in env for n in needs), (name, needs)
        env[name] = fn(env)
    trace_ready_elementwise()
    assert len(env) == len(matmul_jobs) + len(elementwise)
    if next_w_in is not None:
        next_w_in.end_step()


def _whole_spec(array):
    zeros = (0,) * array.ndim
    return pl.BlockSpec(array.shape, lambda b, j: zeros, pipeline_mode=pl.Buffered(1))


def _layer_spec(array, layer):
    zeros = (0,) * (array.ndim - 1)
    return pl.BlockSpec((None,) + array.shape[1:], lambda b, j: (layer,) + zeros,
                        pipeline_mode=pl.Buffered(1))


VECTOR_NAMES = ("norm_g", "ln_g", "ln_b", "conv_w", "conv_b", "pool_scale", "final_g")
PER_LAYER_NAMES = ("w_s", "b_s", "w_pool")
OUT_WEIGHT_NAMES = ("w_pa", "w_pb", "w_pc", "w_o")


def _layer(x, w_in, w_in_bf16, vectors, per_layer, out_weights, layer, *, convert_next,
           apply_final_norm):
    bsz, seq, _ = x.shape
    assert seq % TS == 0 and TS % CHUNK == 0 and TS >= HALO
    assert w_in.dtype == jnp.float32 and w_in.shape[1:] == (D_MODEL, IN_TOTAL)
    assert D_MODEL % W_ROWS == 0 and TS == 2 * O_ROWS and N_STAGE >= 3
    steps_per_seq = seq // TS
    n_steps = bsz * steps_per_seq
    bf16 = jnp.bfloat16
    f32 = jnp.float32

    def next_tile(b, j):
        s = jnp.minimum(b * steps_per_seq + j + 1, n_steps - 1)
        return (s // steps_per_seq, s % steps_per_seq, 0)

    tile = (1, TS, D_MODEL)
    any_spec = pl.BlockSpec(memory_space=pl.ANY)
    inputs = [("x", x, pl.BlockSpec(tile, lambda b, j: (b, j, 0))),
              ("x_next", x, pl.BlockSpec(tile, next_tile)),
              ("w_in_f32_hbm", w_in, any_spec)]
    if w_in_bf16 is not None:
        inputs.append(("w_in_bf16_hbm", w_in_bf16, any_spec))
    inputs += [(n, a, _whole_spec(a)) for n, a in zip(VECTOR_NAMES, vectors, strict=True)]
    inputs += [(n, a, _layer_spec(a, layer))
               for n, a in zip(PER_LAYER_NAMES, per_layer, strict=True)]
    inputs += [(n + "_hbm", a, any_spec)
               for n, a in zip(OUT_WEIGHT_NAMES, out_weights, strict=True)]

    outputs = [("out", jax.ShapeDtypeStruct(x.shape, x.dtype),
                pl.BlockSpec(tile, lambda b, j: (b, j, 0)))]
    scratch = [
        ("w_in", pltpu.VMEM((D_MODEL, IN_TOTAL), bf16)),
        ("sem", pltpu.SemaphoreType.DMA((N_STAGE,))),
        ("w_pa", pltpu.VMEM((A_WIDTH, D_MODEL), bf16)),
        ("w_pb", pltpu.VMEM((B_WIDTH, D_MODEL), bf16)),
        ("w_pc", pltpu.VMEM((C_WIDTH, D_MODEL), bf16)),
        ("w_o", pltpu.VMEM((D_MODEL, D_MODEL), bf16)),
        ("w_sp", pltpu.VMEM((A_GROUPS // 2, CHUNK, 2 * CHUNK), bf16)),
        ("b_sp", pltpu.VMEM((CHUNK, A_WIDTH), f32)),
        ("w_pool_bd", pltpu.VMEM((len(POOL_WINDOWS) // 2, 2 * C_GROUP, 2 * C_GROUP), bf16)),
        ("h", pltpu.VMEM((TS, D_MODEL), bf16)),
        ("puv", pltpu.VMEM((TS, 2 * A_WIDTH), f32)),
        ("cx", pltpu.VMEM((TS + HALO, B_WIDTH), f32)),
        ("xc", pltpu.VMEM((TS + HALO, C_WIDTH), f32)),
    ]
    if w_in_bf16 is None or convert_next:
        scratch.append(("stage", pltpu.VMEM((N_STAGE, W_ROWS, IN_TOTAL), f32)))
    if convert_next:
        assert D_MODEL % n_steps == 0 and (D_MODEL // n_steps) % BF16_SUBLANES == 0
        assert D_MODEL // n_steps <= W_ROWS and n_steps >= 2
        outputs.append(("w_in_next", jax.ShapeDtypeStruct((D_MODEL, IN_TOTAL), bf16), any_spec))
        scratch += [("cast", pltpu.VMEM((2, D_MODEL // n_steps, IN_TOTAL), bf16)),
                    ("next_sem", pltpu.SemaphoreType.DMA((4,)))]

    names = tuple(n for n, *_ in inputs + outputs + scratch)
    results = pl.pallas_call(
        functools.partial(_layer_kernel, names=names, layer=layer, steps_per_seq=steps_per_seq,
                          n_steps=n_steps, apply_final_norm=apply_final_norm),
        out_shape=[s for _, s, _ in outputs],
        grid=(bsz, steps_per_seq),
        in_specs=[spec for _, _, spec in inputs],
        out_specs=[spec for _, _, spec in outputs],
        scratch_shapes=[s for _, s in scratch],
        compiler_params=pltpu.CompilerParams(
            dimension_semantics=("arbitrary", "arbitrary"),
            vmem_limit_bytes=VMEM_LIMIT_BYTES,
        ),
        name="hybrid_layer_final" if apply_final_norm else "hybrid_layer",
    )(*[a for _, a, _ in inputs])
    return results[0], (results[1] if convert_next else None)


def kernel(x, norm_g, w_in, ln_g, ln_b, w_s, b_s, conv_w, conv_b, w_pool, pool_scale,
           w_pa, w_pb, w_pc, w_o, final_g):
    vectors = (norm_g, ln_g, ln_b, conv_w, conv_b, pool_scale, final_g[None, :])
    per_layer = (w_s, b_s, w_pool)
    out_weights = (w_pa, w_pb, w_pc, w_o)
    w_in_bf16 = None
    for layer in range(DEPTH):
        last = layer == DEPTH - 1
        x, w_in_bf16 = _layer(x, w_in, w_in_bf16, vectors, per_layer, out_weights, layer,
                              convert_next=not last, apply_final_norm=last)
    return x
```

```python
import functools
import math

import jax
import jax.numpy as jnp
from jax import lax
from jax.experimental import pallas as pl
from jax.experimental.pallas import tpu as pltpu

D_MODEL = 1024
DEPTH = 2
CHUNK = 128
A_GROUPS = 8
A_WIDTH = 512
A_HEAD = A_WIDTH // A_GROUPS
B_WIDTH = 512
CONV_WIDTH = 3
C_WIDTH = 512
POOL_WINDOWS = (2, 4, 8, 16)
C_GROUP = C_WIDTH // len(POOL_WINDOWS)
IN_TOTAL = 3 * A_WIDTH + 4 * B_WIDTH + 2 * C_WIDTH + 3 * D_MODEL
RMS_EPS = 1e-6
LN_EPS = 1e-5

_WIDTHS = [A_WIDTH] * 3 + [B_WIDTH] * 4 + [C_WIDTH] * 2 + [D_MODEL] * 3
_OFFS = [sum(_WIDTHS[:i]) for i in range(len(_WIDTHS) + 1)]
PROJ_SEGS = tuple((_OFFS[i], _OFFS[i + 1]) for i in range(len(_WIDTHS)))
(SEG_U, SEG_V, SEG_ZA, SEG_XB, SEG_BG, SEG_CG, SEG_ZB, SEG_XC, SEG_ZC,
 SEG_GA, SEG_GB, SEG_GC) = PROJ_SEGS
SEG_UV = (SEG_U[0], SEG_V[1])
PROJ_JOBS = ("xc", "za", "cg", "xb", "bg", "zb", "zc", "ga", "gb", "gc")
MXU_ORDER = ("za", "cg", "xb", "sp", "bg", "xc", "zb", "pa", "gb", "zc", "po", "ga", "pb", "gc",
             "u_next", "pc", "wo", "v_next")

LANES = 128
BF16_SUBLANES = 16
HALO = 16
TS = 512
W_ROWS = 64
N_STAGE = 3
O_ROWS = 256
VMEM_LIMIT_BYTES = 56 * 1024 * 1024


HALVED_SEGS = (SEG_U, SEG_V, SEG_ZA, SEG_ZB, SEG_ZC, SEG_GA, SEG_GB, SEG_GC)
_GELU_C1 = 2.0 * math.sqrt(2.0 / math.pi)
_GELU_C3 = 8.0 * 0.044715 * math.sqrt(2.0 / math.pi)


def _gelu_of_half(xh):
    t = jnp.tanh(xh * (_GELU_C1 + _GELU_C3 * (xh * xh)))
    return xh + xh * t


def _silu_of_half(zh):
    return zh + zh * jnp.tanh(zh)


def _twice_sigmoid_of_half(gh):
    return jnp.tanh(gh) + 1.0


def _rmsnorm_rows(x, gain):
    ms = jnp.mean(x * x, axis=-1, keepdims=True)
    return x * lax.rsqrt(ms + RMS_EPS) * gain


def _cast_w_in_rows(src, dst_ref, dst_rows):
    for lo, hi in PROJ_SEGS:
        w = src(lo, hi)
        if (lo, hi) in HALVED_SEGS:
            w = w * 0.5
        dst_ref[dst_rows, lo:hi] = w.astype(jnp.bfloat16)


def _w_in_copy(w_in_hbm_ref, stage_ref, sem_ref, layer, i):
    return pltpu.make_async_copy(
        w_in_hbm_ref.at[layer, pl.ds(i * W_ROWS, W_ROWS), :],
        stage_ref.at[i % N_STAGE], sem_ref.at[i % N_STAGE])


def _load_w_in(w_in_hbm_ref, w_in_ref, stage_ref, sem_ref, layer):
    n_blocks = D_MODEL // W_ROWS
    for i in range(N_STAGE):
        _w_in_copy(w_in_hbm_ref, stage_ref, sem_ref, layer, i).start()

    def convert_block(i, carry):
        _w_in_copy(w_in_hbm_ref, stage_ref, sem_ref, layer, i).wait()
        rows = pl.ds(pl.multiple_of(i * W_ROWS, W_ROWS), W_ROWS)
        _cast_w_in_rows(lambda lo, hi: stage_ref[i % N_STAGE, :, lo:hi], w_in_ref, rows)

        @pl.when(i + N_STAGE < n_blocks)
        def _():
            _w_in_copy(w_in_hbm_ref, stage_ref, sem_ref, layer, i + N_STAGE).start()

        return carry

    lax.fori_loop(0, n_blocks, convert_block, 0)


def _load_out_weights(hbm_refs, vmem_refs, stage_ref, sem_ref, layer):
    blocks = [(src, dst, r) for src, dst in zip(hbm_refs, vmem_refs)
              for r in range(0, dst.shape[0], O_ROWS)]

    def copy(i):
        src, _, r = blocks[i]
        return pltpu.make_async_copy(src.at[layer, pl.ds(r, O_ROWS), :],
                                     stage_ref.at[pl.ds((i % 2) * O_ROWS, O_ROWS), :],
                                     sem_ref.at[i % 2])

    copy(0).start()
    copy(1).start()
    for i, (_, dst, r) in enumerate(blocks):
        copy(i).wait()
        w = stage_ref[(i % 2) * O_ROWS:(i % 2 + 1) * O_ROWS, :]
        if dst is vmem_refs[-1]:
            w = w * 0.5
        dst[r:r + O_ROWS, :] = w.astype(jnp.bfloat16)
        if i + 2 < len(blocks):
            copy(i + 2).start()


class _NextLayerWIn:
    def __init__(self, w_in_hbm_ref, w_next_hbm_ref, stage_ref, cast_ref, sem_ref, layer,
                 step, n_steps):
        self.refs = (w_in_hbm_ref, w_next_hbm_ref, stage_ref, cast_ref, sem_ref)
        self.layer, self.step, self.n_steps = layer, step, n_steps
        self.rows = D_MODEL // n_steps

    def _rows_of(self, block):
        return pl.ds(pl.multiple_of(block * self.rows, self.rows), self.rows)

    def _fetch(self, block):
        w_in_hbm_ref, _, stage_ref, _, sem_ref = self.refs
        return pltpu.make_async_copy(w_in_hbm_ref.at[self.layer + 1, self._rows_of(block), :],
                                     stage_ref.at[block % 2, pl.ds(0, self.rows), :],
                                     sem_ref.at[block % 2])

    def _write_back(self, block):
        _, w_next_hbm_ref, _, cast_ref, sem_ref = self.refs
        return pltpu.make_async_copy(cast_ref.at[block % 2],
                                     w_next_hbm_ref.at[self._rows_of(block), :],
                                     sem_ref.at[2 + block % 2])

    def start_first_fetch(self):
        self._fetch(0).start()

    def begin_step(self):
        step = self.step
        self._fetch(step).wait()

        @pl.when(step + 1 < self.n_steps)
        def _():
            self._fetch(step + 1).start()

        @pl.when(step >= 2)
        def _():
            self._write_back(step - 2).wait()

    def cast(self):
        _, _, stage_ref, cast_ref, _ = self.refs
        slot = self.step % 2
        _cast_w_in_rows(lambda lo, hi: stage_ref[slot, 0:self.rows, lo:hi], cast_ref.at[slot],
                        slice(None))

    def end_step(self):
        step = self.step
        self._write_back(step).start()

        @pl.when(step == self.n_steps - 1)
        def _():
            self._write_back(step - 1).wait()
            self._write_back(step).wait()


def _prepare_small_weights(w_s_ref, b_s_ref, w_pool_ref, w_sp_ref, b_sp_ref, w_pool_bd_ref):
    f32 = jnp.float32
    bf16 = jnp.bfloat16
    row = lax.broadcasted_iota(jnp.int32, (CHUNK, CHUNK), 0)
    col = lax.broadcasted_iota(jnp.int32, (CHUNK, CHUNK), 1)
    causal = row >= col
    first_group = col < A_HEAD
    b_t = b_s_ref[...].T
    for jb in range(A_GROUPS // 2):
        w_pair = [jnp.where(causal, w_s_ref[2 * jb + k], 0.0) for k in range(2)]
        w_sp_ref[jb] = jnp.concatenate(w_pair, axis=1).astype(bf16)
        b_pair = [jnp.broadcast_to(b_t[:, 2 * jb + k:2 * jb + k + 1], (CHUNK, LANES))
                  for k in range(2)]
        b_sp_ref[:, jb * LANES:(jb + 1) * LANES] = jnp.where(first_group, b_pair[0], b_pair[1])
    zero = jnp.zeros((C_GROUP, C_GROUP), f32)
    for i in range(len(POOL_WINDOWS) // 2):
        top = jnp.concatenate([w_pool_ref[2 * i], zero], axis=1)
        bottom = jnp.concatenate([zero, w_pool_ref[2 * i + 1]], axis=1)
        w_pool_bd_ref[i] = jnp.concatenate([top, bottom], axis=0).astype(bf16)


def _layer_kernel(*refs, names, layer, steps_per_seq, n_steps, apply_final_norm):
    f32 = jnp.float32
    bf16 = jnp.bfloat16
    r = dict(zip(names, refs, strict=True))
    x_ref, x_next_ref, out_ref = r["x"], r["x_next"], r["out"]
    w_in_ref, sem_ref = r["w_in"], r["sem"]
    w_pa_ref, w_pb_ref, w_pc_ref, w_o_ref = r["w_pa"], r["w_pb"], r["w_pc"], r["w_o"]
    w_sp_ref, b_sp_ref, w_pool_bd_ref = r["w_sp"], r["b_sp"], r["w_pool_bd"]
    h_ref, puv_ref, cx_ref, xc_ref = r["h"], r["puv"], r["cx"], r["xc"]
    norm_g_ref, ln_g_ref, ln_b_ref = r["norm_g"], r["ln_g"], r["ln_b"]
    conv_w_ref, conv_b_ref, pool_scale_ref = r["conv_w"], r["conv_b"], r["pool_scale"]
    final_g_ref = r["final_g"]
    j = pl.program_id(1)
    step = pl.program_id(0) * steps_per_seq + j
    next_w_in = None
    if "w_in_next" in r:
        next_w_in = _NextLayerWIn(r["w_in_f32_hbm"], r["w_in_next"], r["stage"], r["cast"],
                                  r["next_sem"], layer, step, n_steps)

    this_layer = slice(layer, layer + 1)
    norm_g, final_g = norm_g_ref[this_layer, :], final_g_ref[...]
    ln_g, ln_b = ln_g_ref[this_layer, :], ln_b_ref[this_layer, :]
    conv_b, pool_scale = conv_b_ref[this_layer, :], pool_scale_ref[this_layer, :]
    conv_w = [conv_w_ref[layer, k:k + 1, :] for k in range(CONV_WIDTH)]

    @pl.when(j == 0)
    def _():
        cx_ref[0:HALO, :] = jnp.zeros((HALO, B_WIDTH), f32)
        xc_ref[0:HALO, :] = jnp.zeros((HALO, C_WIDTH), f32)

    def proj(seg):
        return jnp.dot(h_ref[...], w_in_ref[:, seg[0]:seg[1]], preferred_element_type=f32)

    @pl.when(step == 0)
    def _():
        if "w_in_bf16_hbm" in r:
            ready_w_in = pltpu.make_async_copy(r["w_in_bf16_hbm"], w_in_ref, sem_ref.at[2])
            ready_w_in.start()
        else:
            _load_w_in(r["w_in_f32_hbm"], w_in_ref, r["stage"], sem_ref, layer)
        _load_out_weights((r["w_pa_hbm"], r["w_pb_hbm"], r["w_pc_hbm"], r["w_o_hbm"]),
                          (w_pa_ref, w_pb_ref, w_pc_ref, w_o_ref), puv_ref, sem_ref, layer)
        _prepare_small_weights(r["w_s"], r["b_s"], r["w_pool"], w_sp_ref, b_sp_ref,
                               w_pool_bd_ref)
        h_ref[...] = _rmsnorm_rows(x_ref[0], norm_g).astype(bf16)
        if "w_in_bf16_hbm" in r:
            ready_w_in.wait()
        puv_ref[...] = proj(SEG_UV)
        if next_w_in is not None:
            next_w_in.start_first_fetch()

    if next_w_in is not None:
        next_w_in.begin_step()
        next_w_in.cast()

    n_chunks = TS // CHUNK
    lane = lax.broadcasted_iota(jnp.int32, (CHUNK, LANES), 1)
    first_group = lane < A_HEAD

    def spatial_mix(vn):
        sg_blocks = []
        for jb in range(A_WIDTH // LANES):
            rhs = []
            for c in range(n_chunks):
                vb = vn[c * CHUNK:(c + 1) * CHUNK, jb * LANES:(jb + 1) * LANES]
                rhs.append(jnp.concatenate([jnp.where(first_group, vb, 0.0),
                                            jnp.where(first_group, 0.0, vb)], axis=0))
            rhs = jnp.concatenate(rhs, axis=1).astype(bf16)
            mixed = jnp.dot(w_sp_ref[jb], rhs, preferred_element_type=f32)
            sg_blocks.append(jnp.concatenate(
                [mixed[:, c * LANES:(c + 1) * LANES] for c in range(n_chunks)], axis=0))
        sg = jnp.concatenate(sg_blocks, axis=1)
        return sg + jnp.concatenate([b_sp_ref[...]] * n_chunks, axis=0)

    def layer_norm(v):
        mu = jnp.mean(v, axis=-1, keepdims=True)
        vc = v - mu
        var = jnp.mean(vc * vc, axis=-1, keepdims=True)
        return vc * lax.rsqrt(var + LN_EPS) * ln_g + ln_b

    def store_xc(e):
        xc_ref[HALO:HALO + TS, :] = proj(SEG_XC)

    def store_cx(e):
        cx_ref[HALO:HALO + TS, :] = e["cg"] * e["xb"]

    def short_conv(e):
        conv = conv_b + conv_w[CONV_WIDTH - 1] * cx_ref[HALO:HALO + TS, :]
        for k in range(CONV_WIDTH - 1):
            back = CONV_WIDTH - 1 - k
            conv = conv + conv_w[k] * cx_ref[HALO - back:HALO - back + TS, :]
        cx_ref[0:HALO, :] = cx_ref[TS:TS + HALO, :]
        return conv

    def pooled_pairs(e):
        t1 = (j * TS + 1 + lax.broadcasted_iota(jnp.int32, (TS, C_GROUP), 0)).astype(f32)
        inv_t1 = 1.0 / t1
        pooled = []
        for gi, w in enumerate(POOL_WINDOWS):
            cols = slice(gi * C_GROUP, (gi + 1) * C_GROUP)
            ext = xc_ref[:, cols]
            win = ext
            span = 1
            while span < w:
                win = win + pltpu.roll(win, span, axis=0)
                span *= 2
            pooled.append(win[HALO:, :] * jnp.maximum(inv_t1, 1.0 / w) - ext[HALO:, :])
        xc_ref[0:HALO, :] = xc_ref[TS:TS + HALO, :]
        return [jnp.concatenate(pooled[2 * half:2 * half + 2], axis=1).astype(bf16)
                for half in range(2)]

    def pool_dots(e):
        groups = [jnp.dot(lhs, w_pool_bd_ref[half], preferred_element_type=f32)
                  for half, lhs in enumerate(e["pooled"])]
        return jnp.concatenate(groups, axis=1) * pool_scale

    def next_tile_u(e):
        h_ref[...] = _rmsnorm_rows(x_next_ref[0], norm_g).astype(bf16)
        puv_ref[:, 0:A_WIDTH] = proj(SEG_U)

    def next_tile_v(e):
        puv_ref[:, A_WIDTH:] = proj(SEG_V)

    def write_out(e):
        y = x_ref[0] + e["wo"]
        if apply_final_norm:
            y = _rmsnorm_rows(y, final_g)
        out_ref[0] = y

    def bdot(lhs, w_ref):
        return jnp.dot(lhs, w_ref[...], preferred_element_type=f32)

    matmul_jobs = {
        "xc": ((), store_xc),
        "za": ((), lambda e: proj(SEG_ZA)),
        "cg": ((), lambda e: proj(SEG_CG)),
        "xb": ((), lambda e: proj(SEG_XB)),
        "bg": ((), lambda e: proj(SEG_BG)),
        "zb": ((), lambda e: proj(SEG_ZB)),
        "zc": ((), lambda e: proj(SEG_ZC)),
        "ga": ((), lambda e: proj(SEG_GA)),
        "gb": ((), lambda e: proj(SEG_GB)),
        "gc": ((), lambda e: proj(SEG_GC)),
        "sp": (("vn",), lambda e: spatial_mix(e["vn"])),
        "pa": (("ya",), lambda e: bdot(e["ya"], w_pa_ref)),
        "pb": (("yb",), lambda e: bdot(e["yb"], w_pb_ref)),
        "po": (("pooled",), pool_dots),
        "pc": (("yc",), lambda e: bdot(e["yc"], w_pc_ref)),
        "u_next": (PROJ_JOBS, next_tile_u),
        "wo": (("merged_c",), lambda e: bdot(e["merged_c"].astype(bf16), w_o_ref)),
        "v_next": (("u_next",), next_tile_v),
    }
    elementwise = (
        ("u", (), lambda e: _gelu_of_half(puv_ref[:, 0:A_WIDTH])),
        ("vn", (), lambda e: layer_norm(_gelu_of_half(puv_ref[:, A_WIDTH:]))),
        ("cx", ("cg", "xb"), store_cx),
        ("ya", ("u", "sp", "za"),
         lambda e: (e["u"] * e["sp"] * _silu_of_half(e["za"])).astype(bf16)),
        ("conv", ("cx",), short_conv),
        ("yb", ("bg", "conv", "zb"),
         lambda e: (e["bg"] * e["conv"] * _silu_of_half(e["zb"])).astype(bf16)),
        ("pooled", ("xc",), pooled_pairs),
        ("yc", ("po", "zc"), lambda e: (e["po"] * _silu_of_half(e["zc"])).astype(bf16)),
        ("merged_a", ("ga", "pa"), lambda e: _twice_sigmoid_of_half(e["ga"]) * e["pa"]),
        ("merged_b", ("merged_a", "gb", "pb"),
         lambda e: e["merged_a"] + _twice_sigmoid_of_half(e["gb"]) * e["pb"]),
        ("gate_c", ("gc",), lambda e: _twice_sigmoid_of_half(e["gc"])),
        ("merged_c", ("merged_b", "gate_c", "pc"),
         lambda e: e["merged_b"] + e["gate_c"] * e["pc"]),
        ("out", ("wo",), write_out),
    )
    env = {}

    def trace_ready_elementwise():
        progress = True
        while progress:
            progress = False
            for name, needs, fn in elementwise:
                if name not in env and all(n in env for n in needs):
                    env[name] = fn(env)
                    progress = True

    assert sorted(MXU_ORDER) == sorted(matmul_jobs)
    for name in MXU_ORDER:
        trace_ready_elementwise()
        needs, fn = matmul_jobs[name]
        assert all(n in env for n in needs), (name, needs)
        env[name] = fn(env)
    trace_ready_elementwise()
    assert len(env) == len(matmul_jobs) + len(elementwise)
    if next_w_in is not None:
        next_w_in.end_step()


def _whole_spec(array):
    zeros = (0,) * array.ndim
    return pl.BlockSpec(array.shape, lambda b, j: zeros, pipeline_mode=pl.Buffered(1))


def _layer_spec(array, layer):
    zeros = (0,) * (array.ndim - 1)
    return pl.BlockSpec((None,) + array.shape[1:], lambda b, j: (layer,) + zeros,
                        pipeline_mode=pl.Buffered(1))


VECTOR_NAMES = ("norm_g", "ln_g", "ln_b", "conv_w", "conv_b", "pool_scale", "final_g")
PER_LAYER_NAMES = ("w_s", "b_s", "w_pool")
OUT_WEIGHT_NAMES = ("w_pa", "w_pb", "w_pc", "w_o")


def _layer(x, w_in, w_in_bf16, vectors, per_layer, out_weights, layer, *, convert_next,
           apply_final_norm):
    bsz, seq, _ = x.shape
    assert seq % TS == 0 and TS % CHUNK == 0 and TS >= HALO
    assert w_in.dtype == jnp.float32 and w_in.shape[1:] == (D_MODEL, IN_TOTAL)
    assert D_MODEL % W_ROWS == 0 and TS == 2 * O_ROWS and N_STAGE >= 3
    steps_per_seq = seq // TS
    n_steps = bsz * steps_per_seq
    bf16 = jnp.bfloat16
    f32 = jnp.float32

    def next_tile(b, j):
        s = jnp.minimum(b * steps_per_seq + j + 1, n_steps - 1)
        return (s // steps_per_seq, s % steps_per_seq, 0)

    tile = (1, TS, D_MODEL)
    any_spec = pl.BlockSpec(memory_space=pl.ANY)
    inputs = [("x", x, pl.BlockSpec(tile, lambda b, j: (b, j, 0))),
              ("x_next", x, pl.BlockSpec(tile, next_tile)),
              ("w_in_f32_hbm", w_in, any_spec)]
    if w_in_bf16 is not None:
        inputs.append(("w_in_bf16_hbm", w_in_bf16, any_spec))
    inputs += [(n, a, _whole_spec(a)) for n, a in zip(VECTOR_NAMES, vectors, strict=True)]
    inputs += [(n, a, _layer_spec(a, layer))
               for n, a in zip(PER_LAYER_NAMES, per_layer, strict=True)]
    inputs += [(n + "_hbm", a, any_spec)
               for n, a in zip(OUT_WEIGHT_NAMES, out_weights, strict=True)]

    outputs = [("out", jax.ShapeDtypeStruct(x.shape, x.dtype),
                pl.BlockSpec(tile, lambda b, j: (b, j, 0)))]
    scratch = [
        ("w_in", pltpu.VMEM((D_MODEL, IN_TOTAL), bf16)),
        ("sem", pltpu.SemaphoreType.DMA((N_STAGE,))),
        ("w_pa", pltpu.VMEM((A_WIDTH, D_MODEL), bf16)),
        ("w_pb", pltpu.VMEM((B_WIDTH, D_MODEL), bf16)),
        ("w_pc", pltpu.VMEM((C_WIDTH, D_MODEL), bf16)),
        ("w_o", pltpu.VMEM((D_MODEL, D_MODEL), bf16)),
        ("w_sp", pltpu.VMEM((A_GROUPS // 2, CHUNK, 2 * CHUNK), bf16)),
        ("b_sp", pltpu.VMEM((CHUNK, A_WIDTH), f32)),
        ("w_pool_bd", pltpu.VMEM((len(POOL_WINDOWS) // 2, 2 * C_GROUP, 2 * C_GROUP), bf16)),
        ("h", pltpu.VMEM((TS, D_MODEL), bf16)),
        ("puv", pltpu.VMEM((TS, 2 * A_WIDTH), f32)),
        ("cx", pltpu.VMEM((TS + HALO, B_WIDTH), f32)),
        ("xc", pltpu.VMEM((TS + HALO, C_WIDTH), f32)),
    ]
    if w_in_bf16 is None or convert_next:
        scratch.append(("stage", pltpu.VMEM((N_STAGE, W_ROWS, IN_TOTAL), f32)))
    if convert_next:
        assert D_MODEL % n_steps == 0 and (D_MODEL // n_steps) % BF16_SUBLANES == 0
        assert D_MODEL // n_steps <= W_ROWS and n_steps >= 2
        outputs.append(("w_in_next", jax.ShapeDtypeStruct((D_MODEL, IN_TOTAL), bf16), any_spec))
        scratch += [("cast", pltpu.VMEM((2, D_MODEL // n_steps, IN_TOTAL), bf16)),
                    ("next_sem", pltpu.SemaphoreType.DMA((4,)))]

    names = tuple(n for n, *_ in inputs + outputs + scratch)
    results = pl.pallas_call(
        functools.partial(_layer_kernel, names=names, layer=layer, steps_per_seq=steps_per_seq,
                          n_steps=n_steps, apply_final_norm=apply_final_norm),
        out_shape=[s for _, s, _ in outputs],
        grid=(bsz, steps_per_seq),
        in_specs=[spec for _, _, spec in inputs],
        out_specs=[spec for _, _, spec in outputs],
        scratch_shapes=[s for _, s in scratch],
        compiler_params=pltpu.CompilerParams(
            dimension_semantics=("arbitrary", "arbitrary"),
            vmem_limit_bytes=VMEM_LIMIT_BYTES,
        ),
        name="hybrid_layer_final" if apply_final_norm else "hybrid_layer",
    )(*[a for _, a, _ in inputs])
    return results[0], (results[1] if convert_next else None)


def kernel(x, norm_g, w_in, ln_g, ln_b, w_s, b_s, conv_w, conv_b, w_pool, pool_scale,
           w_pa, w_pb, w_pc, w_o, final_g):
    vectors = (norm_g, ln_g, ln_b, conv_w, conv_b, pool_scale, final_g[None, :])
    per_layer = (w_s, b_s, w_pool)
    out_weights = (w_pa, w_pb, w_pc, w_o)
    w_in_bf16 = None
    for layer in range(DEPTH):
        last = layer == DEPTH - 1
        x, w_in_bf16 = _layer(x, w_in, w_in_bf16, vectors, per_layer, out_weights, layer,
                              convert_next=not last, apply_final_norm=last)
    return x
```

```python
import functools
import math

import jax
import jax.numpy as jnp
from jax import lax
from jax.experimental import pallas as pl
from jax.experimental.pallas import tpu as pltpu

D_MODEL = 1024
DEPTH = 2
CHUNK = 128
A_GROUPS = 8
A_WIDTH = 512
A_HEAD = A_WIDTH // A_GROUPS
B_WIDTH = 512
CONV_WIDTH = 3
C_WIDTH = 512
POOL_WINDOWS = (2, 4, 8, 16)
C_GROUP = C_WIDTH // len(POOL_WINDOWS)
IN_TOTAL = 3 * A_WIDTH + 4 * B_WIDTH + 2 * C_WIDTH + 3 * D_MODEL
RMS_EPS = 1e-6
LN_EPS = 1e-5

_WIDTHS = [A_WIDTH] * 3 + [B_WIDTH] * 4 + [C_WIDTH] * 2 + [D_MODEL] * 3
_OFFS = [sum(_WIDTHS[:i]) for i in range(len(_WIDTHS) + 1)]
PROJ_SEGS = tuple((_OFFS[i], _OFFS[i + 1]) for i in range(len(_WIDTHS)))
(SEG_U, SEG_V, SEG_ZA, SEG_XB, SEG_BG, SEG_CG, SEG_ZB, SEG_XC, SEG_ZC,
 SEG_GA, SEG_GB, SEG_GC) = PROJ_SEGS
SEG_UV = (SEG_U[0], SEG_V[1])
PROJ_JOBS = ("xc", "za", "cg", "xb", "bg", "zb", "zc", "ga", "gb", "gc")
MXU_ORDER = ("gb", "ga", "za", "cg", "xb", "sp", "bg", "xc", "zb", "pa", "zc", "po", "pb", "gc",
             "u_next", "pc", "wo", "v_next")

LANES = 128
BF16_SUBLANES = 16
HALO = 16
TS = 512
W_ROWS = 64
N_STAGE = 3
O_ROWS = 256
VMEM_LIMIT_BYTES = 56 * 1024 * 1024


HALVED_SEGS = (SEG_U, SEG_V, SEG_ZA, SEG_ZB, SEG_ZC, SEG_GA, SEG_GB, SEG_GC)
_GELU_C1 = 2.0 * math.sqrt(2.0 / math.pi)
_GELU_C3 = 8.0 * 0.044715 * math.sqrt(2.0 / math.pi)


def _gelu_of_half(xh):
    t = jnp.tanh(xh * (_GELU_C1 + _GELU_C3 * (xh * xh)))
    return xh + xh * t


def _silu_of_half(zh):
    return zh + zh * jnp.tanh(zh)


def _twice_sigmoid_of_half(gh):
    return jnp.tanh(gh) + 1.0


def _rmsnorm_rows(x, gain):
    ms = jnp.mean(x * x, axis=-1, keepdims=True)
    return x * lax.rsqrt(ms + RMS_EPS) * gain


def _cast_w_in_rows(src, dst_ref, dst_rows):
    for lo, hi in PROJ_SEGS:
        w = src(lo, hi)
        if (lo, hi) in HALVED_SEGS:
            w = w * 0.5
        dst_ref[dst_rows, lo:hi] = w.astype(jnp.bfloat16)


def _w_in_copy(w_in_hbm_ref, stage_ref, sem_ref, layer, i):
    return pltpu.make_async_copy(
        w_in_hbm_ref.at[layer, pl.ds(i * W_ROWS, W_ROWS), :],
        stage_ref.at[i % N_STAGE], sem_ref.at[i % N_STAGE])


def _load_w_in(w_in_hbm_ref, w_in_ref, stage_ref, sem_ref, layer):
    n_blocks = D_MODEL // W_ROWS
    for i in range(N_STAGE):
        _w_in_copy(w_in_hbm_ref, stage_ref, sem_ref, layer, i).start()

    def convert_block(i, carry):
        _w_in_copy(w_in_hbm_ref, stage_ref, sem_ref, layer, i).wait()
        rows = pl.ds(pl.multiple_of(i * W_ROWS, W_ROWS), W_ROWS)
        _cast_w_in_rows(lambda lo, hi: stage_ref[i % N_STAGE, :, lo:hi], w_in_ref, rows)

        @pl.when(i + N_STAGE < n_blocks)
        def _():
            _w_in_copy(w_in_hbm_ref, stage_ref, sem_ref, layer, i + N_STAGE).start()

        return carry

    lax.fori_loop(0, n_blocks, convert_block, 0)


def _load_out_weights(hbm_refs, vmem_refs, stage_ref, sem_ref, layer):
    blocks = [(src, dst, r) for src, dst in zip(hbm_refs, vmem_refs)
              for r in range(0, dst.shape[0], O_ROWS)]

    def copy(i):
        src, _, r = blocks[i]
        return pltpu.make_async_copy(src.at[layer, pl.ds(r, O_ROWS), :],
                                     stage_ref.at[pl.ds((i % 2) * O_ROWS, O_ROWS), :],
                                     sem_ref.at[i % 2])

    copy(0).start()
    copy(1).start()
    for i, (_, dst, r) in enumerate(blocks):
        copy(i).wait()
        w = stage_ref[(i % 2) * O_ROWS:(i % 2 + 1) * O_ROWS, :]
        if dst is vmem_refs[-1]:
            w = w * 0.5
        dst[r:r + O_ROWS, :] = w.astype(jnp.bfloat16)
        if i + 2 < len(blocks):
            copy(i + 2).start()


class _NextLayerWIn:
    def __init__(self, w_in_hbm_ref, w_next_hbm_ref, stage_ref, cast_ref, sem_ref, layer,
                 step, n_steps):
        self.refs = (w_in_hbm_ref, w_next_hbm_ref, stage_ref, cast_ref, sem_ref)
        self.layer, self.step, self.n_steps = layer, step, n_steps
        self.rows = D_MODEL // n_steps

    def _rows_of(self, block):
        return pl.ds(pl.multiple_of(block * self.rows, self.rows), self.rows)

    def _fetch(self, block):
        w_in_hbm_ref, _, stage_ref, _, sem_ref = self.refs
        return pltpu.make_async_copy(w_in_hbm_ref.at[self.layer + 1, self._rows_of(block), :],
                                     stage_ref.at[block % 2, pl.ds(0, self.rows), :],
                                     sem_ref.at[block % 2])

    def _write_back(self, block):
        _, w_next_hbm_ref, _, cast_ref, sem_ref = self.refs
        return pltpu.make_async_copy(cast_ref.at[block % 2],
                                     w_next_hbm_ref.at[self._rows_of(block), :],
                                     sem_ref.at[2 + block % 2])

    def start_first_fetch(self):
        self._fetch(0).start()

    def begin_step(self):
        step = self.step
        self._fetch(step).wait()

        @pl.when(step + 1 < self.n_steps)
        def _():
            self._fetch(step + 1).start()

        @pl.when(step >= 2)
        def _():
            self._write_back(step - 2).wait()

    def cast(self):
        _, _, stage_ref, cast_ref, _ = self.refs
        slot = self.step % 2
        _cast_w_in_rows(lambda lo, hi: stage_ref[slot, 0:self.rows, lo:hi], cast_ref.at[slot],
                        slice(None))

    def end_step(self):
        step = self.step
        self._write_back(step).start()

        @pl.when(step == self.n_steps - 1)
        def _():
            self._write_back(step - 1).wait()
            self._write_back(step).wait()


def _prepare_small_weights(w_s_ref, b_s_ref, w_pool_ref, w_sp_ref, b_sp_ref, w_pool_bd_ref):
    f32 = jnp.float32
    bf16 = jnp.bfloat16
    row = lax.broadcasted_iota(jnp.int32, (CHUNK, CHUNK), 0)
    col = lax.broadcasted_iota(jnp.int32, (CHUNK, CHUNK), 1)
    causal = row >= col
    first_group = col < A_HEAD
    b_t = b_s_ref[...].T
    for jb in range(A_GROUPS // 2):
        w_pair = [jnp.where(causal, w_s_ref[2 * jb + k], 0.0) for k in range(2)]
        w_sp_ref[jb] = jnp.concatenate(w_pair, axis=1).astype(bf16)
        b_pair = [jnp.broadcast_to(b_t[:, 2 * jb + k:2 * jb + k + 1], (CHUNK, LANES))
                  for k in range(2)]
        b_sp_ref[:, jb * LANES:(jb + 1) * LANES] = jnp.where(first_group, b_pair[0], b_pair[1])
    zero = jnp.zeros((C_GROUP, C_GROUP), f32)
    for i in range(len(POOL_WINDOWS) // 2):
        top = jnp.concatenate([w_pool_ref[2 * i], zero], axis=1)
        bottom = jnp.concatenate([zero, w_pool_ref[2 * i + 1]], axis=1)
        w_pool_bd_ref[i] = jnp.concatenate([top, bottom], axis=0).astype(bf16)


def _layer_kernel(*refs, names, layer, steps_per_seq, n_steps, apply_final_norm):
    f32 = jnp.float32
    bf16 = jnp.bfloat16
    r = dict(zip(names, refs, strict=True))
    x_ref, x_next_ref, out_ref = r["x"], r["x_next"], r["out"]
    w_in_ref, sem_ref = r["w_in"], r["sem"]
    w_pa_ref, w_pb_ref, w_pc_ref, w_o_ref = r["w_pa"], r["w_pb"], r["w_pc"], r["w_o"]
    w_sp_ref, b_sp_ref, w_pool_bd_ref = r["w_sp"], r["b_sp"], r["w_pool_bd"]
    h_ref, puv_ref, cx_ref, xc_ref = r["h"], r["puv"], r["cx"], r["xc"]
    norm_g_ref, ln_g_ref, ln_b_ref = r["norm_g"], r["ln_g"], r["ln_b"]
    conv_w_ref, conv_b_ref, pool_scale_ref = r["conv_w"], r["conv_b"], r["pool_scale"]
    final_g_ref = r["final_g"]
    j = pl.program_id(1)
    step = pl.program_id(0) * steps_per_seq + j
    next_w_in = None
    if "w_in_next" in r:
        next_w_in = _NextLayerWIn(r["w_in_f32_hbm"], r["w_in_next"], r["stage"], r["cast"],
                                  r["next_sem"], layer, step, n_steps)

    this_layer = slice(layer, layer + 1)
    norm_g, final_g = norm_g_ref[this_layer, :], final_g_ref[...]
    ln_g, ln_b = ln_g_ref[this_layer, :], ln_b_ref[this_layer, :]
    conv_b, pool_scale = conv_b_ref[this_layer, :], pool_scale_ref[this_layer, :]
    conv_w = [conv_w_ref[layer, k:k + 1, :] for k in range(CONV_WIDTH)]

    @pl.when(j == 0)
    def _():
        cx_ref[0:HALO, :] = jnp.zeros((HALO, B_WIDTH), f32)
        xc_ref[0:HALO, :] = jnp.zeros((HALO, C_WIDTH), f32)

    def proj(seg):
        return jnp.dot(h_ref[...], w_in_ref[:, seg[0]:seg[1]], preferred_element_type=f32)

    @pl.when(step == 0)
    def _():
        if "w_in_bf16_hbm" in r:
            ready_w_in = pltpu.make_async_copy(r["w_in_bf16_hbm"], w_in_ref, sem_ref.at[2])
            ready_w_in.start()
        else:
            _load_w_in(r["w_in_f32_hbm"], w_in_ref, r["stage"], sem_ref, layer)
        _load_out_weights((r["w_pa_hbm"], r["w_pb_hbm"], r["w_pc_hbm"], r["w_o_hbm"]),
                          (w_pa_ref, w_pb_ref, w_pc_ref, w_o_ref), puv_ref, sem_ref, layer)
        _prepare_small_weights(r["w_s"], r["b_s"], r["w_pool"], w_sp_ref, b_sp_ref,
                               w_pool_bd_ref)
        h_ref[...] = _rmsnorm_rows(x_ref[0], norm_g).astype(bf16)
        if "w_in_bf16_hbm" in r:
            ready_w_in.wait()
        puv_ref[...] = proj(SEG_UV)
        if next_w_in is not None:
            next_w_in.start_first_fetch()

    if next_w_in is not None:
        next_w_in.begin_step()
        next_w_in.cast()

    n_chunks = TS // CHUNK
    lane = lax.broadcasted_iota(jnp.int32, (CHUNK, LANES), 1)
    first_group = lane < A_HEAD

    def spatial_mix(vn):
        sg_blocks = []
        for jb in range(A_WIDTH // LANES):
            rhs = []
            for c in range(n_chunks):
                vb = vn[c * CHUNK:(c + 1) * CHUNK, jb * LANES:(jb + 1) * LANES]
                rhs.append(jnp.concatenate([jnp.where(first_group, vb, 0.0),
                                            jnp.where(first_group, 0.0, vb)], axis=0))
            rhs = jnp.concatenate(rhs, axis=1).astype(bf16)
            mixed = jnp.dot(w_sp_ref[jb], rhs, preferred_element_type=f32)
            sg_blocks.append(jnp.concatenate(
                [mixed[:, c * LANES:(c + 1) * LANES] for c in range(n_chunks)], axis=0))
        sg = jnp.concatenate(sg_blocks, axis=1)
        return sg + jnp.concatenate([b_sp_ref[...]] * n_chunks, axis=0)

    def layer_norm(v):
        mu = jnp.mean(v, axis=-1, keepdims=True)
        vc = v - mu
        var = jnp.mean(vc * vc, axis=-1, keepdims=True)
        return vc * lax.rsqrt(var + LN_EPS) * ln_g + ln_b

    def store_xc(e):
        xc_ref[HALO:HALO + TS, :] = proj(SEG_XC)

    def store_cx(e):
        cx_ref[HALO:HALO + TS, :] = e["cg"] * e["xb"]

    def short_conv(e):
        conv = conv_b + conv_w[CONV_WIDTH - 1] * cx_ref[HALO:HALO + TS, :]
        for k in range(CONV_WIDTH - 1):
            back = CONV_WIDTH - 1 - k
            conv = conv + conv_w[k] * cx_ref[HALO - back:HALO - back + TS, :]
        cx_ref[0:HALO, :] = cx_ref[TS:TS + HALO, :]
        return conv

    def pooled_pairs(e):
        t1 = (j * TS + 1 + lax.broadcasted_iota(jnp.int32, (TS, C_GROUP), 0)).astype(f32)
        inv_t1 = 1.0 / t1
        pooled = []
        for gi, w in enumerate(POOL_WINDOWS):
            cols = slice(gi * C_GROUP, (gi + 1) * C_GROUP)
            ext = xc_ref[:, cols]
            win = ext
            span = 1
            while span < w:
                win = win + pltpu.roll(win, span, axis=0)
                span *= 2
            pooled.append(win[HALO:, :] * jnp.maximum(inv_t1, 1.0 / w) - ext[HALO:, :])
        xc_ref[0:HALO, :] = xc_ref[TS:TS + HALO, :]
        return [jnp.concatenate(pooled[2 * half:2 * half + 2], axis=1).astype(bf16)
                for half in range(2)]

    def pool_dots(e):
        groups = [jnp.dot(lhs, w_pool_bd_ref[half], preferred_element_type=f32)
                  for half, lhs in enumerate(e["pooled"])]
        return jnp.concatenate(groups, axis=1) * pool_scale

    def next_tile_u(e):
        h_ref[...] = _rmsnorm_rows(x_next_ref[0], norm_g).astype(bf16)
        puv_ref[:, 0:A_WIDTH] = proj(SEG_U)

    def next_tile_v(e):
        puv_ref[:, A_WIDTH:] = proj(SEG_V)

    def write_out(e):
        y = x_ref[0] + e["wo"]
        if apply_final_norm:
            y = _rmsnorm_rows(y, final_g)
        out_ref[0] = y

    def bdot(lhs, w_ref):
        return jnp.dot(lhs, w_ref[...], preferred_element_type=f32)

    matmul_jobs = {
        "xc": ((), store_xc),
        "za": ((), lambda e: proj(SEG_ZA)),
        "cg": ((), lambda e: proj(SEG_CG)),
        "xb": ((), lambda e: proj(SEG_XB)),
        "bg": ((), lambda e: proj(SEG_BG)),
        "zb": ((), lambda e: proj(SEG_ZB)),
        "zc": ((), lambda e: proj(SEG_ZC)),
        "ga": ((), lambda e: proj(SEG_GA)),
        "gb": ((), lambda e: proj(SEG_GB)),
        "gc": ((), lambda e: proj(SEG_GC)),
        "sp": (("vn",), lambda e: spatial_mix(e["vn"])),
        "pa": (("ya",), lambda e: bdot(e["ya"], w_pa_ref)),
        "pb": (("yb",), lambda e: bdot(e["yb"], w_pb_ref)),
        "po": (("pooled",), pool_dots),
        "pc": (("yc",), lambda e: bdot(e["yc"], w_pc_ref)),
        "u_next": (PROJ_JOBS, next_tile_u),
        "wo": (("merged_c",), lambda e: bdot(e["merged_c"].astype(bf16), w_o_ref)),
        "v_next": (("u_next",), next_tile_v),
    }
    elementwise = (
        ("u", (), lambda e: _gelu_of_half(puv_ref[:, 0:A_WIDTH])),
        ("vn", (), lambda e: layer_norm(_gelu_of_half(puv_ref[:, A_WIDTH:]))),
        ("cx", ("cg", "xb"), store_cx),
        ("ya", ("u", "sp", "za"),
         lambda e: (e["u"] * e["sp"] * _silu_of_half(e["za"])).astype(bf16)),
        ("conv", ("cx",), short_conv),
        ("yb", ("bg", "conv", "zb"),
         lambda e: (e["bg"] * e["conv"] * _silu_of_half(e["zb"])).astype(bf16)),
        ("pooled", ("xc",), pooled_pairs),
        ("yc", ("po", "zc"), lambda e: (e["po"] * _silu_of_half(e["zc"])).astype(bf16)),
        ("merged_a", ("ga", "pa"), lambda e: _twice_sigmoid_of_half(e["ga"]) * e["pa"]),
        ("merged_b", ("merged_a", "gb", "pb"),
         lambda e: e["merged_a"] + _twice_sigmoid_of_half(e["gb"]) * e["pb"]),
        ("gate_c", ("gc",), lambda e: _twice_sigmoid_of_half(e["gc"])),
        ("merged_c", ("merged_b", "gate_c", "pc"),
         lambda e: e["merged_b"] + e["gate_c"] * e["pc"]),
        ("out", ("wo",), write_out),
    )
    env = {}

    def trace_ready_elementwise():
        progress = True
        while progress:
            progress = False
            for name, needs, fn in elementwise:
                if name not in env and all(n in env for n in needs):
                    env[name] = fn(env)
                    progress = True

    assert sorted(MXU_ORDER) == sorted(matmul_jobs)
    for name in MXU_ORDER:
        trace_ready_elementwise()
        needs, fn = matmul_jobs[name]
        assert all(n in env for n in needs), (name, needs)
        env[name] = fn(env)
    trace_ready_elementwise()
    assert len(env) == len(matmul_jobs) + len(elementwise)
    if next_w_in is not None:
        next_w_in.end_step()


def _whole_spec(array):
    zeros = (0,) * array.ndim
    return pl.BlockSpec(array.shape, lambda b, j: zeros, pipeline_mode=pl.Buffered(1))


def _layer_spec(array, layer):
    zeros = (0,) * (array.ndim - 1)
    return pl.BlockSpec((None,) + array.shape[1:], lambda b, j: (layer,) + zeros,
                        pipeline_mode=pl.Buffered(1))


VECTOR_NAMES = ("norm_g", "ln_g", "ln_b", "conv_w", "conv_b", "pool_scale", "final_g")
PER_LAYER_NAMES = ("w_s", "b_s", "w_pool")
OUT_WEIGHT_NAMES = ("w_pa", "w_pb", "w_pc", "w_o")


def _layer(x, w_in, w_in_bf16, vectors, per_layer, out_weights, layer, *, convert_next,
           apply_final_norm):
    bsz, seq, _ = x.shape
    assert seq % TS == 0 and TS % CHUNK == 0 and TS >= HALO
    assert w_in.dtype == jnp.float32 and w_in.shape[1:] == (D_MODEL, IN_TOTAL)
    assert D_MODEL % W_ROWS == 0 and TS == 2 * O_ROWS and N_STAGE >= 3
    steps_per_seq = seq // TS
    n_steps = bsz * steps_per_seq
    bf16 = jnp.bfloat16
    f32 = jnp.float32

    def next_tile(b, j):
        s = jnp.minimum(b * steps_per_seq + j + 1, n_steps - 1)
        return (s // steps_per_seq, s % steps_per_seq, 0)

    tile = (1, TS, D_MODEL)
    any_spec = pl.BlockSpec(memory_space=pl.ANY)
    inputs = [("x", x, pl.BlockSpec(tile, lambda b, j: (b, j, 0))),
              ("x_next", x, pl.BlockSpec(tile, next_tile)),
              ("w_in_f32_hbm", w_in, any_spec)]
    if w_in_bf16 is not None:
        inputs.append(("w_in_bf16_hbm", w_in_bf16, any_spec))
    inputs += [(n, a, _whole_spec(a)) for n, a in zip(VECTOR_NAMES, vectors, strict=True)]
    inputs += [(n, a, _layer_spec(a, layer))
               for n, a in zip(PER_LAYER_NAMES, per_layer, strict=True)]
    inputs += [(n + "_hbm", a, any_spec)
               for n, a in zip(OUT_WEIGHT_NAMES, out_weights, strict=True)]

    outputs = [("out", jax.ShapeDtypeStruct(x.shape, x.dtype),
                pl.BlockSpec(tile, lambda b, j: (b, j, 0)))]
    scratch = [
        ("w_in", pltpu.VMEM((D_MODEL, IN_TOTAL), bf16)),
        ("sem", pltpu.SemaphoreType.DMA((N_STAGE,))),
        ("w_pa", pltpu.VMEM((A_WIDTH, D_MODEL), bf16)),
        ("w_pb", pltpu.VMEM((B_WIDTH, D_MODEL), bf16)),
        ("w_pc", pltpu.VMEM((C_WIDTH, D_MODEL), bf16)),
        ("w_o", pltpu.VMEM((D_MODEL, D_MODEL), bf16)),
        ("w_sp", pltpu.VMEM((A_GROUPS // 2, CHUNK, 2 * CHUNK), bf16)),
        ("b_sp", pltpu.VMEM((CHUNK, A_WIDTH), f32)),
        ("w_pool_bd", pltpu.VMEM((len(POOL_WINDOWS) // 2, 2 * C_GROUP, 2 * C_GROUP), bf16)),
        ("h", pltpu.VMEM((TS, D_MODEL), bf16)),
        ("puv", pltpu.VMEM((TS, 2 * A_WIDTH), f32)),
        ("cx", pltpu.VMEM((TS + HALO, B_WIDTH), f32)),
        ("xc", pltpu.VMEM((TS + HALO, C_WIDTH), f32)),
    ]
    if w_in_bf16 is None or convert_next:
        scratch.append(("stage", pltpu.VMEM((N_STAGE, W_ROWS, IN_TOTAL), f32)))
    if convert_next:
        assert D_MODEL % n_steps == 0 and (D_MODEL // n_steps) % BF16_SUBLANES == 0
        assert D_MODEL // n_steps <= W_ROWS and n_steps >= 2
        outputs.append(("w_in_next", jax.ShapeDtypeStruct((D_MODEL, IN_TOTAL), bf16), any_spec))
        scratch += [("cast", pltpu.VMEM((2, D_MODEL // n_steps, IN_TOTAL), bf16)),
                    ("next_sem", pltpu.SemaphoreType.DMA((4,)))]

    names = tuple(n for n, *_ in inputs + outputs + scratch)
    results = pl.pallas_call(
        functools.partial(_layer_kernel, names=names, layer=layer, steps_per_seq=steps_per_seq,
                          n_steps=n_steps, apply_final_norm=apply_final_norm),
        out_shape=[s for _, s, _ in outputs],
        grid=(bsz, steps_per_seq),
        in_specs=[spec for _, _, spec in inputs],
        out_specs=[spec for _, _, spec in outputs],
        scratch_shapes=[s for _, s in scratch],
        compiler_params=pltpu.CompilerParams(
            dimension_semantics=("arbitrary", "arbitrary"),
            vmem_limit_bytes=VMEM_LIMIT_BYTES,
        ),
        name="hybrid_layer_final" if apply_final_norm else "hybrid_layer",
    )(*[a for _, a, _ in inputs])
    return results[0], (results[1] if convert_next else None)


def kernel(x, norm_g, w_in, ln_g, ln_b, w_s, b_s, conv_w, conv_b, w_pool, pool_scale,
           w_pa, w_pb, w_pc, w_o, final_g):
    vectors = (norm_g, ln_g, ln_b, conv_w, conv_b, pool_scale, final_g[None, :])
    per_layer = (w_s, b_s, w_pool)
    out_weights = (w_pa, w_pb, w_pc, w_o)
    w_in_bf16 = None
    for layer in range(DEPTH):
        last = layer == DEPTH - 1
        x, w_in_bf16 = _layer(x, w_in, w_in_bf16, vectors, per_layer, out_weights, layer,
                              convert_next=not last, apply_final_norm=last)
    return x
```

```python
import functools
import math

import jax
import jax.numpy as jnp
from jax import lax
from jax.experimental import pallas as pl
from jax.experimental.pallas import tpu as pltpu

D_MODEL = 1024
DEPTH = 2
CHUNK = 128
A_GROUPS = 8
A_WIDTH = 512
A_HEAD = A_WIDTH // A_GROUPS
B_WIDTH = 512
CONV_WIDTH = 3
C_WIDTH = 512
POOL_WINDOWS = (2, 4, 8, 16)
C_GROUP = C_WIDTH // len(POOL_WINDOWS)
IN_TOTAL = 3 * A_WIDTH + 4 * B_WIDTH + 2 * C_WIDTH + 3 * D_MODEL
RMS_EPS = 1e-6
LN_EPS = 1e-5

_WIDTHS = [A_WIDTH] * 3 + [B_WIDTH] * 4 + [C_WIDTH] * 2 + [D_MODEL] * 3
_OFFS = [sum(_WIDTHS[:i]) for i in range(len(_WIDTHS) + 1)]
PROJ_SEGS = tuple((_OFFS[i], _OFFS[i + 1]) for i in range(len(_WIDTHS)))
(SEG_U, SEG_V, SEG_ZA, SEG_XB, SEG_BG, SEG_CG, SEG_ZB, SEG_XC, SEG_ZC,
 SEG_GA, SEG_GB, SEG_GC) = PROJ_SEGS
SEG_UV = (SEG_U[0], SEG_V[1])
PROJ_JOBS = ("xc", "za", "cg", "xb", "bg", "zb", "zc", "ga", "gb", "gc")
MXU_ORDER = ("gb", "za", "cg", "xb", "sp", "bg", "xc", "zb", "pa", "zc", "po", "ga", "pb", "gc",
             "pc", "u_next", "wo", "v_next")

LANES = 128
BF16_SUBLANES = 16
HALO = 16
TS = 512
W_ROWS = 64
N_STAGE = 3
O_ROWS = 256
VMEM_LIMIT_BYTES = 56 * 1024 * 1024


HALVED_SEGS = (SEG_U, SEG_V, SEG_ZA, SEG_ZB, SEG_ZC, SEG_GA, SEG_GB, SEG_GC)
_GELU_C1 = 2.0 * math.sqrt(2.0 / math.pi)
_GELU_C3 = 8.0 * 0.044715 * math.sqrt(2.0 / math.pi)


def _gelu_of_half(xh):
    t = jnp.tanh(xh * (_GELU_C1 + _GELU_C3 * (xh * xh)))
    return xh + xh * t


def _silu_of_half(zh):
    return zh + zh * jnp.tanh(zh)


def _twice_sigmoid_of_half(gh):
    return jnp.tanh(gh) + 1.0


def _rmsnorm_rows(x, gain):
    ms = jnp.mean(x * x, axis=-1, keepdims=True)
    return x * lax.rsqrt(ms + RMS_EPS) * gain


def _cast_w_in_rows(src, dst_ref, dst_rows):
    for lo, hi in PROJ_SEGS:
        w = src(lo, hi)
        if (lo, hi) in HALVED_SEGS:
            w = w * 0.5
        dst_ref[dst_rows, lo:hi] = w.astype(jnp.bfloat16)


def _w_in_copy(w_in_hbm_ref, stage_ref, sem_ref, layer, i):
    return pltpu.make_async_copy(
        w_in_hbm_ref.at[layer, pl.ds(i * W_ROWS, W_ROWS), :],
        stage_ref.at[i % N_STAGE], sem_ref.at[i % N_STAGE])


def _load_w_in(w_in_hbm_ref, w_in_ref, stage_ref, sem_ref, layer):
    n_blocks = D_MODEL // W_ROWS
    for i in range(N_STAGE):
        _w_in_copy(w_in_hbm_ref, stage_ref, sem_ref, layer, i).start()

    def convert_block(i, carry):
        _w_in_copy(w_in_hbm_ref, stage_ref, sem_ref, layer, i).wait()
        rows = pl.ds(pl.multiple_of(i * W_ROWS, W_ROWS), W_ROWS)
        _cast_w_in_rows(lambda lo, hi: stage_ref[i % N_STAGE, :, lo:hi], w_in_ref, rows)

        @pl.when(i + N_STAGE < n_blocks)
        def _():
            _w_in_copy(w_in_hbm_ref, stage_ref, sem_ref, layer, i + N_STAGE).start()

        return carry

    lax.fori_loop(0, n_blocks, convert_block, 0)


def _load_out_weights(hbm_refs, vmem_refs, stage_ref, sem_ref, layer):
    blocks = [(src, dst, r) for src, dst in zip(hbm_refs, vmem_refs)
              for r in range(0, dst.shape[0], O_ROWS)]

    def copy(i):
        src, _, r = blocks[i]
        return pltpu.make_async_copy(src.at[layer, pl.ds(r, O_ROWS), :],
                                     stage_ref.at[pl.ds((i % 2) * O_ROWS, O_ROWS), :],
                                     sem_ref.at[i % 2])

    copy(0).start()
    copy(1).start()
    for i, (_, dst, r) in enumerate(blocks):
        copy(i).wait()
        w = stage_ref[(i % 2) * O_ROWS:(i % 2 + 1) * O_ROWS, :]
        if dst is vmem_refs[-1]:
            w = w * 0.5
        dst[r:r + O_ROWS, :] = w.astype(jnp.bfloat16)
        if i + 2 < len(blocks):
            copy(i + 2).start()


class _NextLayerWIn:
    def __init__(self, w_in_hbm_ref, w_next_hbm_ref, stage_ref, cast_ref, sem_ref, layer,
                 step, n_steps):
        self.refs = (w_in_hbm_ref, w_next_hbm_ref, stage_ref, cast_ref, sem_ref)
        self.layer, self.step, self.n_steps = layer, step, n_steps
        self.rows = D_MODEL // n_steps

    def _rows_of(self, block):
        return pl.ds(pl.multiple_of(block * self.rows, self.rows), self.rows)

    def _fetch(self, block):
        w_in_hbm_ref, _, stage_ref, _, sem_ref = self.refs
        return pltpu.make_async_copy(w_in_hbm_ref.at[self.layer + 1, self._rows_of(block), :],
                                     stage_ref.at[block % 2, pl.ds(0, self.rows), :],
                                     sem_ref.at[block % 2])

    def _write_back(self, block):
        _, w_next_hbm_ref, _, cast_ref, sem_ref = self.refs
        return pltpu.make_async_copy(cast_ref.at[block % 2],
                                     w_next_hbm_ref.at[self._rows_of(block), :],
                                     sem_ref.at[2 + block % 2])

    def start_first_fetch(self):
        self._fetch(0).start()

    def begin_step(self):
        step = self.step
        self._fetch(step).wait()

        @pl.when(step + 1 < self.n_steps)
        def _():
            self._fetch(step + 1).start()

        @pl.when(step >= 2)
        def _():
            self._write_back(step - 2).wait()

    def cast(self):
        _, _, stage_ref, cast_ref, _ = self.refs
        slot = self.step % 2
        _cast_w_in_rows(lambda lo, hi: stage_ref[slot, 0:self.rows, lo:hi], cast_ref.at[slot],
                        slice(None))

    def end_step(self):
        step = self.step
        self._write_back(step).start()

        @pl.when(step == self.n_steps - 1)
        def _():
            self._write_back(step - 1).wait()
            self._write_back(step).wait()


def _prepare_small_weights(w_s_ref, b_s_ref, w_pool_ref, w_sp_ref, b_sp_ref, w_pool_bd_ref):
    f32 = jnp.float32
    bf16 = jnp.bfloat16
    row = lax.broadcasted_iota(jnp.int32, (CHUNK, CHUNK), 0)
    col = lax.broadcasted_iota(jnp.int32, (CHUNK, CHUNK), 1)
    causal = row >= col
    first_group = col < A_HEAD
    b_t = b_s_ref[...].T
    for jb in range(A_GROUPS // 2):
        w_pair = [jnp.where(causal, w_s_ref[2 * jb + k], 0.0) for k in range(2)]
        w_sp_ref[jb] = jnp.concatenate(w_pair, axis=1).astype(bf16)
        b_pair = [jnp.broadcast_to(b_t[:, 2 * jb + k:2 * jb + k + 1], (CHUNK, LANES))
                  for k in range(2)]
        b_sp_ref[:, jb * LANES:(jb + 1) * LANES] = jnp.where(first_group, b_pair[0], b_pair[1])
    zero = jnp.zeros((C_GROUP, C_GROUP), f32)
    for i in range(len(POOL_WINDOWS) // 2):
        top = jnp.concatenate([w_pool_ref[2 * i], zero], axis=1)
        bottom = jnp.concatenate([zero, w_pool_ref[2 * i + 1]], axis=1)
        w_pool_bd_ref[i] = jnp.concatenate([top, bottom], axis=0).astype(bf16)


def _layer_kernel(*refs, names, layer, steps_per_seq, n_steps, apply_final_norm):
    f32 = jnp.float32
    bf16 = jnp.bfloat16
    r = dict(zip(names, refs, strict=True))
    x_ref, x_next_ref, out_ref = r["x"], r["x_next"], r["out"]
    w_in_ref, sem_ref = r["w_in"], r["sem"]
    w_pa_ref, w_pb_ref, w_pc_ref, w_o_ref = r["w_pa"], r["w_pb"], r["w_pc"], r["w_o"]
    w_sp_ref, b_sp_ref, w_pool_bd_ref = r["w_sp"], r["b_sp"], r["w_pool_bd"]
    h_ref, puv_ref, cx_ref, xc_ref = r["h"], r["puv"], r["cx"], r["xc"]
    norm_g_ref, ln_g_ref, ln_b_ref = r["norm_g"], r["ln_g"], r["ln_b"]
    conv_w_ref, conv_b_ref, pool_scale_ref = r["conv_w"], r["conv_b"], r["pool_scale"]
    final_g_ref = r["final_g"]
    j = pl.program_id(1)
    step = pl.program_id(0) * steps_per_seq + j
    next_w_in = None
    if "w_in_next" in r:
        next_w_in = _NextLayerWIn(r["w_in_f32_hbm"], r["w_in_next"], r["stage"], r["cast"],
                                  r["next_sem"], layer, step, n_steps)

    this_layer = slice(layer, layer + 1)
    norm_g, final_g = norm_g_ref[this_layer, :], final_g_ref[...]
    ln_g, ln_b = ln_g_ref[this_layer, :], ln_b_ref[this_layer, :]
    conv_b, pool_scale = conv_b_ref[this_layer, :], pool_scale_ref[this_layer, :]
    conv_w = [conv_w_ref[layer, k:k + 1, :] for k in range(CONV_WIDTH)]

    @pl.when(j == 0)
    def _():
        cx_ref[0:HALO, :] = jnp.zeros((HALO, B_WIDTH), f32)
        xc_ref[0:HALO, :] = jnp.zeros((HALO, C_WIDTH), f32)

    def proj(seg):
        return jnp.dot(h_ref[...], w_in_ref[:, seg[0]:seg[1]], preferred_element_type=f32)

    @pl.when(step == 0)
    def _():
        if "w_in_bf16_hbm" in r:
            ready_w_in = pltpu.make_async_copy(r["w_in_bf16_hbm"], w_in_ref, sem_ref.at[2])
            ready_w_in.start()
        else:
            _load_w_in(r["w_in_f32_hbm"], w_in_ref, r["stage"], sem_ref, layer)
        _load_out_weights((r["w_pa_hbm"], r["w_pb_hbm"], r["w_pc_hbm"], r["w_o_hbm"]),
                          (w_pa_ref, w_pb_ref, w_pc_ref, w_o_ref), puv_ref, sem_ref, layer)
        _prepare_small_weights(r["w_s"], r["b_s"], r["w_pool"], w_sp_ref, b_sp_ref,
                               w_pool_bd_ref)
        h_ref[...] = _rmsnorm_rows(x_ref[0], norm_g).astype(bf16)
        if "w_in_bf16_hbm" in r:
            ready_w_in.wait()
        puv_ref[...] = proj(SEG_UV)
        if next_w_in is not None:
            next_w_in.start_first_fetch()

    if next_w_in is not None:
        next_w_in.begin_step()
        next_w_in.cast()

    n_chunks = TS // CHUNK
    lane = lax.broadcasted_iota(jnp.int32, (CHUNK, LANES), 1)
    first_group = lane < A_HEAD

    def spatial_mix(vn):
        sg_blocks = []
        for jb in range(A_WIDTH // LANES):
            rhs = []
            for c in range(n_chunks):
                vb = vn[c * CHUNK:(c + 1) * CHUNK, jb * LANES:(jb + 1) * LANES]
                rhs.append(jnp.concatenate([jnp.where(first_group, vb, 0.0),
                                            jnp.where(first_group, 0.0, vb)], axis=0))
            rhs = jnp.concatenate(rhs, axis=1).astype(bf16)
            mixed = jnp.dot(w_sp_ref[jb], rhs, preferred_element_type=f32)
            sg_blocks.append(jnp.concatenate(
                [mixed[:, c * LANES:(c + 1) * LANES] for c in range(n_chunks)], axis=0))
        sg = jnp.concatenate(sg_blocks, axis=1)
        return sg + jnp.concatenate([b_sp_ref[...]] * n_chunks, axis=0)

    def layer_norm(v):
        mu = jnp.mean(v, axis=-1, keepdims=True)
        vc = v - mu
        var = jnp.mean(vc * vc, axis=-1, keepdims=True)
        return vc * lax.rsqrt(var + LN_EPS) * ln_g + ln_b

    def store_xc(e):
        xc_ref[HALO:HALO + TS, :] = proj(SEG_XC)

    def store_cx(e):
        cx_ref[HALO:HALO + TS, :] = e["cg"] * e["xb"]

    def short_conv(e):
        conv = conv_b + conv_w[CONV_WIDTH - 1] * cx_ref[HALO:HALO + TS, :]
        for k in range(CONV_WIDTH - 1):
            back = CONV_WIDTH - 1 - k
            conv = conv + conv_w[k] * cx_ref[HALO - back:HALO - back + TS, :]
        cx_ref[0:HALO, :] = cx_ref[TS:TS + HALO, :]
        return conv

    def pooled_pairs(e):
        t1 = (j * TS + 1 + lax.broadcasted_iota(jnp.int32, (TS, C_GROUP), 0)).astype(f32)
        inv_t1 = 1.0 / t1
        pooled = []
        for gi, w in enumerate(POOL_WINDOWS):
            cols = slice(gi * C_GROUP, (gi + 1) * C_GROUP)
            ext = xc_ref[:, cols]
            win = ext
            span = 1
            while span < w:
                win = win + pltpu.roll(win, span, axis=0)
                span *= 2
            pooled.append(win[HALO:, :] * jnp.maximum(inv_t1, 1.0 / w) - ext[HALO:, :])
        xc_ref[0:HALO, :] = xc_ref[TS:TS + HALO, :]
        return [jnp.concatenate(pooled[2 * half:2 * half + 2], axis=1).astype(bf16)
                for half in range(2)]

    def pool_dots(e):
        groups = [jnp.dot(lhs, w_pool_bd_ref[half], preferred_element_type=f32)
                  for half, lhs in enumerate(e["pooled"])]
        return jnp.concatenate(groups, axis=1) * pool_scale

    def next_tile_u(e):
        h_ref[...] = _rmsnorm_rows(x_next_ref[0], norm_g).astype(bf16)
        puv_ref[:, 0:A_WIDTH] = proj(SEG_U)

    def next_tile_v(e):
        puv_ref[:, A_WIDTH:] = proj(SEG_V)

    def write_out(e):
        y = x_ref[0] + e["wo"]
        if apply_final_norm:
            y = _rmsnorm_rows(y, final_g)
        out_ref[0] = y

    def bdot(lhs, w_ref):
        return jnp.dot(lhs, w_ref[...], preferred_element_type=f32)

    matmul_jobs = {
        "xc": ((), store_xc),
        "za": ((), lambda e: proj(SEG_ZA)),
        "cg": ((), lambda e: proj(SEG_CG)),
        "xb": ((), lambda e: proj(SEG_XB)),
        "bg": ((), lambda e: proj(SEG_BG)),
        "zb": ((), lambda e: proj(SEG_ZB)),
        "zc": ((), lambda e: proj(SEG_ZC)),
        "ga": ((), lambda e: proj(SEG_GA)),
        "gb": ((), lambda e: proj(SEG_GB)),
        "gc": ((), lambda e: proj(SEG_GC)),
        "sp": (("vn",), lambda e: spatial_mix(e["vn"])),
        "pa": (("ya",), lambda e: bdot(e["ya"], w_pa_ref)),
        "pb": (("yb",), lambda e: bdot(e["yb"], w_pb_ref)),
        "po": (("pooled",), pool_dots),
        "pc": (("yc",), lambda e: bdot(e["yc"], w_pc_ref)),
        "u_next": (PROJ_JOBS, next_tile_u),
        "wo": (("merged_c",), lambda e: bdot(e["merged_c"].astype(bf16), w_o_ref)),
        "v_next": (("u_next",), next_tile_v),
    }
    elementwise = (
        ("u", (), lambda e: _gelu_of_half(puv_ref[:, 0:A_WIDTH])),
        ("vn", (), lambda e: layer_norm(_gelu_of_half(puv_ref[:, A_WIDTH:]))),
        ("cx", ("cg", "xb"), store_cx),
        ("ya", ("u", "sp", "za"),
         lambda e: (e["u"] * e["sp"] * _silu_of_half(e["za"])).astype(bf16)),
        ("conv", ("cx",), short_conv),
        ("yb", ("bg", "conv", "zb"),
         lambda e: (e["bg"] * e["conv"] * _silu_of_half(e["zb"])).astype(bf16)),
        ("pooled", ("xc",), pooled_pairs),
        ("yc", ("po", "zc"), lambda e: (e["po"] * _silu_of_half(e["zc"])).astype(bf16)),
        ("merged_a", ("ga", "pa"), lambda e: _twice_sigmoid_of_half(e["ga"]) * e["pa"]),
        ("merged_b", ("merged_a", "gb", "pb"),
         lambda e: e["merged_a"] + _twice_sigmoid_of_half(e["gb"]) * e["pb"]),
        ("gate_c", ("gc",), lambda e: _twice_sigmoid_of_half(e["gc"])),
        ("merged_c", ("merged_b", "gate_c", "pc"),
         lambda e: e["merged_b"] + e["gate_c"] * e["pc"]),
        ("out", ("wo",), write_out),
    )
    env = {}

    def trace_ready_elementwise():
        progress = True
        while progress:
            progress = False
            for name, needs, fn in elementwise:
                if name not in env and all(n in env for n in needs):
                    env[name] = fn(env)
                    progress = True

    assert sorted(MXU_ORDER) == sorted(matmul_jobs)
    for name in MXU_ORDER:
        trace_ready_elementwise()
        needs, fn = matmul_jobs[name]
        assert all(n in env for n in needs), (name, needs)
        env[name] = fn(env)
    trace_ready_elementwise()
    assert len(env) == len(matmul_jobs) + len(elementwise)
    if next_w_in is not None:
        next_w_in.end_step()


def _whole_spec(array):
    zeros = (0,) * array.ndim
    return pl.BlockSpec(array.shape, lambda b, j: zeros, pipeline_mode=pl.Buffered(1))


def _layer_spec(array, layer):
    zeros = (0,) * (array.ndim - 1)
    return pl.BlockSpec((None,) + array.shape[1:], lambda b, j: (layer,) + zeros,
                        pipeline_mode=pl.Buffered(1))


VECTOR_NAMES = ("norm_g", "ln_g", "ln_b", "conv_w", "conv_b", "pool_scale", "final_g")
PER_LAYER_NAMES = ("w_s", "b_s", "w_pool")
OUT_WEIGHT_NAMES = ("w_pa", "w_pb", "w_pc", "w_o")


def _layer(x, w_in, w_in_bf16, vectors, per_layer, out_weights, layer, *, convert_next,
           apply_final_norm):
    bsz, seq, _ = x.shape
    assert seq % TS == 0 and TS % CHUNK == 0 and TS >= HALO
    assert w_in.dtype == jnp.float32 and w_in.shape[1:] == (D_MODEL, IN_TOTAL)
    assert D_MODEL % W_ROWS == 0 and TS == 2 * O_ROWS and N_STAGE >= 3
    steps_per_seq = seq // TS
    n_steps = bsz * steps_per_seq
    bf16 = jnp.bfloat16
    f32 = jnp.float32

    def next_tile(b, j):
        s = jnp.minimum(b * steps_per_seq + j + 1, n_steps - 1)
        return (s // steps_per_seq, s % steps_per_seq, 0)

    tile = (1, TS, D_MODEL)
    any_spec = pl.BlockSpec(memory_space=pl.ANY)
    inputs = [("x", x, pl.BlockSpec(tile, lambda b, j: (b, j, 0))),
              ("x_next", x, pl.BlockSpec(tile, next_tile)),
              ("w_in_f32_hbm", w_in, any_spec)]
    if w_in_bf16 is not None:
        inputs.append(("w_in_bf16_hbm", w_in_bf16, any_spec))
    inputs += [(n, a, _whole_spec(a)) for n, a in zip(VECTOR_NAMES, vectors, strict=True)]
    inputs += [(n, a, _layer_spec(a, layer))
               for n, a in zip(PER_LAYER_NAMES, per_layer, strict=True)]
    inputs += [(n + "_hbm", a, any_spec)
               for n, a in zip(OUT_WEIGHT_NAMES, out_weights, strict=True)]

    outputs = [("out", jax.ShapeDtypeStruct(x.shape, x.dtype),
                pl.BlockSpec(tile, lambda b, j: (b, j, 0)))]
    scratch = [
        ("w_in", pltpu.VMEM((D_MODEL, IN_TOTAL), bf16)),
        ("sem", pltpu.SemaphoreType.DMA((N_STAGE,))),
        ("w_pa", pltpu.VMEM((A_WIDTH, D_MODEL), bf16)),
        ("w_pb", pltpu.VMEM((B_WIDTH, D_MODEL), bf16)),
        ("w_pc", pltpu.VMEM((C_WIDTH, D_MODEL), bf16)),
        ("w_o", pltpu.VMEM((D_MODEL, D_MODEL), bf16)),
        ("w_sp", pltpu.VMEM((A_GROUPS // 2, CHUNK, 2 * CHUNK), bf16)),
        ("b_sp", pltpu.VMEM((CHUNK, A_WIDTH), f32)),
        ("w_pool_bd", pltpu.VMEM((len(POOL_WINDOWS) // 2, 2 * C_GROUP, 2 * C_GROUP), bf16)),
        ("h", pltpu.VMEM((TS, D_MODEL), bf16)),
        ("puv", pltpu.VMEM((TS, 2 * A_WIDTH), f32)),
        ("cx", pltpu.VMEM((TS + HALO, B_WIDTH), f32)),
        ("xc", pltpu.VMEM((TS + HALO, C_WIDTH), f32)),
    ]
    if w_in_bf16 is None or convert_next:
        scratch.append(("stage", pltpu.VMEM((N_STAGE, W_ROWS, IN_TOTAL), f32)))
    if convert_next:
        assert D_MODEL % n_steps == 0 and (D_MODEL // n_steps) % BF16_SUBLANES == 0
        assert D_MODEL // n_steps <= W_ROWS and n_steps >= 2
        outputs.append(("w_in_next", jax.ShapeDtypeStruct((D_MODEL, IN_TOTAL), bf16), any_spec))
        scratch += [("cast", pltpu.VMEM((2, D_MODEL // n_steps, IN_TOTAL), bf16)),
                    ("next_sem", pltpu.SemaphoreType.DMA((4,)))]

    names = tuple(n for n, *_ in inputs + outputs + scratch)
    results = pl.pallas_call(
        functools.partial(_layer_kernel, names=names, layer=layer, steps_per_seq=steps_per_seq,
                          n_steps=n_steps, apply_final_norm=apply_final_norm),
        out_shape=[s for _, s, _ in outputs],
        grid=(bsz, steps_per_seq),
        in_specs=[spec for _, _, spec in inputs],
        out_specs=[spec for _, _, spec in outputs],
        scratch_shapes=[s for _, s in scratch],
        compiler_params=pltpu.CompilerParams(
            dimension_semantics=("arbitrary", "arbitrary"),
            vmem_limit_bytes=VMEM_LIMIT_BYTES,
        ),
        name="hybrid_layer_final" if apply_final_norm else "hybrid_layer",
    )(*[a for _, a, _ in inputs])
    return results[0], (results[1] if convert_next else None)


def kernel(x, norm_g, w_in, ln_g, ln_b, w_s, b_s, conv_w, conv_b, w_pool, pool_scale,
           w_pa, w_pb, w_pc, w_o, final_g):
    vectors = (norm_g, ln_g, ln_b, conv_w, conv_b, pool_scale, final_g[None, :])
    per_layer = (w_s, b_s, w_pool)
    out_weights = (w_pa, w_pb, w_pc, w_o)
    w_in_bf16 = None
    for layer in range(DEPTH):
        last = layer == DEPTH - 1
        x, w_in_bf16 = _layer(x, w_in, w_in_bf16, vectors, per_layer, out_weights, layer,
                              convert_next=not last, apply_final_norm=last)
    return x
```

```python
import functools
import math

import jax
import jax.numpy as jnp
from jax import lax
from jax.experimental import pallas as pl
from jax.experimental.pallas import tpu as pltpu

D_MODEL = 1024
DEPTH = 2
CHUNK = 128
A_GROUPS = 8
A_WIDTH = 512
A_HEAD = A_WIDTH // A_GROUPS
B_WIDTH = 512
CONV_WIDTH = 3
C_WIDTH = 512
POOL_WINDOWS = (2, 4, 8, 16)
C_GROUP = C_WIDTH // len(POOL_WINDOWS)
IN_TOTAL = 3 * A_WIDTH + 4 * B_WIDTH + 2 * C_WIDTH + 3 * D_MODEL
RMS_EPS = 1e-6
LN_EPS = 1e-5

_WIDTHS = [A_WIDTH] * 3 + [B_WIDTH] * 4 + [C_WIDTH] * 2 + [D_MODEL] * 3
_OFFS = [sum(_WIDTHS[:i]) for i in range(len(_WIDTHS) + 1)]
PROJ_SEGS = tuple((_OFFS[i], _OFFS[i + 1]) for i in range(len(_WIDTHS)))
(SEG_U, SEG_V, SEG_ZA, SEG_XB, SEG_BG, SEG_CG, SEG_ZB, SEG_XC, SEG_ZC,
 SEG_GA, SEG_GB, SEG_GC) = PROJ_SEGS
SEG_UV = (SEG_U[0], SEG_V[1])
PROJ_JOBS = ("xc", "za", "cg", "xb", "bg", "zb", "zc", "ga", "gb", "gc")
MXU_ORDER = ("gb", "za", "cg", "xb", "sp", "bg", "zb", "xc", "pa", "zc", "po", "ga", "pb", "gc",
             "u_next", "pc", "wo", "v_next")

LANES = 128
BF16_SUBLANES = 16
HALO = 16
TS = 512
W_ROWS = 64
N_STAGE = 3
O_ROWS = 256
VMEM_LIMIT_BYTES = 56 * 1024 * 1024


HALVED_SEGS = (SEG_U, SEG_V, SEG_ZA, SEG_ZB, SEG_ZC, SEG_GA, SEG_GB, SEG_GC)
_GELU_C1 = 2.0 * math.sqrt(2.0 / math.pi)
_GELU_C3 = 8.0 * 0.044715 * math.sqrt(2.0 / math.pi)


def _gelu_of_half(xh):
    t = jnp.tanh(xh * (_GELU_C1 + _GELU_C3 * (xh * xh)))
    return xh + xh * t


def _silu_of_half(zh):
    return zh + zh * jnp.tanh(zh)


def _twice_sigmoid_of_half(gh):
    return jnp.tanh(gh) + 1.0


def _rmsnorm_rows(x, gain):
    ms = jnp.mean(x * x, axis=-1, keepdims=True)
    return x * lax.rsqrt(ms + RMS_EPS) * gain


def _cast_w_in_rows(src, dst_ref, dst_rows):
    for lo, hi in PROJ_SEGS:
        w = src(lo, hi)
        if (lo, hi) in HALVED_SEGS:
            w = w * 0.5
        dst_ref[dst_rows, lo:hi] = w.astype(jnp.bfloat16)


def _w_in_copy(w_in_hbm_ref, stage_ref, sem_ref, layer, i):
    return pltpu.make_async_copy(
        w_in_hbm_ref.at[layer, pl.ds(i * W_ROWS, W_ROWS), :],
        stage_ref.at[i % N_STAGE], sem_ref.at[i % N_STAGE])


def _load_w_in(w_in_hbm_ref, w_in_ref, stage_ref, sem_ref, layer):
    n_blocks = D_MODEL // W_ROWS
    for i in range(N_STAGE):
        _w_in_copy(w_in_hbm_ref, stage_ref, sem_ref, layer, i).start()

    def convert_block(i, carry):
        _w_in_copy(w_in_hbm_ref, stage_ref, sem_ref, layer, i).wait()
        rows = pl.ds(pl.multiple_of(i * W_ROWS, W_ROWS), W_ROWS)
        _cast_w_in_rows(lambda lo, hi: stage_ref[i % N_STAGE, :, lo:hi], w_in_ref, rows)

        @pl.when(i + N_STAGE < n_blocks)
        def _():
            _w_in_copy(w_in_hbm_ref, stage_ref, sem_ref, layer, i + N_STAGE).start()

        return carry

    lax.fori_loop(0, n_blocks, convert_block, 0)


def _load_out_weights(hbm_refs, vmem_refs, stage_ref, sem_ref, layer):
    blocks = [(src, dst, r) for src, dst in zip(hbm_refs, vmem_refs)
              for r in range(0, dst.shape[0], O_ROWS)]

    def copy(i):
        src, _, r = blocks[i]
        return pltpu.make_async_copy(src.at[layer, pl.ds(r, O_ROWS), :],
                                     stage_ref.at[pl.ds((i % 2) * O_ROWS, O_ROWS), :],
                                     sem_ref.at[i % 2])

    copy(0).start()
    copy(1).start()
    for i, (_, dst, r) in enumerate(blocks):
        copy(i).wait()
        w = stage_ref[(i % 2) * O_ROWS:(i % 2 + 1) * O_ROWS, :]
        if dst is vmem_refs[-1]:
            w = w * 0.5
        dst[r:r + O_ROWS, :] = w.astype(jnp.bfloat16)
        if i + 2 < len(blocks):
            copy(i + 2).start()


class _NextLayerWIn:
    def __init__(self, w_in_hbm_ref, w_next_hbm_ref, stage_ref, cast_ref, sem_ref, layer,
                 step, n_steps):
        self.refs = (w_in_hbm_ref, w_next_hbm_ref, stage_ref, cast_ref, sem_ref)
        self.layer, self.step, self.n_steps = layer, step, n_steps
        self.rows = D_MODEL // n_steps

    def _rows_of(self, block):
        return pl.ds(pl.multiple_of(block * self.rows, self.rows), self.rows)

    def _fetch(self, block):
        w_in_hbm_ref, _, stage_ref, _, sem_ref = self.refs
        return pltpu.make_async_copy(w_in_hbm_ref.at[self.layer + 1, self._rows_of(block), :],
                                     stage_ref.at[block % 2, pl.ds(0, self.rows), :],
                                     sem_ref.at[block % 2])

    def _write_back(self, block):
        _, w_next_hbm_ref, _, cast_ref, sem_ref = self.refs
        return pltpu.make_async_copy(cast_ref.at[block % 2],
                                     w_next_hbm_ref.at[self._rows_of(block), :],
                                     sem_ref.at[2 + block % 2])

    def start_first_fetch(self):
        self._fetch(0).start()

    def begin_step(self):
        step = self.step
        self._fetch(step).wait()

        @pl.when(step + 1 < self.n_steps)
        def _():
            self._fetch(step + 1).start()

        @pl.when(step >= 2)
        def _():
            self._write_back(step - 2).wait()

    def cast(self):
        _, _, stage_ref, cast_ref, _ = self.refs
        slot = self.step % 2
        _cast_w_in_rows(lambda lo, hi: stage_ref[slot, 0:self.rows, lo:hi], cast_ref.at[slot],
                        slice(None))

    def end_step(self):
        step = self.step
        self._write_back(step).start()

        @pl.when(step == self.n_steps - 1)
        def _():
            self._write_back(step - 1).wait()
            self._write_back(step).wait()


def _prepare_small_weights(w_s_ref, b_s_ref, w_pool_ref, w_sp_ref, b_sp_ref, w_pool_bd_ref):
    f32 = jnp.float32
    bf16 = jnp.bfloat16
    row = lax.broadcasted_iota(jnp.int32, (CHUNK, CHUNK), 0)
    col = lax.broadcasted_iota(jnp.int32, (CHUNK, CHUNK), 1)
    causal = row >= col
    first_group = col < A_HEAD
    b_t = b_s_ref[...].T
    for jb in range(A_GROUPS // 2):
        w_pair = [jnp.where(causal, w_s_ref[2 * jb + k], 0.0) for k in range(2)]
        w_sp_ref[jb] = jnp.concatenate(w_pair, axis=1).astype(bf16)
        b_pair = [jnp.broadcast_to(b_t[:, 2 * jb + k:2 * jb + k + 1], (CHUNK, LANES))
                  for k in range(2)]
        b_sp_ref[:, jb * LANES:(jb + 1) * LANES] = jnp.where(first_group, b_pair[0], b_pair[1])
    zero = jnp.zeros((C_GROUP, C_GROUP), f32)
    for i in range(len(POOL_WINDOWS) // 2):
        top = jnp.concatenate([w_pool_ref[2 * i], zero], axis=1)
        bottom = jnp.concatenate([zero, w_pool_ref[2 * i + 1]], axis=1)
        w_pool_bd_ref[i] = jnp.concatenate([top, bottom], axis=0).astype(bf16)


def _layer_kernel(*refs, names, layer, steps_per_seq, n_steps, apply_final_norm):
    f32 = jnp.float32
    bf16 = jnp.bfloat16
    r = dict(zip(names, refs, strict=True))
    x_ref, x_next_ref, out_ref = r["x"], r["x_next"], r["out"]
    w_in_ref, sem_ref = r["w_in"], r["sem"]
    w_pa_ref, w_pb_ref, w_pc_ref, w_o_ref = r["w_pa"], r["w_pb"], r["w_pc"], r["w_o"]
    w_sp_ref, b_sp_ref, w_pool_bd_ref = r["w_sp"], r["b_sp"], r["w_pool_bd"]
    h_ref, puv_ref, cx_ref, xc_ref = r["h"], r["puv"], r["cx"], r["xc"]
    norm_g_ref, ln_g_ref, ln_b_ref = r["norm_g"], r["ln_g"], r["ln_b"]
    conv_w_ref, conv_b_ref, pool_scale_ref = r["conv_w"], r["conv_b"], r["pool_scale"]
    final_g_ref = r["final_g"]
    j = pl.program_id(1)
    step = pl.program_id(0) * steps_per_seq + j
    next_w_in = None
    if "w_in_next" in r:
        next_w_in = _NextLayerWIn(r["w_in_f32_hbm"], r["w_in_next"], r["stage"], r["cast"],
                                  r["next_sem"], layer, step, n_steps)

    this_layer = slice(layer, layer + 1)
    norm_g, final_g = norm_g_ref[this_layer, :], final_g_ref[...]
    ln_g, ln_b = ln_g_ref[this_layer, :], ln_b_ref[this_layer, :]
    conv_b, pool_scale = conv_b_ref[this_layer, :], pool_scale_ref[this_layer, :]
    conv_w = [conv_w_ref[layer, k:k + 1, :] for k in range(CONV_WIDTH)]

    @pl.when(j == 0)
    def _():
        cx_ref[0:HALO, :] = jnp.zeros((HALO, B_WIDTH), f32)
        xc_ref[0:HALO, :] = jnp.zeros((HALO, C_WIDTH), f32)

    def proj(seg):
        return jnp.dot(h_ref[...], w_in_ref[:, seg[0]:seg[1]], preferred_element_type=f32)

    @pl.when(step == 0)
    def _():
        if "w_in_bf16_hbm" in r:
            ready_w_in = pltpu.make_async_copy(r["w_in_bf16_hbm"], w_in_ref, sem_ref.at[2])
            ready_w_in.start()
        else:
            _load_w_in(r["w_in_f32_hbm"], w_in_ref, r["stage"], sem_ref, layer)
        _load_out_weights((r["w_pa_hbm"], r["w_pb_hbm"], r["w_pc_hbm"], r["w_o_hbm"]),
                          (w_pa_ref, w_pb_ref, w_pc_ref, w_o_ref), puv_ref, sem_ref, layer)
        _prepare_small_weights(r["w_s"], r["b_s"], r["w_pool"], w_sp_ref, b_sp_ref,
                               w_pool_bd_ref)
        h_ref[...] = _rmsnorm_rows(x_ref[0], norm_g).astype(bf16)
        if "w_in_bf16_hbm" in r:
            ready_w_in.wait()
        puv_ref[...] = proj(SEG_UV)
        if next_w_in is not None:
            next_w_in.start_first_fetch()

    if next_w_in is not None:
        next_w_in.begin_step()
        next_w_in.cast()

    n_chunks = TS // CHUNK
    lane = lax.broadcasted_iota(jnp.int32, (CHUNK, LANES), 1)
    first_group = lane < A_HEAD

    def spatial_mix(vn):
        sg_blocks = []
        for jb in range(A_WIDTH // LANES):
            rhs = []
            for c in range(n_chunks):
                vb = vn[c * CHUNK:(c + 1) * CHUNK, jb * LANES:(jb + 1) * LANES]
                rhs.append(jnp.concatenate([jnp.where(first_group, vb, 0.0),
                                            jnp.where(first_group, 0.0, vb)], axis=0))
            rhs = jnp.concatenate(rhs, axis=1).astype(bf16)
            mixed = jnp.dot(w_sp_ref[jb], rhs, preferred_element_type=f32)
            sg_blocks.append(jnp.concatenate(
                [mixed[:, c * LANES:(c + 1) * LANES] for c in range(n_chunks)], axis=0))
        sg = jnp.concatenate(sg_blocks, axis=1)
        return sg + jnp.concatenate([b_sp_ref[...]] * n_chunks, axis=0)

    def layer_norm(v):
        mu = jnp.mean(v, axis=-1, keepdims=True)
        vc = v - mu
        var = jnp.mean(vc * vc, axis=-1, keepdims=True)
        return vc * lax.rsqrt(var + LN_EPS) * ln_g + ln_b

    def store_xc(e):
        xc_ref[HALO:HALO + TS, :] = proj(SEG_XC)

    def store_cx(e):
        cx_ref[HALO:HALO + TS, :] = e["cg"] * e["xb"]

    def short_conv(e):
        conv = conv_b + conv_w[CONV_WIDTH - 1] * cx_ref[HALO:HALO + TS, :]
        for k in range(CONV_WIDTH - 1):
            back = CONV_WIDTH - 1 - k
            conv = conv + conv_w[k] * cx_ref[HALO - back:HALO - back + TS, :]
        cx_ref[0:HALO, :] = cx_ref[TS:TS + HALO, :]
        return conv

    def pooled_pairs(e):
        t1 = (j * TS + 1 + lax.broadcasted_iota(jnp.int32, (TS, C_GROUP), 0)).astype(f32)
        inv_t1 = 1.0 / t1
        pooled = []
        for gi, w in enumerate(POOL_WINDOWS):
            cols = slice(gi * C_GROUP, (gi + 1) * C_GROUP)
            ext = xc_ref[:, cols]
            win = ext
            span = 1
            while span < w:
                win = win + pltpu.roll(win, span, axis=0)
                span *= 2
            pooled.append(win[HALO:, :] * jnp.maximum(inv_t1, 1.0 / w) - ext[HALO:, :])
        xc_ref[0:HALO, :] = xc_ref[TS:TS + HALO, :]
        return [jnp.concatenate(pooled[2 * half:2 * half + 2], axis=1).astype(bf16)
                for half in range(2)]

    def pool_dots(e):
        groups = [jnp.dot(lhs, w_pool_bd_ref[half], preferred_element_type=f32)
                  for half, lhs in enumerate(e["pooled"])]
        return jnp.concatenate(groups, axis=1) * pool_scale

    def next_tile_u(e):
        h_ref[...] = _rmsnorm_rows(x_next_ref[0], norm_g).astype(bf16)
        puv_ref[:, 0:A_WIDTH] = proj(SEG_U)

    def next_tile_v(e):
        puv_ref[:, A_WIDTH:] = proj(SEG_V)

    def write_out(e):
        y = x_ref[0] + e["wo"]
        if apply_final_norm:
            y = _rmsnorm_rows(y, final_g)
        out_ref[0] = y

    def bdot(lhs, w_ref):
        return jnp.dot(lhs, w_ref[...], preferred_element_type=f32)

    matmul_jobs = {
        "xc": ((), store_xc),
        "za": ((), lambda e: proj(SEG_ZA)),
        "cg": ((), lambda e: proj(SEG_CG)),
        "xb": ((), lambda e: proj(SEG_XB)),
        "bg": ((), lambda e: proj(SEG_BG)),
        "zb": ((), lambda e: proj(SEG_ZB)),
        "zc": ((), lambda e: proj(SEG_ZC)),
        "ga": ((), lambda e: proj(SEG_GA)),
        "gb": ((), lambda e: proj(SEG_GB)),
        "gc": ((), lambda e: proj(SEG_GC)),
        "sp": (("vn",), lambda e: spatial_mix(e["vn"])),
        "pa": (("ya",), lambda e: bdot(e["ya"], w_pa_ref)),
        "pb": (("yb",), lambda e: bdot(e["yb"], w_pb_ref)),
        "po": (("pooled",), pool_dots),
        "pc": (("yc",), lambda e: bdot(e["yc"], w_pc_ref)),
        "u_next": (PROJ_JOBS, next_tile_u),
        "wo": (("merged_c",), lambda e: bdot(e["merged_c"].astype(bf16), w_o_ref)),
        "v_next": (("u_next",), next_tile_v),
    }
    elementwise = (
        ("u", (), lambda e: _gelu_of_half(puv_ref[:, 0:A_WIDTH])),
        ("vn", (), lambda e: layer_norm(_gelu_of_half(puv_ref[:, A_WIDTH:]))),
        ("cx", ("cg", "xb"), store_cx),
        ("ya", ("u", "sp", "za"),
         lambda e: (e["u"] * e["sp"] * _silu_of_half(e["za"])).astype(bf16)),
        ("conv", ("cx",), short_conv),
        ("yb", ("bg", "conv", "zb"),
         lambda e: (e["bg"] * e["conv"] * _silu_of_half(e["zb"])).astype(bf16)),
        ("pooled", ("xc",), pooled_pairs),
        ("yc", ("po", "zc"), lambda e: (e["po"] * _silu_of_half(e["zc"])).astype(bf16)),
        ("merged_a", ("ga", "pa"), lambda e: _twice_sigmoid_of_half(e["ga"]) * e["pa"]),
        ("merged_b", ("merged_a", "gb", "pb"),
         lambda e: e["merged_a"] + _twice_sigmoid_of_half(e["gb"]) * e["pb"]),
        ("gate_c", ("gc",), lambda e: _twice_sigmoid_of_half(e["gc"])),
        ("merged_c", ("merged_b", "gate_c", "pc"),
         lambda e: e["merged_b"] + e["gate_c"] * e["pc"]),
        ("out", ("wo",), write_out),
    )
    env = {}

    def trace_ready_elementwise():
        progress = True
        while progress:
            progress = False
            for name, needs, fn in elementwise:
                if name not in env and all(n in env for n in needs):
                    env[name] = fn(env)
                    progress = True

    assert sorted(MXU_ORDER) == sorted(matmul_jobs)
    for name in MXU_ORDER:
        trace_ready_elementwise()
        needs, fn = matmul_jobs[name]
        assert all(n in env for n in needs), (name, needs)
        env[name] = fn(env)
    trace_ready_elementwise()
    assert len(env) == len(matmul_jobs) + len(elementwise)
    if next_w_in is not None:
        next_w_in.end_step()


def _whole_spec(array):
    zeros = (0,) * array.ndim
    return pl.BlockSpec(array.shape, lambda b, j: zeros, pipeline_mode=pl.Buffered(1))


def _layer_spec(array, layer):
    zeros = (0,) * (array.ndim - 1)
    return pl.BlockSpec((None,) + array.shape[1:], lambda b, j: (layer,) + zeros,
                        pipeline_mode=pl.Buffered(1))


VECTOR_NAMES = ("norm_g", "ln_g", "ln_b", "conv_w", "conv_b", "pool_scale", "final_g")
PER_LAYER_NAMES = ("w_s", "b_s", "w_pool")
OUT_WEIGHT_NAMES = ("w_pa", "w_pb", "w_pc", "w_o")


def _layer(x, w_in, w_in_bf16, vectors, per_layer, out_weights, layer, *, convert_next,
           apply_final_norm):
    bsz, seq, _ = x.shape
    assert seq % TS == 0 and TS % CHUNK == 0 and TS >= HALO
    assert w_in.dtype == jnp.float32 and w_in.shape[1:] == (D_MODEL, IN_TOTAL)
    assert D_MODEL % W_ROWS == 0 and TS == 2 * O_ROWS and N_STAGE >= 3
    steps_per_seq = seq // TS
    n_steps = bsz * steps_per_seq
    bf16 = jnp.bfloat16
    f32 = jnp.float32

    def next_tile(b, j):
        s = jnp.minimum(b * steps_per_seq + j + 1, n_steps - 1)
        return (s // steps_per_seq, s % steps_per_seq, 0)

    tile = (1, TS, D_MODEL)
    any_spec = pl.BlockSpec(memory_space=pl.ANY)
    inputs = [("x", x, pl.BlockSpec(tile, lambda b, j: (b, j, 0))),
              ("x_next", x, pl.BlockSpec(tile, next_tile)),
              ("w_in_f32_hbm", w_in, any_spec)]
    if w_in_bf16 is not None:
        inputs.append(("w_in_bf16_hbm", w_in_bf16, any_spec))
    inputs += [(n, a, _whole_spec(a)) for n, a in zip(VECTOR_NAMES, vectors, strict=True)]
    inputs += [(n, a, _layer_spec(a, layer))
               for n, a in zip(PER_LAYER_NAMES, per_layer, strict=True)]
    inputs += [(n + "_hbm", a, any_spec)
               for n, a in zip(OUT_WEIGHT_NAMES, out_weights, strict=True)]

    outputs = [("out", jax.ShapeDtypeStruct(x.shape, x.dtype),
                pl.BlockSpec(tile, lambda b, j: (b, j, 0)))]
    scratch = [
        ("w_in", pltpu.VMEM((D_MODEL, IN_TOTAL), bf16)),
        ("sem", pltpu.SemaphoreType.DMA((N_STAGE,))),
        ("w_pa", pltpu.VMEM((A_WIDTH, D_MODEL), bf16)),
        ("w_pb", pltpu.VMEM((B_WIDTH, D_MODEL), bf16)),
        ("w_pc", pltpu.VMEM((C_WIDTH, D_MODEL), bf16)),
        ("w_o", pltpu.VMEM((D_MODEL, D_MODEL), bf16)),
        ("w_sp", pltpu.VMEM((A_GROUPS // 2, CHUNK, 2 * CHUNK), bf16)),
        ("b_sp", pltpu.VMEM((CHUNK, A_WIDTH), f32)),
        ("w_pool_bd", pltpu.VMEM((len(POOL_WINDOWS) // 2, 2 * C_GROUP, 2 * C_GROUP), bf16)),
        ("h", pltpu.VMEM((TS, D_MODEL), bf16)),
        ("puv", pltpu.VMEM((TS, 2 * A_WIDTH), f32)),
        ("cx", pltpu.VMEM((TS + HALO, B_WIDTH), f32)),
        ("xc", pltpu.VMEM((TS + HALO, C_WIDTH), f32)),
    ]
    if w_in_bf16 is None or convert_next:
        scratch.append(("stage", pltpu.VMEM((N_STAGE, W_ROWS, IN_TOTAL), f32)))
    if convert_next:
        assert D_MODEL % n_steps == 0 and (D_MODEL // n_steps) % BF16_SUBLANES == 0
        assert D_MODEL // n_steps <= W_ROWS and n_steps >= 2
        outputs.append(("w_in_next", jax.ShapeDtypeStruct((D_MODEL, IN_TOTAL), bf16), any_spec))
        scratch += [("cast", pltpu.VMEM((2, D_MODEL // n_steps, IN_TOTAL), bf16)),
                    ("next_sem", pltpu.SemaphoreType.DMA((4,)))]

    names = tuple(n for n, *_ in inputs + outputs + scratch)
    results = pl.pallas_call(
        functools.partial(_layer_kernel, names=names, layer=layer, steps_per_seq=steps_per_seq,
                          n_steps=n_steps, apply_final_norm=apply_final_norm),
        out_shape=[s for _, s, _ in outputs],
        grid=(bsz, steps_per_seq),
        in_specs=[spec for _, _, spec in inputs],
        out_specs=[spec for _, _, spec in outputs],
        scratch_shapes=[s for _, s in scratch],
        compiler_params=pltpu.CompilerParams(
            dimension_semantics=("arbitrary", "arbitrary"),
            vmem_limit_bytes=VMEM_LIMIT_BYTES,
        ),
        name="hybrid_layer_final" if apply_final_norm else "hybrid_layer",
    )(*[a for _, a, _ in inputs])
    return results[0], (results[1] if convert_next else None)


def kernel(x, norm_g, w_in, ln_g, ln_b, w_s, b_s, conv_w, conv_b, w_pool, pool_scale,
           w_pa, w_pb, w_pc, w_o, final_g):
    vectors = (norm_g, ln_g, ln_b, conv_w, conv_b, pool_scale, final_g[None, :])
    per_layer = (w_s, b_s, w_pool)
    out_weights = (w_pa, w_pb, w_pc, w_o)
    w_in_bf16 = None
    for layer in range(DEPTH):
        last = layer == DEPTH - 1
        x, w_in_bf16 = _layer(x, w_in, w_in_bf16, vectors, per_layer, out_weights, layer,
                              convert_next=not last, apply_final_norm=last)
    return x
```

```python
import functools
import math

import jax
import jax.numpy as jnp
from jax import lax
from jax.experimental import pallas as pl
from jax.experimental.pallas import tpu as pltpu

D_MODEL = 1024
DEPTH = 2
CHUNK = 128
A_GROUPS = 8
A_WIDTH = 512
A_HEAD = A_WIDTH // A_GROUPS
B_WIDTH = 512
CONV_WIDTH = 3
C_WIDTH = 512
POOL_WINDOWS = (2, 4, 8, 16)
C_GROUP = C_WIDTH // len(POOL_WINDOWS)
IN_TOTAL = 3 * A_WIDTH + 4 * B_WIDTH + 2 * C_WIDTH + 3 * D_MODEL
RMS_EPS = 1e-6
LN_EPS = 1e-5

_WIDTHS = [A_WIDTH] * 3 + [B_WIDTH] * 4 + [C_WIDTH] * 2 + [D_MODEL] * 3
_OFFS = [sum(_WIDTHS[:i]) for i in range(len(_WIDTHS) + 1)]
PROJ_SEGS = tuple((_OFFS[i], _OFFS[i + 1]) for i in range(len(_WIDTHS)))
(SEG_U, SEG_V, SEG_ZA, SEG_XB, SEG_BG, SEG_CG, SEG_ZB, SEG_XC, SEG_ZC,
 SEG_GA, SEG_GB, SEG_GC) = PROJ_SEGS
SEG_UV = (SEG_U[0], SEG_V[1])
PROJ_JOBS = ("xc", "za", "cg", "xb", "bg", "zb", "zc", "ga", "gb", "gc")
MXU_ORDER = ("gb", "za", "cg", "xb", "sp", "bg", "xc", "zb", "zc", "pa", "po", "ga", "pb", "gc",
             "u_next", "pc", "wo", "v_next")

LANES = 128
BF16_SUBLANES = 16
HALO = 16
TS = 512
W_ROWS = 64
N_STAGE = 3
O_ROWS = 256
VMEM_LIMIT_BYTES = 56 * 1024 * 1024


HALVED_SEGS = (SEG_U, SEG_V, SEG_ZA, SEG_ZB, SEG_ZC, SEG_GA, SEG_GB, SEG_GC)
_GELU_C1 = 2.0 * math.sqrt(2.0 / math.pi)
_GELU_C3 = 8.0 * 0.044715 * math.sqrt(2.0 / math.pi)


def _gelu_of_half(xh):
    t = jnp.tanh(xh * (_GELU_C1 + _GELU_C3 * (xh * xh)))
    return xh + xh * t


def _silu_of_half(zh):
    return zh + zh * jnp.tanh(zh)


def _twice_sigmoid_of_half(gh):
    return jnp.tanh(gh) + 1.0


def _rmsnorm_rows(x, gain):
    ms = jnp.mean(x * x, axis=-1, keepdims=True)
    return x * lax.rsqrt(ms + RMS_EPS) * gain


def _cast_w_in_rows(src, dst_ref, dst_rows):
    for lo, hi in PROJ_SEGS:
        w = src(lo, hi)
        if (lo, hi) in HALVED_SEGS:
            w = w * 0.5
        dst_ref[dst_rows, lo:hi] = w.astype(jnp.bfloat16)


def _w_in_copy(w_in_hbm_ref, stage_ref, sem_ref, layer, i):
    return pltpu.make_async_copy(
        w_in_hbm_ref.at[layer, pl.ds(i * W_ROWS, W_ROWS), :],
        stage_ref.at[i % N_STAGE], sem_ref.at[i % N_STAGE])


def _load_w_in(w_in_hbm_ref, w_in_ref, stage_ref, sem_ref, layer):
    n_blocks = D_MODEL // W_ROWS
    for i in range(N_STAGE):
        _w_in_copy(w_in_hbm_ref, stage_ref, sem_ref, layer, i).start()

    def convert_block(i, carry):
        _w_in_copy(w_in_hbm_ref, stage_ref, sem_ref, layer, i).wait()
        rows = pl.ds(pl.multiple_of(i * W_ROWS, W_ROWS), W_ROWS)
        _cast_w_in_rows(lambda lo, hi: stage_ref[i % N_STAGE, :, lo:hi], w_in_ref, rows)

        @pl.when(i + N_STAGE < n_blocks)
        def _():
            _w_in_copy(w_in_hbm_ref, stage_ref, sem_ref, layer, i + N_STAGE).start()

        return carry

    lax.fori_loop(0, n_blocks, convert_block, 0)


def _load_out_weights(hbm_refs, vmem_refs, stage_ref, sem_ref, layer):
    blocks = [(src, dst, r) for src, dst in zip(hbm_refs, vmem_refs)
              for r in range(0, dst.shape[0], O_ROWS)]

    def copy(i):
        src, _, r = blocks[i]
        return pltpu.make_async_copy(src.at[layer, pl.ds(r, O_ROWS), :],
                                     stage_ref.at[pl.ds((i % 2) * O_ROWS, O_ROWS), :],
                                     sem_ref.at[i % 2])

    copy(0).start()
    copy(1).start()
    for i, (_, dst, r) in enumerate(blocks):
        copy(i).wait()
        w = stage_ref[(i % 2) * O_ROWS:(i % 2 + 1) * O_ROWS, :]
        if dst is vmem_refs[-1]:
            w = w * 0.5
        dst[r:r + O_ROWS, :] = w.astype(jnp.bfloat16)
        if i + 2 < len(blocks):
            copy(i + 2).start()


class _NextLayerWIn:
    def __init__(self, w_in_hbm_ref, w_next_hbm_ref, stage_ref, cast_ref, sem_ref, layer,
                 step, n_steps):
        self.refs = (w_in_hbm_ref, w_next_hbm_ref, stage_ref, cast_ref, sem_ref)
        self.layer, self.step, self.n_steps = layer, step, n_steps
        self.rows = D_MODEL // n_steps

    def _rows_of(self, block):
        return pl.ds(pl.multiple_of(block * self.rows, self.rows), self.rows)

    def _fetch(self, block):
        w_in_hbm_ref, _, stage_ref, _, sem_ref = self.refs
        return pltpu.make_async_copy(w_in_hbm_ref.at[self.layer + 1, self._rows_of(block), :],
                                     stage_ref.at[block % 2, pl.ds(0, self.rows), :],
                                     sem_ref.at[block % 2])

    def _write_back(self, block):
        _, w_next_hbm_ref, _, cast_ref, sem_ref = self.refs
        return pltpu.make_async_copy(cast_ref.at[block % 2],
                                     w_next_hbm_ref.at[self._rows_of(block), :],
                                     sem_ref.at[2 + block % 2])

    def start_first_fetch(self):
        self._fetch(0).start()

    def begin_step(self):
        step = self.step
        self._fetch(step).wait()

        @pl.when(step + 1 < self.n_steps)
        def _():
            self._fetch(step + 1).start()

        @pl.when(step >= 2)
        def _():
            self._write_back(step - 2).wait()

    def cast(self):
        _, _, stage_ref, cast_ref, _ = self.refs
        slot = self.step % 2
        _cast_w_in_rows(lambda lo, hi: stage_ref[slot, 0:self.rows, lo:hi], cast_ref.at[slot],
                        slice(None))

    def end_step(self):
        step = self.step
        self._write_back(step).start()

        @pl.when(step == self.n_steps - 1)
        def _():
            self._write_back(step - 1).wait()
            self._write_back(step).wait()


def _prepare_small_weights(w_s_ref, b_s_ref, w_pool_ref, w_sp_ref, b_sp_ref, w_pool_bd_ref):
    f32 = jnp.float32
    bf16 = jnp.bfloat16
    row = lax.broadcasted_iota(jnp.int32, (CHUNK, CHUNK), 0)
    col = lax.broadcasted_iota(jnp.int32, (CHUNK, CHUNK), 1)
    causal = row >= col
    first_group = col < A_HEAD
    b_t = b_s_ref[...].T
    for jb in range(A_GROUPS // 2):
        w_pair = [jnp.where(causal, w_s_ref[2 * jb + k], 0.0) for k in range(2)]
        w_sp_ref[jb] = jnp.concatenate(w_pair, axis=1).astype(bf16)
        b_pair = [jnp.broadcast_to(b_t[:, 2 * jb + k:2 * jb + k + 1], (CHUNK, LANES))
                  for k in range(2)]
        b_sp_ref[:, jb * LANES:(jb + 1) * LANES] = jnp.where(first_group, b_pair[0], b_pair[1])
    zero = jnp.zeros((C_GROUP, C_GROUP), f32)
    for i in range(len(POOL_WINDOWS) // 2):
        top = jnp.concatenate([w_pool_ref[2 * i], zero], axis=1)
        bottom = jnp.concatenate([zero, w_pool_ref[2 * i + 1]], axis=1)
        w_pool_bd_ref[i] = jnp.concatenate([top, bottom], axis=0).astype(bf16)


def _layer_kernel(*refs, names, layer, steps_per_seq, n_steps, apply_final_norm):
    f32 = jnp.float32
    bf16 = jnp.bfloat16
    r = dict(zip(names, refs, strict=True))
    x_ref, x_next_ref, out_ref = r["x"], r["x_next"], r["out"]
    w_in_ref, sem_ref = r["w_in"], r["sem"]
    w_pa_ref, w_pb_ref, w_pc_ref, w_o_ref = r["w_pa"], r["w_pb"], r["w_pc"], r["w_o"]
    w_sp_ref, b_sp_ref, w_pool_bd_ref = r["w_sp"], r["b_sp"], r["w_pool_bd"]
    h_ref, puv_ref, cx_ref, xc_ref = r["h"], r["puv"], r["cx"], r["xc"]
    norm_g_ref, ln_g_ref, ln_b_ref = r["norm_g"], r["ln_g"], r["ln_b"]
    conv_w_ref, conv_b_ref, pool_scale_ref = r["conv_w"], r["conv_b"], r["pool_scale"]
    final_g_ref = r["final_g"]
    j = pl.program_id(1)
    step = pl.program_id(0) * steps_per_seq + j
    next_w_in = None
    if "w_in_next" in r:
        next_w_in = _NextLayerWIn(r["w_in_f32_hbm"], r["w_in_next"], r["stage"], r["cast"],
                                  r["next_sem"], layer, step, n_steps)

    this_layer = slice(layer, layer + 1)
    norm_g, final_g = norm_g_ref[this_layer, :], final_g_ref[...]
    ln_g, ln_b = ln_g_ref[this_layer, :], ln_b_ref[this_layer, :]
    conv_b, pool_scale = conv_b_ref[this_layer, :], pool_scale_ref[this_layer, :]
    conv_w = [conv_w_ref[layer, k:k + 1, :] for k in range(CONV_WIDTH)]

    @pl.when(j == 0)
    def _():
        cx_ref[0:HALO, :] = jnp.zeros((HALO, B_WIDTH), f32)
        xc_ref[0:HALO, :] = jnp.zeros((HALO, C_WIDTH), f32)

    def proj(seg):
        return jnp.dot(h_ref[...], w_in_ref[:, seg[0]:seg[1]], preferred_element_type=f32)

    @pl.when(step == 0)
    def _():
        if "w_in_bf16_hbm" in r:
            ready_w_in = pltpu.make_async_copy(r["w_in_bf16_hbm"], w_in_ref, sem_ref.at[2])
            ready_w_in.start()
        else:
            _load_w_in(r["w_in_f32_hbm"], w_in_ref, r["stage"], sem_ref, layer)
        _load_out_weights((r["w_pa_hbm"], r["w_pb_hbm"], r["w_pc_hbm"], r["w_o_hbm"]),
                          (w_pa_ref, w_pb_ref, w_pc_ref, w_o_ref), puv_ref, sem_ref, layer)
        _prepare_small_weights(r["w_s"], r["b_s"], r["w_pool"], w_sp_ref, b_sp_ref,
                               w_pool_bd_ref)
        h_ref[...] = _rmsnorm_rows(x_ref[0], norm_g).astype(bf16)
        if "w_in_bf16_hbm" in r:
            ready_w_in.wait()
        puv_ref[...] = proj(SEG_UV)
        if next_w_in is not None:
            next_w_in.start_first_fetch()

    if next_w_in is not None:
        next_w_in.begin_step()
        next_w_in.cast()

    n_chunks = TS // CHUNK
    lane = lax.broadcasted_iota(jnp.int32, (CHUNK, LANES), 1)
    first_group = lane < A_HEAD

    def spatial_mix(vn):
        sg_blocks = []
        for jb in range(A_WIDTH // LANES):
            rhs = []
            for c in range(n_chunks):
                vb = vn[c * CHUNK:(c + 1) * CHUNK, jb * LANES:(jb + 1) * LANES]
                rhs.append(jnp.concatenate([jnp.where(first_group, vb, 0.0),
                                            jnp.where(first_group, 0.0, vb)], axis=0))
            rhs = jnp.concatenate(rhs, axis=1).astype(bf16)
            mixed = jnp.dot(w_sp_ref[jb], rhs, preferred_element_type=f32)
            sg_blocks.append(jnp.concatenate(
                [mixed[:, c * LANES:(c + 1) * LANES] for c in range(n_chunks)], axis=0))
        sg = jnp.concatenate(sg_blocks, axis=1)
        return sg + jnp.concatenate([b_sp_ref[...]] * n_chunks, axis=0)

    def layer_norm(v):
        mu = jnp.mean(v, axis=-1, keepdims=True)
        vc = v - mu
        var = jnp.mean(vc * vc, axis=-1, keepdims=True)
        return vc * lax.rsqrt(var + LN_EPS) * ln_g + ln_b

    def store_xc(e):
        xc_ref[HALO:HALO + TS, :] = proj(SEG_XC)

    def store_cx(e):
        cx_ref[HALO:HALO + TS, :] = e["cg"] * e["xb"]

    def short_conv(e):
        conv = conv_b + conv_w[CONV_WIDTH - 1] * cx_ref[HALO:HALO + TS, :]
        for k in range(CONV_WIDTH - 1):
            back = CONV_WIDTH - 1 - k
            conv = conv + conv_w[k] * cx_ref[HALO - back:HALO - back + TS, :]
        cx_ref[0:HALO, :] = cx_ref[TS:TS + HALO, :]
        return conv

    def pooled_pairs(e):
        t1 = (j * TS + 1 + lax.broadcasted_iota(jnp.int32, (TS, C_GROUP), 0)).astype(f32)
        inv_t1 = 1.0 / t1
        pooled = []
        for gi, w in enumerate(POOL_WINDOWS):
            cols = slice(gi * C_GROUP, (gi + 1) * C_GROUP)
            ext = xc_ref[:, cols]
            win = ext
            span = 1
            while span < w:
                win = win + pltpu.roll(win, span, axis=0)
                span *= 2
            pooled.append(win[HALO:, :] * jnp.maximum(inv_t1, 1.0 / w) - ext[HALO:, :])
        xc_ref[0:HALO, :] = xc_ref[TS:TS + HALO, :]
        return [jnp.concatenate(pooled[2 * half:2 * half + 2], axis=1).astype(bf16)
                for half in range(2)]

    def pool_dots(e):
        groups = [jnp.dot(lhs, w_pool_bd_ref[half], preferred_element_type=f32)
                  for half, lhs in enumerate(e["pooled"])]
        return jnp.concatenate(groups, axis=1) * pool_scale

    def next_tile_u(e):
        h_ref[...] = _rmsnorm_rows(x_next_ref[0], norm_g).astype(bf16)
        puv_ref[:, 0:A_WIDTH] = proj(SEG_U)

    def next_tile_v(e):
        puv_ref[:, A_WIDTH:] = proj(SEG_V)

    def write_out(e):
        y = x_ref[0] + e["wo"]
        if apply_final_norm:
            y = _rmsnorm_rows(y, final_g)
        out_ref[0] = y

    def bdot(lhs, w_ref):
        return jnp.dot(lhs, w_ref[...], preferred_element_type=f32)

    matmul_jobs = {
        "xc": ((), store_xc),
        "za": ((), lambda e: proj(SEG_ZA)),
        "cg": ((), lambda e: proj(SEG_CG)),
        "xb": ((), lambda e: proj(SEG_XB)),
        "bg": ((), lambda e: proj(SEG_BG)),
        "zb": ((), lambda e: proj(SEG_ZB)),
        "zc": ((), lambda e: proj(SEG_ZC)),
        "ga": ((), lambda e: proj(SEG_GA)),
        "gb": ((), lambda e: proj(SEG_GB)),
        "gc": ((), lambda e: proj(SEG_GC)),
        "sp": (("vn",), lambda e: spatial_mix(e["vn"])),
        "pa": (("ya",), lambda e: bdot(e["ya"], w_pa_ref)),
        "pb": (("yb",), lambda e: bdot(e["yb"], w_pb_ref)),
        "po": (("pooled",), pool_dots),
        "pc": (("yc",), lambda e: bdot(e["yc"], w_pc_ref)),
        "u_next": (PROJ_JOBS, next_tile_u),
        "wo": (("merged_c",), lambda e: bdot(e["merged_c"].astype(bf16), w_o_ref)),
        "v_next": (("u_next",), next_tile_v),
    }
    elementwise = (
        ("u", (), lambda e: _gelu_of_half(puv_ref[:, 0:A_WIDTH])),
        ("vn", (), lambda e: layer_norm(_gelu_of_half(puv_ref[:, A_WIDTH:]))),
        ("cx", ("cg", "xb"), store_cx),
        ("ya", ("u", "sp", "za"),
         lambda e: (e["u"] * e["sp"] * _silu_of_half(e["za"])).astype(bf16)),
        ("conv", ("cx",), short_conv),
        ("yb", ("bg", "conv", "zb"),
         lambda e: (e["bg"] * e["conv"] * _silu_of_half(e["zb"])).astype(bf16)),
        ("pooled", ("xc",), pooled_pairs),
        ("yc", ("po", "zc"), lambda e: (e["po"] * _silu_of_half(e["zc"])).astype(bf16)),
        ("merged_a", ("ga", "pa"), lambda e: _twice_sigmoid_of_half(e["ga"]) * e["pa"]),
        ("merged_b", ("merged_a", "gb", "pb"),
         lambda e: e["merged_a"] + _twice_sigmoid_of_half(e["gb"]) * e["pb"]),
        ("gate_c", ("gc",), lambda e: _twice_sigmoid_of_half(e["gc"])),
        ("merged_c", ("merged_b", "gate_c", "pc"),
         lambda e: e["merged_b"] + e["gate_c"] * e["pc"]),
        ("out", ("wo",), write_out),
    )
    env = {}

    def trace_ready_elementwise():
        progress = True
        while progress:
            progress = False
            for name, needs, fn in elementwise:
                if name not in env and all(n in env for n in needs):
                    env[name] = fn(env)
                    progress = True

    assert sorted(MXU_ORDER) == sorted(matmul_jobs)
    for name in MXU_ORDER:
        trace_ready_elementwise()
        needs, fn = matmul_jobs[name]
        assert all(n in env for n in needs), (name, needs)
        env[name] = fn(env)
    trace_ready_elementwise()
    assert len(env) == len(matmul_jobs) + len(elementwise)
    if next_w_in is not None:
        next_w_in.end_step()


def _whole_spec(array):
    zeros = (0,) * array.ndim
    return pl.BlockSpec(array.shape, lambda b, j: zeros, pipeline_mode=pl.Buffered(1))


def _layer_spec(array, layer):
    zeros = (0,) * (array.ndim - 1)
    return pl.BlockSpec((None,) + array.shape[1:], lambda b, j: (layer,) + zeros,
                        pipeline_mode=pl.Buffered(1))


VECTOR_NAMES = ("norm_g", "ln_g", "ln_b", "conv_w", "conv_b", "pool_scale", "final_g")
PER_LAYER_NAMES = ("w_s", "b_s", "w_pool")
OUT_WEIGHT_NAMES = ("w_pa", "w_pb", "w_pc", "w_o")


def _layer(x, w_in, w_in_bf16, vectors, per_layer, out_weights, layer, *, convert_next,
           apply_final_norm):
    bsz, seq, _ = x.shape
    assert seq % TS == 0 and TS % CHUNK == 0 and TS >= HALO
    assert w_in.dtype == jnp.float32 and w_in.shape[1:] == (D_MODEL, IN_TOTAL)
    assert D_MODEL % W_ROWS == 0 and TS == 2 * O_ROWS and N_STAGE >= 3
    steps_per_seq = seq // TS
    n_steps = bsz * steps_per_seq
    bf16 = jnp.bfloat16
    f32 = jnp.float32

    def next_tile(b, j):
        s = jnp.minimum(b * steps_per_seq + j + 1, n_steps - 1)
        return (s // steps_per_seq, s % steps_per_seq, 0)

    tile = (1, TS, D_MODEL)
    any_spec = pl.BlockSpec(memory_space=pl.ANY)
    inputs = [("x", x, pl.BlockSpec(tile, lambda b, j: (b, j, 0))),
              ("x_next", x, pl.BlockSpec(tile, next_tile)),
              ("w_in_f32_hbm", w_in, any_spec)]
    if w_in_bf16 is not None:
        inputs.append(("w_in_bf16_hbm", w_in_bf16, any_spec))
    inputs += [(n, a, _whole_spec(a)) for n, a in zip(VECTOR_NAMES, vectors, strict=True)]
    inputs += [(n, a, _layer_spec(a, layer))
               for n, a in zip(PER_LAYER_NAMES, per_layer, strict=True)]
    inputs += [(n + "_hbm", a, any_spec)
               for n, a in zip(OUT_WEIGHT_NAMES, out_weights, strict=True)]

    outputs = [("out", jax.ShapeDtypeStruct(x.shape, x.dtype),
                pl.BlockSpec(tile, lambda b, j: (b, j, 0)))]
    scratch = [
        ("w_in", pltpu.VMEM((D_MODEL, IN_TOTAL), bf16)),
        ("sem", pltpu.SemaphoreType.DMA((N_STAGE,))),
        ("w_pa", pltpu.VMEM((A_WIDTH, D_MODEL), bf16)),
        ("w_pb", pltpu.VMEM((B_WIDTH, D_MODEL), bf16)),
        ("w_pc", pltpu.VMEM((C_WIDTH, D_MODEL), bf16)),
        ("w_o", pltpu.VMEM((D_MODEL, D_MODEL), bf16)),
        ("w_sp", pltpu.VMEM((A_GROUPS // 2, CHUNK, 2 * CHUNK), bf16)),
        ("b_sp", pltpu.VMEM((CHUNK, A_WIDTH), f32)),
        ("w_pool_bd", pltpu.VMEM((len(POOL_WINDOWS) // 2, 2 * C_GROUP, 2 * C_GROUP), bf16)),
        ("h", pltpu.VMEM((TS, D_MODEL), bf16)),
        ("puv", pltpu.VMEM((TS, 2 * A_WIDTH), f32)),
        ("cx", pltpu.VMEM((TS + HALO, B_WIDTH), f32)),
        ("xc", pltpu.VMEM((TS + HALO, C_WIDTH), f32)),
    ]
    if w_in_bf16 is None or convert_next:
        scratch.append(("stage", pltpu.VMEM((N_STAGE, W_ROWS, IN_TOTAL), f32)))
    if convert_next:
        assert D_MODEL % n_steps == 0 and (D_MODEL // n_steps) % BF16_SUBLANES == 0
        assert D_MODEL // n_steps <= W_ROWS and n_steps >= 2
        outputs.append(("w_in_next", jax.ShapeDtypeStruct((D_MODEL, IN_TOTAL), bf16), any_spec))
        scratch += [("cast", pltpu.VMEM((2, D_MODEL // n_steps, IN_TOTAL), bf16)),
                    ("next_sem", pltpu.SemaphoreType.DMA((4,)))]

    names = tuple(n for n, *_ in inputs + outputs + scratch)
    results = pl.pallas_call(
        functools.partial(_layer_kernel, names=names, layer=layer, steps_per_seq=steps_per_seq,
                          n_steps=n_steps, apply_final_norm=apply_final_norm),
        out_shape=[s for _, s, _ in outputs],
        grid=(bsz, steps_per_seq),
        in_specs=[spec for _, _, spec in inputs],
        out_specs=[spec for _, _, spec in outputs],
        scratch_shapes=[s for _, s in scratch],
        compiler_params=pltpu.CompilerParams(
            dimension_semantics=("arbitrary", "arbitrary"),
            vmem_limit_bytes=VMEM_LIMIT_BYTES,
        ),
        name="hybrid_layer_final" if apply_final_norm else "hybrid_layer",
    )(*[a for _, a, _ in inputs])
    return results[0], (results[1] if convert_next else None)


def kernel(x, norm_g, w_in, ln_g, ln_b, w_s, b_s, conv_w, conv_b, w_pool, pool_scale,
           w_pa, w_pb, w_pc, w_o, final_g):
    vectors = (norm_g, ln_g, ln_b, conv_w, conv_b, pool_scale, final_g[None, :])
    per_layer = (w_s, b_s, w_pool)
    out_weights = (w_pa, w_pb, w_pc, w_o)
    w_in_bf16 = None
    for layer in range(DEPTH):
        last = layer == DEPTH - 1
        x, w_in_bf16 = _layer(x, w_in, w_in_bf16, vectors, per_layer, out_weights, layer,
                              convert_next=not last, apply_final_norm=last)
    return x
```

```python
import functools
import math

import jax
import jax.numpy as jnp
from jax import lax
from jax.experimental import pallas as pl
from jax.experimental.pallas import tpu as pltpu

D_MODEL = 1024
DEPTH = 2
CHUNK = 128
A_GROUPS = 8
A_WIDTH = 512
A_HEAD = A_WIDTH // A_GROUPS
B_WIDTH = 512
CONV_WIDTH = 3
C_WIDTH = 512
POOL_WINDOWS = (2, 4, 8, 16)
C_GROUP = C_WIDTH // len(POOL_WINDOWS)
IN_TOTAL = 3 * A_WIDTH + 4 * B_WIDTH + 2 * C_WIDTH + 3 * D_MODEL
RMS_EPS = 1e-6
LN_EPS = 1e-5

_WIDTHS = [A_WIDTH] * 3 + [B_WIDTH] * 4 + [C_WIDTH] * 2 + [D_MODEL] * 3
_OFFS = [sum(_WIDTHS[:i]) for i in range(len(_WIDTHS) + 1)]
PROJ_SEGS = tuple((_OFFS[i], _OFFS[i + 1]) for i in range(len(_WIDTHS)))
(SEG_U, SEG_V, SEG_ZA, SEG_XB, SEG_BG, SEG_CG, SEG_ZB, SEG_XC, SEG_ZC,
 SEG_GA, SEG_GB, SEG_GC) = PROJ_SEGS
SEG_UV = (SEG_U[0], SEG_V[1])
PROJ_JOBS = ("xc", "za", "cg", "xb", "bg", "zb", "zc", "ga", "gb", "gc")
MXU_ORDER = ("gb", "xc", "za", "cg", "xb", "sp", "bg", "zb", "pa", "zc", "po", "ga", "pb", "gc",
             "u_next", "pc", "wo", "v_next")

LANES = 128
BF16_SUBLANES = 16
HALO = 16
TS = 512
W_ROWS = 64
N_STAGE = 3
O_ROWS = 256
VMEM_LIMIT_BYTES = 56 * 1024 * 1024


HALVED_SEGS = (SEG_U, SEG_V, SEG_ZA, SEG_ZB, SEG_ZC, SEG_GA, SEG_GB, SEG_GC)
_GELU_C1 = 2.0 * math.sqrt(2.0 / math.pi)
_GELU_C3 = 8.0 * 0.044715 * math.sqrt(2.0 / math.pi)


def _gelu_of_half(xh):
    t = jnp.tanh(xh * (_GELU_C1 + _GELU_C3 * (xh * xh)))
    return xh + xh * t


def _silu_of_half(zh):
    return zh + zh * jnp.tanh(zh)


def _twice_sigmoid_of_half(gh):
    return jnp.tanh(gh) + 1.0


def _rmsnorm_rows(x, gain):
    ms = jnp.mean(x * x, axis=-1, keepdims=True)
    return x * lax.rsqrt(ms + RMS_EPS) * gain


def _cast_w_in_rows(src, dst_ref, dst_rows):
    for lo, hi in PROJ_SEGS:
        w = src(lo, hi)
        if (lo, hi) in HALVED_SEGS:
            w = w * 0.5
        dst_ref[dst_rows, lo:hi] = w.astype(jnp.bfloat16)


def _w_in_copy(w_in_hbm_ref, stage_ref, sem_ref, layer, i):
    return pltpu.make_async_copy(
        w_in_hbm_ref.at[layer, pl.ds(i * W_ROWS, W_ROWS), :],
        stage_ref.at[i % N_STAGE], sem_ref.at[i % N_STAGE])


def _load_w_in(w_in_hbm_ref, w_in_ref, stage_ref, sem_ref, layer):
    n_blocks = D_MODEL // W_ROWS
    for i in range(N_STAGE):
        _w_in_copy(w_in_hbm_ref, stage_ref, sem_ref, layer, i).start()

    def convert_block(i, carry):
        _w_in_copy(w_in_hbm_ref, stage_ref, sem_ref, layer, i).wait()
        rows = pl.ds(pl.multiple_of(i * W_ROWS, W_ROWS), W_ROWS)
        _cast_w_in_rows(lambda lo, hi: stage_ref[i % N_STAGE, :, lo:hi], w_in_ref, rows)

        @pl.when(i + N_STAGE < n_blocks)
        def _():
            _w_in_copy(w_in_hbm_ref, stage_ref, sem_ref, layer, i + N_STAGE).start()

        return carry

    lax.fori_loop(0, n_blocks, convert_block, 0)


def _load_out_weights(hbm_refs, vmem_refs, stage_ref, sem_ref, layer):
    blocks = [(src, dst, r) for src, dst in zip(hbm_refs, vmem_refs)
              for r in range(0, dst.shape[0], O_ROWS)]

    def copy(i):
        src, _, r = blocks[i]
        return pltpu.make_async_copy(src.at[layer, pl.ds(r, O_ROWS), :],
                                     stage_ref.at[pl.ds((i % 2) * O_ROWS, O_ROWS), :],
                                     sem_ref.at[i % 2])

    copy(0).start()
    copy(1).start()
    for i, (_, dst, r) in enumerate(blocks):
        copy(i).wait()
        w = stage_ref[(i % 2) * O_ROWS:(i % 2 + 1) * O_ROWS, :]
        if dst is vmem_refs[-1]:
            w = w * 0.5
        dst[r:r + O_ROWS, :] = w.astype(jnp.bfloat16)
        if i + 2 < len(blocks):
            copy(i + 2).start()


class _NextLayerWIn:
    def __init__(self, w_in_hbm_ref, w_next_hbm_ref, stage_ref, cast_ref, sem_ref, layer,
                 step, n_steps):
        self.refs = (w_in_hbm_ref, w_next_hbm_ref, stage_ref, cast_ref, sem_ref)
        self.layer, self.step, self.n_steps = layer, step, n_steps
        self.rows = D_MODEL // n_steps

    def _rows_of(self, block):
        return pl.ds(pl.multiple_of(block * self.rows, self.rows), self.rows)

    def _fetch(self, block):
        w_in_hbm_ref, _, stage_ref, _, sem_ref = self.refs
        return pltpu.make_async_copy(w_in_hbm_ref.at[self.layer + 1, self._rows_of(block), :],
                                     stage_ref.at[block % 2, pl.ds(0, self.rows), :],
                                     sem_ref.at[block % 2])

    def _write_back(self, block):
        _, w_next_hbm_ref, _, cast_ref, sem_ref = self.refs
        return pltpu.make_async_copy(cast_ref.at[block % 2],
                                     w_next_hbm_ref.at[self._rows_of(block), :],
                                     sem_ref.at[2 + block % 2])

    def start_first_fetch(self):
        self._fetch(0).start()

    def begin_step(self):
        step = self.step
        self._fetch(step).wait()

        @pl.when(step + 1 < self.n_steps)
        def _():
            self._fetch(step + 1).start()

        @pl.when(step >= 2)
        def _():
            self._write_back(step - 2).wait()

    def cast(self):
        _, _, stage_ref, cast_ref, _ = self.refs
        slot = self.step % 2
        _cast_w_in_rows(lambda lo, hi: stage_ref[slot, 0:self.rows, lo:hi], cast_ref.at[slot],
                        slice(None))

    def end_step(self):
        step = self.step
        self._write_back(step).start()

        @pl.when(step == self.n_steps - 1)
        def _():
            self._write_back(step - 1).wait()
            self._write_back(step).wait()


def _prepare_small_weights(w_s_ref, b_s_ref, w_pool_ref, w_sp_ref, b_sp_ref, w_pool_bd_ref):
    f32 = jnp.float32
    bf16 = jnp.bfloat16
    row = lax.broadcasted_iota(jnp.int32, (CHUNK, CHUNK), 0)
    col = lax.broadcasted_iota(jnp.int32, (CHUNK, CHUNK), 1)
    causal = row >= col
    first_group = col < A_HEAD
    b_t = b_s_ref[...].T
    for jb in range(A_GROUPS // 2):
        w_pair = [jnp.where(causal, w_s_ref[2 * jb + k], 0.0) for k in range(2)]
        w_sp_ref[jb] = jnp.concatenate(w_pair, axis=1).astype(bf16)
        b_pair = [jnp.broadcast_to(b_t[:, 2 * jb + k:2 * jb + k + 1], (CHUNK, LANES))
                  for k in range(2)]
        b_sp_ref[:, jb * LANES:(jb + 1) * LANES] = jnp.where(first_group, b_pair[0], b_pair[1])
    zero = jnp.zeros((C_GROUP, C_GROUP), f32)
    for i in range(len(POOL_WINDOWS) // 2):
        top = jnp.concatenate([w_pool_ref[2 * i], zero], axis=1)
        bottom = jnp.concatenate([zero, w_pool_ref[2 * i + 1]], axis=1)
        w_pool_bd_ref[i] = jnp.concatenate([top, bottom], axis=0).astype(bf16)


def _layer_kernel(*refs, names, layer, steps_per_seq, n_steps, apply_final_norm):
    f32 = jnp.float32
    bf16 = jnp.bfloat16
    r = dict(zip(names, refs, strict=True))
    x_ref, x_next_ref, out_ref = r["x"], r["x_next"], r["out"]
    w_in_ref, sem_ref = r["w_in"], r["sem"]
    w_pa_ref, w_pb_ref, w_pc_ref, w_o_ref = r["w_pa"], r["w_pb"], r["w_pc"], r["w_o"]
    w_sp_ref, b_sp_ref, w_pool_bd_ref = r["w_sp"], r["b_sp"], r["w_pool_bd"]
    h_ref, puv_ref, cx_ref, xc_ref = r["h"], r["puv"], r["cx"], r["xc"]
    norm_g_ref, ln_g_ref, ln_b_ref = r["norm_g"], r["ln_g"], r["ln_b"]
    conv_w_ref, conv_b_ref, pool_scale_ref = r["conv_w"], r["conv_b"], r["pool_scale"]
    final_g_ref = r["final_g"]
    j = pl.program_id(1)
    step = pl.program_id(0) * steps_per_seq + j
    next_w_in = None
    if "w_in_next" in r:
        next_w_in = _NextLayerWIn(r["w_in_f32_hbm"], r["w_in_next"], r["stage"], r["cast"],
                                  r["next_sem"], layer, step, n_steps)

    this_layer = slice(layer, layer + 1)
    norm_g, final_g = norm_g_ref[this_layer, :], final_g_ref[...]
    ln_g, ln_b = ln_g_ref[this_layer, :], ln_b_ref[this_layer, :]
    conv_b, pool_scale = conv_b_ref[this_layer, :], pool_scale_ref[this_layer, :]
    conv_w = [conv_w_ref[layer, k:k + 1, :] for k in range(CONV_WIDTH)]

    @pl.when(j == 0)
    def _():
        cx_ref[0:HALO, :] = jnp.zeros((HALO, B_WIDTH), f32)
        xc_ref[0:HALO, :] = jnp.zeros((HALO, C_WIDTH), f32)

    def proj(seg):
        return jnp.dot(h_ref[...], w_in_ref[:, seg[0]:seg[1]], preferred_element_type=f32)

    @pl.when(step == 0)
    def _():
        if "w_in_bf16_hbm" in r:
            ready_w_in = pltpu.make_async_copy(r["w_in_bf16_hbm"], w_in_ref, sem_ref.at[2])
            ready_w_in.start()
        else:
            _load_w_in(r["w_in_f32_hbm"], w_in_ref, r["stage"], sem_ref, layer)
        _load_out_weights((r["w_pa_hbm"], r["w_pb_hbm"], r["w_pc_hbm"], r["w_o_hbm"]),
                          (w_pa_ref, w_pb_ref, w_pc_ref, w_o_ref), puv_ref, sem_ref, layer)
        _prepare_small_weights(r["w_s"], r["b_s"], r["w_pool"], w_sp_ref, b_sp_ref,
                               w_pool_bd_ref)
        h_ref[...] = _rmsnorm_rows(x_ref[0], norm_g).astype(bf16)
        if "w_in_bf16_hbm" in r:
            ready_w_in.wait()
        puv_ref[...] = proj(SEG_UV)
        if next_w_in is not None:
            next_w_in.start_first_fetch()

    if next_w_in is not None:
        next_w_in.begin_step()
        next_w_in.cast()

    n_chunks = TS // CHUNK
    lane = lax.broadcasted_iota(jnp.int32, (CHUNK, LANES), 1)
    first_group = lane < A_HEAD

    def spatial_mix(vn):
        sg_blocks = []
        for jb in range(A_WIDTH // LANES):
            rhs = []
            for c in range(n_chunks):
                vb = vn[c * CHUNK:(c + 1) * CHUNK, jb * LANES:(jb + 1) * LANES]
                rhs.append(jnp.concatenate([jnp.where(first_group, vb, 0.0),
                                            jnp.where(first_group, 0.0, vb)], axis=0))
            rhs = jnp.concatenate(rhs, axis=1).astype(bf16)
            mixed = jnp.dot(w_sp_ref[jb], rhs, preferred_element_type=f32)
            sg_blocks.append(jnp.concatenate(
                [mixed[:, c * LANES:(c + 1) * LANES] for c in range(n_chunks)], axis=0))
        sg = jnp.concatenate(sg_blocks, axis=1)
        return sg + jnp.concatenate([b_sp_ref[...]] * n_chunks, axis=0)

    def layer_norm(v):
        mu = jnp.mean(v, axis=-1, keepdims=True)
        vc = v - mu
        var = jnp.mean(vc * vc, axis=-1, keepdims=True)
        return vc * lax.rsqrt(var + LN_EPS) * ln_g + ln_b

    def store_xc(e):
        xc_ref[HALO:HALO + TS, :] = proj(SEG_XC)

    def store_cx(e):
        cx_ref[HALO:HALO + TS, :] = e["cg"] * e["xb"]

    def short_conv(e):
        conv = conv_b + conv_w[CONV_WIDTH - 1] * cx_ref[HALO:HALO + TS, :]
        for k in range(CONV_WIDTH - 1):
            back = CONV_WIDTH - 1 - k
            conv = conv + conv_w[k] * cx_ref[HALO - back:HALO - back + TS, :]
        cx_ref[0:HALO, :] = cx_ref[TS:TS + HALO, :]
        return conv

    def pooled_pairs(e):
        t1 = (j * TS + 1 + lax.broadcasted_iota(jnp.int32, (TS, C_GROUP), 0)).astype(f32)
        inv_t1 = 1.0 / t1
        pooled = []
        for gi, w in enumerate(POOL_WINDOWS):
            cols = slice(gi * C_GROUP, (gi + 1) * C_GROUP)
            ext = xc_ref[:, cols]
            win = ext
            span = 1
            while span < w:
                win = win + pltpu.roll(win, span, axis=0)
                span *= 2
            pooled.append(win[HALO:, :] * jnp.maximum(inv_t1, 1.0 / w) - ext[HALO:, :])
        xc_ref[0:HALO, :] = xc_ref[TS:TS + HALO, :]
        return [jnp.concatenate(pooled[2 * half:2 * half + 2], axis=1).astype(bf16)
                for half in range(2)]

    def pool_dots(e):
        groups = [jnp.dot(lhs, w_pool_bd_ref[half], preferred_element_type=f32)
                  for half, lhs in enumerate(e["pooled"])]
        return jnp.concatenate(groups, axis=1) * pool_scale

    def next_tile_u(e):
        h_ref[...] = _rmsnorm_rows(x_next_ref[0], norm_g).astype(bf16)
        puv_ref[:, 0:A_WIDTH] = proj(SEG_U)

    def next_tile_v(e):
        puv_ref[:, A_WIDTH:] = proj(SEG_V)

    def write_out(e):
        y = x_ref[0] + e["wo"]
        if apply_final_norm:
            y = _rmsnorm_rows(y, final_g)
        out_ref[0] = y

    def bdot(lhs, w_ref):
        return jnp.dot(lhs, w_ref[...], preferred_element_type=f32)

    matmul_jobs = {
        "xc": ((), store_xc),
        "za": ((), lambda e: proj(SEG_ZA)),
        "cg": ((), lambda e: proj(SEG_CG)),
        "xb": ((), lambda e: proj(SEG_XB)),
        "bg": ((), lambda e: proj(SEG_BG)),
        "zb": ((), lambda e: proj(SEG_ZB)),
        "zc": ((), lambda e: proj(SEG_ZC)),
        "ga": ((), lambda e: proj(SEG_GA)),
        "gb": ((), lambda e: proj(SEG_GB)),
        "gc": ((), lambda e: proj(SEG_GC)),
        "sp": (("vn",), lambda e: spatial_mix(e["vn"])),
        "pa": (("ya",), lambda e: bdot(e["ya"], w_pa_ref)),
        "pb": (("yb",), lambda e: bdot(e["yb"], w_pb_ref)),
        "po": (("pooled",), pool_dots),
        "pc": (("yc",), lambda e: bdot(e["yc"], w_pc_ref)),
        "u_next": (PROJ_JOBS, next_tile_u),
        "wo": (("merged_c",), lambda e: bdot(e["merged_c"].astype(bf16), w_o_ref)),
        "v_next": (("u_next",), next_tile_v),
    }
    elementwise = (
        ("u", (), lambda e: _gelu_of_half(puv_ref[:, 0:A_WIDTH])),
        ("vn", (), lambda e: layer_norm(_gelu_of_half(puv_ref[:, A_WIDTH:]))),
        ("cx", ("cg", "xb"), store_cx),
        ("ya", ("u", "sp", "za"),
         lambda e: (e["u"] * e["sp"] * _silu_of_half(e["za"])).astype(bf16)),
        ("conv", ("cx",), short_conv),
        ("yb", ("bg", "conv", "zb"),
         lambda e: (e["bg"] * e["conv"] * _silu_of_half(e["zb"])).astype(bf16)),
        ("pooled", ("xc",), pooled_pairs),
        ("yc", ("po", "zc"), lambda e: (e["po"] * _silu_of_half(e["zc"])).astype(bf16)),
        ("merged_a", ("ga", "pa"), lambda e: _twice_sigmoid_of_half(e["ga"]) * e["pa"]),
        ("merged_b", ("merged_a", "gb", "pb"),
         lambda e: e["merged_a"] + _twice_sigmoid_of_half(e["gb"]) * e["pb"]),
        ("gate_c", ("gc",), lambda e: _twice_sigmoid_of_half(e["gc"])),
        ("merged_c", ("merged_b", "gate_c", "pc"),
         lambda e: e["merged_b"] + e["gate_c"] * e["pc"]),
        ("out", ("wo",), write_out),
    )
    env = {}

    def trace_ready_elementwise():
        progress = True
        while progress:
            progress = False
            for name, needs, fn in elementwise:
                if name not in env and all(n in env for n in needs):
                    env[name] = fn(env)
                    progress = True

    assert sorted(MXU_ORDER) == sorted(matmul_jobs)
    for name in MXU_ORDER:
        trace_ready_elementwise()
        needs, fn = matmul_jobs[name]
        assert all(n in env for n in needs), (name, needs)
        env[name] = fn(env)
    trace_ready_elementwise()
    assert len(env) == len(matmul_jobs) + len(elementwise)
    if next_w_in is not None:
        next_w_in.end_step()


def _whole_spec(array):
    zeros = (0,) * array.ndim
    return pl.BlockSpec(array.shape, lambda b, j: zeros, pipeline_mode=pl.Buffered(1))


def _layer_spec(array, layer):
    zeros = (0,) * (array.ndim - 1)
    return pl.BlockSpec((None,) + array.shape[1:], lambda b, j: (layer,) + zeros,
                        pipeline_mode=pl.Buffered(1))


VECTOR_NAMES = ("norm_g", "ln_g", "ln_b", "conv_w", "conv_b", "pool_scale", "final_g")
PER_LAYER_NAMES = ("w_s", "b_s", "w_pool")
OUT_WEIGHT_NAMES = ("w_pa", "w_pb", "w_pc", "w_o")


def _layer(x, w_in, w_in_bf16, vectors, per_layer, out_weights, layer, *, convert_next,
           apply_final_norm):
    bsz, seq, _ = x.shape
    assert seq % TS == 0 and TS % CHUNK == 0 and TS >= HALO
    assert w_in.dtype == jnp.float32 and w_in.shape[1:] == (D_MODEL, IN_TOTAL)
    assert D_MODEL % W_ROWS == 0 and TS == 2 * O_ROWS and N_STAGE >= 3
    steps_per_seq = seq // TS
    n_steps = bsz * steps_per_seq
    bf16 = jnp.bfloat16
    f32 = jnp.float32

    def next_tile(b, j):
        s = jnp.minimum(b * steps_per_seq + j + 1, n_steps - 1)
        return (s // steps_per_seq, s % steps_per_seq, 0)

    tile = (1, TS, D_MODEL)
    any_spec = pl.BlockSpec(memory_space=pl.ANY)
    inputs = [("x", x, pl.BlockSpec(tile, lambda b, j: (b, j, 0))),
              ("x_next", x, pl.BlockSpec(tile, next_tile)),
              ("w_in_f32_hbm", w_in, any_spec)]
    if w_in_bf16 is not None:
        inputs.append(("w_in_bf16_hbm", w_in_bf16, any_spec))
    inputs += [(n, a, _whole_spec(a)) for n, a in zip(VECTOR_NAMES, vectors, strict=True)]
    inputs += [(n, a, _layer_spec(a, layer))
               for n, a in zip(PER_LAYER_NAMES, per_layer, strict=True)]
    inputs += [(n + "_hbm", a, any_spec)
               for n, a in zip(OUT_WEIGHT_NAMES, out_weights, strict=True)]

    outputs = [("out", jax.ShapeDtypeStruct(x.shape, x.dtype),
                pl.BlockSpec(tile, lambda b, j: (b, j, 0)))]
    scratch = [
        ("w_in", pltpu.VMEM((D_MODEL, IN_TOTAL), bf16)),
        ("sem", pltpu.SemaphoreType.DMA((N_STAGE,))),
        ("w_pa", pltpu.VMEM((A_WIDTH, D_MODEL), bf16)),
        ("w_pb", pltpu.VMEM((B_WIDTH, D_MODEL), bf16)),
        ("w_pc", pltpu.VMEM((C_WIDTH, D_MODEL), bf16)),
        ("w_o", pltpu.VMEM((D_MODEL, D_MODEL), bf16)),
        ("w_sp", pltpu.VMEM((A_GROUPS // 2, CHUNK, 2 * CHUNK), bf16)),
        ("b_sp", pltpu.VMEM((CHUNK, A_WIDTH), f32)),
        ("w_pool_bd", pltpu.VMEM((len(POOL_WINDOWS) // 2, 2 * C_GROUP, 2 * C_GROUP), bf16)),
        ("h", pltpu.VMEM((TS, D_MODEL), bf16)),
        ("puv", pltpu.VMEM((TS, 2 * A_WIDTH), f32)),
        ("cx", pltpu.VMEM((TS + HALO, B_WIDTH), f32)),
        ("xc", pltpu.VMEM((TS + HALO, C_WIDTH), f32)),
    ]
    if w_in_bf16 is None or convert_next:
        scratch.append(("stage", pltpu.VMEM((N_STAGE, W_ROWS, IN_TOTAL), f32)))
    if convert_next:
        assert D_MODEL % n_steps == 0 and (D_MODEL // n_steps) % BF16_SUBLANES == 0
        assert D_MODEL // n_steps <= W_ROWS and n_steps >= 2
        outputs.append(("w_in_next", jax.ShapeDtypeStruct((D_MODEL, IN_TOTAL), bf16), any_spec))
        scratch += [("cast", pltpu.VMEM((2, D_MODEL // n_steps, IN_TOTAL), bf16)),
                    ("next_sem", pltpu.SemaphoreType.DMA((4,)))]

    names = tuple(n for n, *_ in inputs + outputs + scratch)
    results = pl.pallas_call(
        functools.partial(_layer_kernel, names=names, layer=layer, steps_per_seq=steps_per_seq,
                          n_steps=n_steps, apply_final_norm=apply_final_norm),
        out_shape=[s for _, s, _ in outputs],
        grid=(bsz, steps_per_seq),
        in_specs=[spec for _, _, spec in inputs],
        out_specs=[spec for _, _, spec in outputs],
        scratch_shapes=[s for _, s in scratch],
        compiler_params=pltpu.CompilerParams(
            dimension_semantics=("arbitrary", "arbitrary"),
            vmem_limit_bytes=VMEM_LIMIT_BYTES,
        ),
        name="hybrid_layer_final" if apply_final_norm else "hybrid_layer",
    )(*[a for _, a, _ in inputs])
    return results[0], (results[1] if convert_next else None)


def kernel(x, norm_g, w_in, ln_g, ln_b, w_s, b_s, conv_w, conv_b, w_pool, pool_scale,
           w_pa, w_pb, w_pc, w_o, final_g):
    vectors = (norm_g, ln_g, ln_b, conv_w, conv_b, pool_scale, final_g[None, :])
    per_layer = (w_s, b_s, w_pool)
    out_weights = (w_pa, w_pb, w_pc, w_o)
    w_in_bf16 = None
    for layer in range(DEPTH):
        last = layer == DEPTH - 1
        x, w_in_bf16 = _layer(x, w_in, w_in_bf16, vectors, per_layer, out_weights, layer,
                              convert_next=not last, apply_final_norm=last)
    return x
```

```python
import functools
import math

import jax
import jax.numpy as jnp
from jax import lax
from jax.experimental import pallas as pl
from jax.experimental.pallas import tpu as pltpu

D_MODEL = 1024
DEPTH = 2
CHUNK = 128
A_GROUPS = 8
A_WIDTH = 512
A_HEAD = A_WIDTH // A_GROUPS
B_WIDTH = 512
CONV_WIDTH = 3
C_WIDTH = 512
POOL_WINDOWS = (2, 4, 8, 16)
C_GROUP = C_WIDTH // len(POOL_WINDOWS)
IN_TOTAL = 3 * A_WIDTH + 4 * B_WIDTH + 2 * C_WIDTH + 3 * D_MODEL
RMS_EPS = 1e-6
LN_EPS = 1e-5

_WIDTHS = [A_WIDTH] * 3 + [B_WIDTH] * 4 + [C_WIDTH] * 2 + [D_MODEL] * 3
_OFFS = [sum(_WIDTHS[:i]) for i in range(len(_WIDTHS) + 1)]
PROJ_SEGS = tuple((_OFFS[i], _OFFS[i + 1]) for i in range(len(_WIDTHS)))
(SEG_U, SEG_V, SEG_ZA, SEG_XB, SEG_BG, SEG_CG, SEG_ZB, SEG_XC, SEG_ZC,
 SEG_GA, SEG_GB, SEG_GC) = PROJ_SEGS
SEG_UV = (SEG_U[0], SEG_V[1])
PROJ_JOBS = ("xc", "za", "cg", "xb", "bg", "zb", "zc", "ga", "gb", "gc")
MXU_ORDER = ("za", "cg", "gb", "xb", "sp", "bg", "xc", "zb", "pa", "zc", "po", "ga", "pb", "gc",
             "u_next", "pc", "wo", "v_next")

LANES = 128
BF16_SUBLANES = 16
HALO = 16
TS = 512
W_ROWS = 64
N_STAGE = 3
O_ROWS = 256
VMEM_LIMIT_BYTES = 56 * 1024 * 1024


HALVED_SEGS = (SEG_U, SEG_V, SEG_ZA, SEG_ZB, SEG_ZC, SEG_GA, SEG_GB, SEG_GC)
_GELU_C1 = 2.0 * math.sqrt(2.0 / math.pi)
_GELU_C3 = 8.0 * 0.044715 * math.sqrt(2.0 / math.pi)


def _gelu_of_half(xh):
    t = jnp.tanh(xh * (_GELU_C1 + _GELU_C3 * (xh * xh)))
    return xh + xh * t


def _silu_of_half(zh):
    return zh + zh * jnp.tanh(zh)


def _twice_sigmoid_of_half(gh):
    return jnp.tanh(gh) + 1.0


def _rmsnorm_rows(x, gain):
    ms = jnp.mean(x * x, axis=-1, keepdims=True)
    return x * lax.rsqrt(ms + RMS_EPS) * gain


def _cast_w_in_rows(src, dst_ref, dst_rows):
    for lo, hi in PROJ_SEGS:
        w = src(lo, hi)
        if (lo, hi) in HALVED_SEGS:
            w = w * 0.5
        dst_ref[dst_rows, lo:hi] = w.astype(jnp.bfloat16)


def _w_in_copy(w_in_hbm_ref, stage_ref, sem_ref, layer, i):
    return pltpu.make_async_copy(
        w_in_hbm_ref.at[layer, pl.ds(i * W_ROWS, W_ROWS), :],
        stage_ref.at[i % N_STAGE], sem_ref.at[i % N_STAGE])


def _load_w_in(w_in_hbm_ref, w_in_ref, stage_ref, sem_ref, layer):
    n_blocks = D_MODEL // W_ROWS
    for i in range(N_STAGE):
        _w_in_copy(w_in_hbm_ref, stage_ref, sem_ref, layer, i).start()

    def convert_block(i, carry):
        _w_in_copy(w_in_hbm_ref, stage_ref, sem_ref, layer, i).wait()
        rows = pl.ds(pl.multiple_of(i * W_ROWS, W_ROWS), W_ROWS)
        _cast_w_in_rows(lambda lo, hi: stage_ref[i % N_STAGE, :, lo:hi], w_in_ref, rows)

        @pl.when(i + N_STAGE < n_blocks)
        def _():
            _w_in_copy(w_in_hbm_ref, stage_ref, sem_ref, layer, i + N_STAGE).start()

        return carry

    lax.fori_loop(0, n_blocks, convert_block, 0)


def _load_out_weights(hbm_refs, vmem_refs, stage_ref, sem_ref, layer):
    blocks = [(src, dst, r) for src, dst in zip(hbm_refs, vmem_refs)
              for r in range(0, dst.shape[0], O_ROWS)]

    def copy(i):
        src, _, r = blocks[i]
        return pltpu.make_async_copy(src.at[layer, pl.ds(r, O_ROWS), :],
                                     stage_ref.at[pl.ds((i % 2) * O_ROWS, O_ROWS), :],
                                     sem_ref.at[i % 2])

    copy(0).start()
    copy(1).start()
    for i, (_, dst, r) in enumerate(blocks):
        copy(i).wait()
        w = stage_ref[(i % 2) * O_ROWS:(i % 2 + 1) * O_ROWS, :]
        if dst is vmem_refs[-1]:
            w = w * 0.5
        dst[r:r + O_ROWS, :] = w.astype(jnp.bfloat16)
        if i + 2 < len(blocks):
            copy(i + 2).start()


class _NextLayerWIn:
    def __init__(self, w_in_hbm_ref, w_next_hbm_ref, stage_ref, cast_ref, sem_ref, layer,
                 step, n_steps):
        self.refs = (w_in_hbm_ref, w_next_hbm_ref, stage_ref, cast_ref, sem_ref)
        self.layer, self.step, self.n_steps = layer, step, n_steps
        self.rows = D_MODEL // n_steps

    def _rows_of(self, block):
        return pl.ds(pl.multiple_of(block * self.rows, self.rows), self.rows)

    def _fetch(self, block):
        w_in_hbm_ref, _, stage_ref, _, sem_ref = self.refs
        return pltpu.make_async_copy(w_in_hbm_ref.at[self.layer + 1, self._rows_of(block), :],
                                     stage_ref.at[block % 2, pl.ds(0, self.rows), :],
                                     sem_ref.at[block % 2])

    def _write_back(self, block):
        _, w_next_hbm_ref, _, cast_ref, sem_ref = self.refs
        return pltpu.make_async_copy(cast_ref.at[block % 2],
                                     w_next_hbm_ref.at[self._rows_of(block), :],
                                     sem_ref.at[2 + block % 2])

    def start_first_fetch(self):
        self._fetch(0).start()

    def begin_step(self):
        step = self.step
        self._fetch(step).wait()

        @pl.when(step + 1 < self.n_steps)
        def _():
            self._fetch(step + 1).start()

        @pl.when(step >= 2)
        def _():
            self._write_back(step - 2).wait()

    def cast(self):
        _, _, stage_ref, cast_ref, _ = self.refs
        slot = self.step % 2
        _cast_w_in_rows(lambda lo, hi: stage_ref[slot, 0:self.rows, lo:hi], cast_ref.at[slot],
                        slice(None))

    def end_step(self):
        step = self.step
        self._write_back(step).start()

        @pl.when(step == self.n_steps - 1)
        def _():
            self._write_back(step - 1).wait()
            self._write_back(step).wait()


def _prepare_small_weights(w_s_ref, b_s_ref, w_pool_ref, w_sp_ref, b_sp_ref, w_pool_bd_ref):
    f32 = jnp.float32
    bf16 = jnp.bfloat16
    row = lax.broadcasted_iota(jnp.int32, (CHUNK, CHUNK), 0)
    col = lax.broadcasted_iota(jnp.int32, (CHUNK, CHUNK), 1)
    causal = row >= col
    first_group = col < A_HEAD
    b_t = b_s_ref[...].T
    for jb in range(A_GROUPS // 2):
        w_pair = [jnp.where(causal, w_s_ref[2 * jb + k], 0.0) for k in range(2)]
        w_sp_ref[jb] = jnp.concatenate(w_pair, axis=1).astype(bf16)
        b_pair = [jnp.broadcast_to(b_t[:, 2 * jb + k:2 * jb + k + 1], (CHUNK, LANES))
                  for k in range(2)]
        b_sp_ref[:, jb * LANES:(jb + 1) * LANES] = jnp.where(first_group, b_pair[0], b_pair[1])
    zero = jnp.zeros((C_GROUP, C_GROUP), f32)
    for i in range(len(POOL_WINDOWS) // 2):
        top = jnp.concatenate([w_pool_ref[2 * i], zero], axis=1)
        bottom = jnp.concatenate([zero, w_pool_ref[2 * i + 1]], axis=1)
        w_pool_bd_ref[i] = jnp.concatenate([top, bottom], axis=0).astype(bf16)


def _layer_kernel(*refs, names, layer, steps_per_seq, n_steps, apply_final_norm):
    f32 = jnp.float32
    bf16 = jnp.bfloat16
    r = dict(zip(names, refs, strict=True))
    x_ref, x_next_ref, out_ref = r["x"], r["x_next"], r["out"]
    w_in_ref, sem_ref = r["w_in"], r["sem"]
    w_pa_ref, w_pb_ref, w_pc_ref, w_o_ref = r["w_pa"], r["w_pb"], r["w_pc"], r["w_o"]
    w_sp_ref, b_sp_ref, w_pool_bd_ref = r["w_sp"], r["b_sp"], r["w_pool_bd"]
    h_ref, puv_ref, cx_ref, xc_ref = r["h"], r["puv"], r["cx"], r["xc"]
    norm_g_ref, ln_g_ref, ln_b_ref = r["norm_g"], r["ln_g"], r["ln_b"]
    conv_w_ref, conv_b_ref, pool_scale_ref = r["conv_w"], r["conv_b"], r["pool_scale"]
    final_g_ref = r["final_g"]
    j = pl.program_id(1)
    step = pl.program_id(0) * steps_per_seq + j
    next_w_in = None
    if "w_in_next" in r:
        next_w_in = _NextLayerWIn(r["w_in_f32_hbm"], r["w_in_next"], r["stage"], r["cast"],
                                  r["next_sem"], layer, step, n_steps)

    this_layer = slice(layer, layer + 1)
    norm_g, final_g = norm_g_ref[this_layer, :], final_g_ref[...]
    ln_g, ln_b = ln_g_ref[this_layer, :], ln_b_ref[this_layer, :]
    conv_b, pool_scale = conv_b_ref[this_layer, :], pool_scale_ref[this_layer, :]
    conv_w = [conv_w_ref[layer, k:k + 1, :] for k in range(CONV_WIDTH)]

    @pl.when(j == 0)
    def _():
        cx_ref[0:HALO, :] = jnp.zeros((HALO, B_WIDTH), f32)
        xc_ref[0:HALO, :] = jnp.zeros((HALO, C_WIDTH), f32)

    def proj(seg):
        return jnp.dot(h_ref[...], w_in_ref[:, seg[0]:seg[1]], preferred_element_type=f32)

    @pl.when(step == 0)
    def _():
        if "w_in_bf16_hbm" in r:
            ready_w_in = pltpu.make_async_copy(r["w_in_bf16_hbm"], w_in_ref, sem_ref.at[2])
            ready_w_in.start()
        else:
            _load_w_in(r["w_in_f32_hbm"], w_in_ref, r["stage"], sem_ref, layer)
        _load_out_weights((r["w_pa_hbm"], r["w_pb_hbm"], r["w_pc_hbm"], r["w_o_hbm"]),
                          (w_pa_ref, w_pb_ref, w_pc_ref, w_o_ref), puv_ref, sem_ref, layer)
        _prepare_small_weights(r["w_s"], r["b_s"], r["w_pool"], w_sp_ref, b_sp_ref,
                               w_pool_bd_ref)
        h_ref[...] = _rmsnorm_rows(x_ref[0], norm_g).astype(bf16)
        if "w_in_bf16_hbm" in r:
            ready_w_in.wait()
        puv_ref[...] = proj(SEG_UV)
        if next_w_in is not None:
            next_w_in.start_first_fetch()

    if next_w_in is not None:
        next_w_in.begin_step()
        next_w_in.cast()

    n_chunks = TS // CHUNK
    lane = lax.broadcasted_iota(jnp.int32, (CHUNK, LANES), 1)
    first_group = lane < A_HEAD

    def spatial_mix(vn):
        sg_blocks = []
        for jb in range(A_WIDTH // LANES):
            rhs = []
            for c in range(n_chunks):
                vb = vn[c * CHUNK:(c + 1) * CHUNK, jb * LANES:(jb + 1) * LANES]
                rhs.append(jnp.concatenate([jnp.where(first_group, vb, 0.0),
                                            jnp.where(first_group, 0.0, vb)], axis=0))
            rhs = jnp.concatenate(rhs, axis=1).astype(bf16)
            mixed = jnp.dot(w_sp_ref[jb], rhs, preferred_element_type=f32)
            sg_blocks.append(jnp.concatenate(
                [mixed[:, c * LANES:(c + 1) * LANES] for c in range(n_chunks)], axis=0))
        sg = jnp.concatenate(sg_blocks, axis=1)
        return sg + jnp.concatenate([b_sp_ref[...]] * n_chunks, axis=0)

    def layer_norm(v):
        mu = jnp.mean(v, axis=-1, keepdims=True)
        vc = v - mu
        var = jnp.mean(vc * vc, axis=-1, keepdims=True)
        return vc * lax.rsqrt(var + LN_EPS) * ln_g + ln_b

    def store_xc(e):
        xc_ref[HALO:HALO + TS, :] = proj(SEG_XC)

    def store_cx(e):
        cx_ref[HALO:HALO + TS, :] = e["cg"] * e["xb"]

    def short_conv(e):
        conv = conv_b + conv_w[CONV_WIDTH - 1] * cx_ref[HALO:HALO + TS, :]
        for k in range(CONV_WIDTH - 1):
            back = CONV_WIDTH - 1 - k
            conv = conv + conv_w[k] * cx_ref[HALO - back:HALO - back + TS, :]
        cx_ref[0:HALO, :] = cx_ref[TS:TS + HALO, :]
        return conv

    def pooled_pairs(e):
        t1 = (j * TS + 1 + lax.broadcasted_iota(jnp.int32, (TS, C_GROUP), 0)).astype(f32)
        inv_t1 = 1.0 / t1
        pooled = []
        for gi, w in enumerate(POOL_WINDOWS):
            cols = slice(gi * C_GROUP, (gi + 1) * C_GROUP)
            ext = xc_ref[:, cols]
            win = ext
            span = 1
            while span < w:
                win = win + pltpu.roll(win, span, axis=0)
                span *= 2
            pooled.append(win[HALO:, :] * jnp.maximum(inv_t1, 1.0 / w) - ext[HALO:, :])
        xc_ref[0:HALO, :] = xc_ref[TS:TS + HALO, :]
        return [jnp.concatenate(pooled[2 * half:2 * half + 2], axis=1).astype(bf16)
                for half in range(2)]

    def pool_dots(e):
        groups = [jnp.dot(lhs, w_pool_bd_ref[half], preferred_element_type=f32)
                  for half, lhs in enumerate(e["pooled"])]
        return jnp.concatenate(groups, axis=1) * pool_scale

    def next_tile_u(e):
        h_ref[...] = _rmsnorm_rows(x_next_ref[0], norm_g).astype(bf16)
        puv_ref[:, 0:A_WIDTH] = proj(SEG_U)

    def next_tile_v(e):
        puv_ref[:, A_WIDTH:] = proj(SEG_V)

    def write_out(e):
        y = x_ref[0] + e["wo"]
        if apply_final_norm:
            y = _rmsnorm_rows(y, final_g)
        out_ref[0] = y

    def bdot(lhs, w_ref):
        return jnp.dot(lhs, w_ref[...], preferred_element_type=f32)

    matmul_jobs = {
        "xc": ((), store_xc),
        "za": ((), lambda e: proj(SEG_ZA)),
        "cg": ((), lambda e: proj(SEG_CG)),
        "xb": ((), lambda e: proj(SEG_XB)),
        "bg": ((), lambda e: proj(SEG_BG)),
        "zb": ((), lambda e: proj(SEG_ZB)),
        "zc": ((), lambda e: proj(SEG_ZC)),
        "ga": ((), lambda e: proj(SEG_GA)),
        "gb": ((), lambda e: proj(SEG_GB)),
        "gc": ((), lambda e: proj(SEG_GC)),
        "sp": (("vn",), lambda e: spatial_mix(e["vn"])),
        "pa": (("ya",), lambda e: bdot(e["ya"], w_pa_ref)),
        "pb": (("yb",), lambda e: bdot(e["yb"], w_pb_ref)),
        "po": (("pooled",), pool_dots),
        "pc": (("yc",), lambda e: bdot(e["yc"], w_pc_ref)),
        "u_next": (PROJ_JOBS, next_tile_u),
        "wo": (("merged_c",), lambda e: bdot(e["merged_c"].astype(bf16), w_o_ref)),
        "v_next": (("u_next",), next_tile_v),
    }
    elementwise = (
        ("u", (), lambda e: _gelu_of_half(puv_ref[:, 0:A_WIDTH])),
        ("vn", (), lambda e: layer_norm(_gelu_of_half(puv_ref[:, A_WIDTH:]))),
        ("cx", ("cg", "xb"), store_cx),
        ("ya", ("u", "sp", "za"),
         lambda e: (e["u"] * e["sp"] * _silu_of_half(e["za"])).astype(bf16)),
        ("conv", ("cx",), short_conv),
        ("yb", ("bg", "conv", "zb"),
         lambda e: (e["bg"] * e["conv"] * _silu_of_half(e["zb"])).astype(bf16)),
        ("pooled", ("xc",), pooled_pairs),
        ("yc", ("po", "zc"), lambda e: (e["po"] * _silu_of_half(e["zc"])).astype(bf16)),
        ("merged_a", ("ga", "pa"), lambda e: _twice_sigmoid_of_half(e["ga"]) * e["pa"]),
        ("merged_b", ("merged_a", "gb", "pb"),
         lambda e: e["merged_a"] + _twice_sigmoid_of_half(e["gb"]) * e["pb"]),
        ("gate_c", ("gc",), lambda e: _twice_sigmoid_of_half(e["gc"])),
        ("merged_c", ("merged_b", "gate_c", "pc"),
         lambda e: e["merged_b"] + e["gate_c"] * e["pc"]),
        ("out", ("wo",), write_out),
    )
    env = {}

    def trace_ready_elementwise():
        progress = True
        while progress:
            progress = False
            for name, needs, fn in elementwise:
                if name not in env and all(n in env for n in needs):
                    env[name] = fn(env)
                    progress = True

    assert sorted(MXU_ORDER) == sorted(matmul_jobs)
    for name in MXU_ORDER:
        trace_ready_elementwise()
        needs, fn = matmul_jobs[name]
        assert all(n in env for n in needs), (name, needs)
        env[name] = fn(env)
    trace_ready_elementwise()
    assert len(env) == len(matmul_jobs) + len(elementwise)
    if next_w_in is not None:
        next_w_in.end_step()


def _whole_spec(array):
    zeros = (0,) * array.ndim
    return pl.BlockSpec(array.shape, lambda b, j: zeros, pipeline_mode=pl.Buffered(1))


def _layer_spec(array, layer):
    zeros = (0,) * (array.ndim - 1)
    return pl.BlockSpec((None,) + array.shape[1:], lambda b, j: (layer,) + zeros,
                        pipeline_mode=pl.Buffered(1))


VECTOR_NAMES = ("norm_g", "ln_g", "ln_b", "conv_w", "conv_b", "pool_scale", "final_g")
PER_LAYER_NAMES = ("w_s", "b_s", "w_pool")
OUT_WEIGHT_NAMES = ("w_pa", "w_pb", "w_pc", "w_o")


def _layer(x, w_in, w_in_bf16, vectors, per_layer, out_weights, layer, *, convert_next,
           apply_final_norm):
    bsz, seq, _ = x.shape
    assert seq % TS == 0 and TS % CHUNK == 0 and TS >= HALO
    assert w_in.dtype == jnp.float32 and w_in.shape[1:] == (D_MODEL, IN_TOTAL)
    assert D_MODEL % W_ROWS == 0 and TS == 2 * O_ROWS and N_STAGE >= 3
    steps_per_seq = seq // TS
    n_steps = bsz * steps_per_seq
    bf16 = jnp.bfloat16
    f32 = jnp.float32

    def next_tile(b, j):
        s = jnp.minimum(b * steps_per_seq + j + 1, n_steps - 1)
        return (s // steps_per_seq, s % steps_per_seq, 0)

    tile = (1, TS, D_MODEL)
    any_spec = pl.BlockSpec(memory_space=pl.ANY)
    inputs = [("x", x, pl.BlockSpec(tile, lambda b, j: (b, j, 0))),
              ("x_next", x, pl.BlockSpec(tile, next_tile)),
              ("w_in_f32_hbm", w_in, any_spec)]
    if w_in_bf16 is not None:
        inputs.append(("w_in_bf16_hbm", w_in_bf16, any_spec))
    inputs += [(n, a, _whole_spec(a)) for n, a in zip(VECTOR_NAMES, vectors, strict=True)]
    inputs += [(n, a, _layer_spec(a, layer))
               for n, a in zip(PER_LAYER_NAMES, per_layer, strict=True)]
    inputs += [(n + "_hbm", a, any_spec)
               for n, a in zip(OUT_WEIGHT_NAMES, out_weights, strict=True)]

    outputs = [("out", jax.ShapeDtypeStruct(x.shape, x.dtype),
                pl.BlockSpec(tile, lambda b, j: (b, j, 0)))]
    scratch = [
        ("w_in", pltpu.VMEM((D_MODEL, IN_TOTAL), bf16)),
        ("sem", pltpu.SemaphoreType.DMA((N_STAGE,))),
        ("w_pa", pltpu.VMEM((A_WIDTH, D_MODEL), bf16)),
        ("w_pb", pltpu.VMEM((B_WIDTH, D_MODEL), bf16)),
        ("w_pc", pltpu.VMEM((C_WIDTH, D_MODEL), bf16)),
        ("w_o", pltpu.VMEM((D_MODEL, D_MODEL), bf16)),
        ("w_sp", pltpu.VMEM((A_GROUPS // 2, CHUNK, 2 * CHUNK), bf16)),
        ("b_sp", pltpu.VMEM((CHUNK, A_WIDTH), f32)),
        ("w_pool_bd", pltpu.VMEM((len(POOL_WINDOWS) // 2, 2 * C_GROUP, 2 * C_GROUP), bf16)),
        ("h", pltpu.VMEM((TS, D_MODEL), bf16)),
        ("puv", pltpu.VMEM((TS, 2 * A_WIDTH), f32)),
        ("cx", pltpu.VMEM((TS + HALO, B_WIDTH), f32)),
        ("xc", pltpu.VMEM((TS + HALO, C_WIDTH), f32)),
    ]
    if w_in_bf16 is None or convert_next:
        scratch.append(("stage", pltpu.VMEM((N_STAGE, W_ROWS, IN_TOTAL), f32)))
    if convert_next:
        assert D_MODEL % n_steps == 0 and (D_MODEL // n_steps) % BF16_SUBLANES == 0
        assert D_MODEL // n_steps <= W_ROWS and n_steps >= 2
        outputs.append(("w_in_next", jax.ShapeDtypeStruct((D_MODEL, IN_TOTAL), bf16), any_spec))
        scratch += [("cast", pltpu.VMEM((2, D_MODEL // n_steps, IN_TOTAL), bf16)),
                    ("next_sem", pltpu.SemaphoreType.DMA((4,)))]

    names = tuple(n for n, *_ in inputs + outputs + scratch)
    results = pl.pallas_call(
        functools.partial(_layer_kernel, names=names, layer=layer, steps_per_seq=steps_per_seq,
                          n_steps=n_steps, apply_final_norm=apply_final_norm),
        out_shape=[s for _, s, _ in outputs],
        grid=(bsz, steps_per_seq),
        in_specs=[spec for _, _, spec in inputs],
        out_specs=[spec for _, _, spec in outputs],
        scratch_shapes=[s for _, s in scratch],
        compiler_params=pltpu.CompilerParams(
            dimension_semantics=("arbitrary", "arbitrary"),
            vmem_limit_bytes=VMEM_LIMIT_BYTES,
        ),
        name="hybrid_layer_final" if apply_final_norm else "hybrid_layer",
    )(*[a for _, a, _ in inputs])
    return results[0], (results[1] if convert_next else None)


def kernel(x, norm_g, w_in, ln_g, ln_b, w_s, b_s, conv_w, conv_b, w_pool, pool_scale,
           w_pa, w_pb, w_pc, w_o, final_g):
    vectors = (norm_g, ln_g, ln_b, conv_w, conv_b, pool_scale, final_g[None, :])
    per_layer = (w_s, b_s, w_pool)
    out_weights = (w_pa, w_pb, w_pc, w_o)
    w_in_bf16 = None
    for layer in range(DEPTH):
        last = layer == DEPTH - 1
        x, w_in_bf16 = _layer(x, w_in, w_in_bf16, vectors, per_layer, out_weights, layer,
                              convert_next=not last, apply_final_norm=last)
    return x
```
